```python
import jax, jax.numpy as jnp
from jax import lax
import numpy as np

D_MODEL = 1024
BATCH = 8
SEQ = 2048
DEPTH = 2
DEC_BATCH = 128
DEC_SEQ = 4
PAST_LEN = 16384
PAGE_SIZE = 128

D_MIX = D_MODEL
D_A = D_MIX // 2
N_BLK_A = 8
BLK_A = D_A // N_BLK_A
CONV_W = 4
C_RGLRU = 8.0
D_B = D_MIX - D_A
N_HEADS_B = 4
DK_B = D_B // N_HEADS_B
DV_B = D_B // N_HEADS_B
CHUNK = 64
P_IN = 2 * D_A + 4 * D_B
EPS = 1e-6
F_MIN = 1e-30

kernel_name = 'hymba_rglru_hgrn2_step'


def _rmsnorm(x, w):
    xf = x.astype(jnp.float32)
    y = xf * lax.rsqrt(jnp.mean(xf * xf, axis=-1, keepdims=True) + EPS)
    return y * w.astype(jnp.float32)


def _causal_conv(x, buf, w, b):
    L = x.shape[1]
    xp = jnp.concatenate([buf.astype(x.dtype), x], axis=1)
    y = b + sum(xp[:, j:j + L] * w[j] for j in range(CONV_W))
    return y, xp[:, L:]


def _lin_comb(e1, e2):
    a1, b1 = e1
    a2, b2 = e2
    return a1 * a2, a2 * b1 + b2


def _rglru(x, h0, wa, ba, wx, bx, lam):
    B, L, _ = x.shape
    xb = x.reshape(B, L, N_BLK_A, BLK_A)
    r = jax.nn.sigmoid(jnp.einsum('blnc,ncd->blnd', xb, wa).reshape(B, L, D_A) + ba)
    i = jax.nn.sigmoid(jnp.einsum('blnc,ncd->blnd', xb, wx).reshape(B, L, D_A) + bx)
    log_a = -C_RGLRU * jax.nn.softplus(-lam.astype(jnp.float32)) * r.astype(jnp.float32)
    a = jnp.exp(log_a)
    u = jnp.sqrt(jnp.maximum(-jnp.expm1(2.0 * log_a), 0.0)) * (i * x).astype(jnp.float32)
    u = u.at[:, 0].add(a[:, 0] * h0.astype(jnp.float32))
    _, h = lax.associative_scan(_lin_comb, (a, u), axis=1)
    return h, h[:, -1]


def _hgrn2(q, logf, k, v, S0):
    B, L, H, _ = q.shape
    DV = v.shape[-1]
    C = min(CHUNK, L)
    n = -(-L // C)
    pad = n * C - L

    def prep(t):
        t = jnp.pad(t, ((0, 0), (0, pad), (0, 0), (0, 0)))
        return jnp.moveaxis(t.reshape(B, n, C, H, t.shape[-1]), 1, 0)

    mask = jnp.tril(jnp.ones((C, C), bool))[None, :, :, None, None]

    def step(S, inp):
        qc, gc, kc, vc = inp
        b = jnp.cumsum(gc, axis=1)
        diff = b[:, :, None] - b[:, None, :]
        dec = jnp.where(mask, jnp.exp(jnp.where(mask, diff, 0.0)), 0.0)
        A = jnp.einsum('bthk,btshk,bshk->bhts', qc, dec, kc)
        o = (jnp.einsum('bhts,bshv->bthv', A, vc)
             + jnp.einsum('bthk,bhkv->bthv', qc * jnp.exp(b), S))
        bl = b[:, -1]
        S = (jnp.exp(bl)[..., None] * S
             + jnp.einsum('bshk,bshv->bhkv', kc * jnp.exp(bl[:, None] - b), vc))
        return S, o

    S_fin, o = lax.scan(step, S0, (prep(q), prep(logf), prep(k), prep(v)))
    o = jnp.moveaxis(o, 0, 1).reshape(B, n * C, H, DV)[:, :L]
    return o, S_fin


def _layer(x, c, h0, conv0, S0, lb, ada_w, ada_b, pre_w, post_w, w_in, conv_w, conv_b,
           rg_wa, rg_ba, rg_wx, rg_bx, rg_lam, onorm_w, w_out):
    B, L, _ = x.shape
    mod = jax.nn.silu(c) @ ada_w + ada_b
    shift, scale, gate = jnp.split(mod, 3, axis=-1)
    hn = (_rmsnorm(x, pre_w) * (1.0 + scale[:, None]) + shift[:, None]).astype(x.dtype)
    z = hn @ w_in
    xa, ga, q, fr, iv, gb = jnp.split(
        z, [D_A, 2 * D_A, 2 * D_A + D_B, 2 * D_A + 2 * D_B, 2 * D_A + 3 * D_B], axis=-1)
    xc, conv_new = _causal_conv(xa, conv0, conv_w, conv_b)
    ha, h_new = _rglru(xc, h0, rg_wa, rg_ba, rg_wx, rg_bx, rg_lam)
    ya = ha * jax.nn.silu(ga.astype(jnp.float32))
    frh = fr.astype(jnp.float32).reshape(B, L, N_HEADS_B, DK_B)
    lbh = lb.reshape(N_HEADS_B, DK_B)
    f = lbh + (1.0 - lbh) * jax.nn.sigmoid(frh)
    logf = jnp.log(jnp.maximum(f, F_MIN))
    k = (1.0 - lbh) * jax.nn.sigmoid(-frh)
    qh = q.astype(jnp.float32).reshape(B, L, N_HEADS_B, DK_B)
    vh = iv.astype(jnp.float32).reshape(B, L, N_HEADS_B, DV_B)
    ob, S_new = _hgrn2(qh, logf, k, vh, S0.astype(jnp.float32))
    ob = _rmsnorm(ob, onorm_w.reshape(N_HEADS_B, DV_B))
    yb = ob.reshape(B, L, D_B) * jax.nn.silu(gb.astype(jnp.float32))
    m = jnp.concatenate([ya, yb], axis=-1).astype(x.dtype) @ w_out
    x = x + (gate[:, None] * _rmsnorm(m, post_w)).astype(x.dtype)
    return x, h_new.astype(x.dtype), conv_new.astype(x.dtype), S_new.astype(x.dtype)


def setup_inputs(seed: int = 0) -> dict:
    key = jax.random.key(seed)
    ks = jax.random.split(key, 24)
    nrm = jax.random.normal
    f32 = jnp.float32
    u = jax.random.uniform(ks[18], (DEPTH, D_A), f32, minval=0.9, maxval=0.999)
    s = u ** (1.0 / C_RGLRU)
    return {
        'x_prompt': nrm(ks[0], (BATCH, SEQ, D_MODEL), f32),
        'x_sample': nrm(ks[1], (DEC_BATCH, DEC_SEQ, D_MODEL), f32),
        'state_rglru_h': 0.5 * nrm(ks[2], (DEPTH, DEC_BATCH, D_A), f32),
        'state_rglru_conv': nrm(ks[3], (DEPTH, DEC_BATCH, CONV_W - 1, D_A), f32),
        'state_hgrn_S': 0.3 * nrm(ks[4], (DEPTH, DEC_BATCH, N_HEADS_B, DK_B, DV_B), f32),
        'c_prompt': nrm(ks[5], (BATCH, D_MODEL), f32),
        'c_sample': nrm(ks[6], (DEC_BATCH, D_MODEL), f32),
        'ada_w': 0.5 * D_MODEL ** -0.5 * nrm(ks[7], (DEPTH, D_MODEL, 3 * D_MODEL), f32),
        'ada_b': 0.02 * nrm(ks[8], (DEPTH, 3 * D_MODEL), f32),
        'pre_norm_w': 1.0 + 0.1 * nrm(ks[9], (DEPTH, D_MODEL), f32),
        'post_norm_w': 1.0 + 0.1 * nrm(ks[10], (DEPTH, D_MODEL), f32),
        'w_in': D_MODEL ** -0.5 * nrm(ks[11], (DEPTH, D_MODEL, P_IN), f32),
        'conv_w': CONV_W ** -0.5 * nrm(ks[12], (DEPTH, CONV_W, D_A), f32),
        'conv_b': 0.02 * nrm(ks[13], (DEPTH, D_A), f32),
        'rg_wa': BLK_A ** -0.5 * nrm(ks[14], (DEPTH, N_BLK_A, BLK_A, BLK_A), f32),
        'rg_ba': 0.1 * nrm(ks[15], (DEPTH, D_A), f32),
        'rg_wx': BLK_A ** -0.5 * nrm(ks[16], (DEPTH, N_BLK_A, BLK_A, BLK_A), f32),
        'rg_bx': 0.1 * nrm(ks[17], (DEPTH, D_A), f32),
        'rg_lambda': jnp.log(s) - jnp.log1p(-s),
        'hg_lb_logits': nrm(ks[19], (DEPTH, D_B), f32),
        'hg_onorm_w': 1.0 + 0.1 * nrm(ks[20], (DEPTH, D_B), f32),
        'w_out': D_MIX ** -0.5 * nrm(ks[21], (DEPTH, D_MIX, D_MODEL), f32),
    }


def reference(x_prompt, x_sample, state_rglru_h, state_rglru_conv, state_hgrn_S, c_prompt, c_sample,
              ada_w, ada_b, pre_norm_w, post_norm_w, w_in, conv_w, conv_b, rg_wa, rg_ba, rg_wx, rg_bx,
              rg_lambda, hg_lb_logits, hg_onorm_w, w_out):
    probs = jax.nn.softmax(hg_lb_logits.astype(jnp.float32), axis=0)
    lb_all = jnp.cumsum(probs, axis=0) - probs[0]
    bp = x_prompt.shape[0]
    hp0 = jnp.zeros((bp, D_A), jnp.float32)
    cp0 = jnp.zeros((bp, CONV_W - 1, D_A), x_prompt.dtype)
    Sp0 = jnp.zeros((bp, N_HEADS_B, DK_B, DV_B), jnp.float32)
    xp, xs = x_prompt, x_sample
    hp, cvp, Sp, hs, cvs, Ss = [], [], [], [], [], []
    for l in range(DEPTH):
        w = (ada_w[l], ada_b[l], pre_norm_w[l], post_norm_w[l], w_in[l], conv_w[l], conv_b[l],
             rg_wa[l], rg_ba[l], rg_wx[l], rg_bx[l], rg_lambda[l], hg_onorm_w[l], w_out[l])
        xp, h, cv, S = _layer(xp, c_prompt, hp0, cp0, Sp0, lb_all[l], *w)
        hp.append(h); cvp.append(cv); Sp.append(S)
        xs, h, cv, S = _layer(xs, c_sample, state_rglru_h[l], state_rglru_conv[l], state_hgrn_S[l],
                              lb_all[l], *w)
        hs.append(h); cvs.append(cv); Ss.append(S)
    return (xp, xs, jnp.stack(hp), jnp.stack(cvp), jnp.stack(Sp), jnp.stack(hs), jnp.stack(cvs), jnp.stack(Ss))
```

```python
import functools

import jax
import jax.numpy as jnp
from jax import lax
from jax.experimental import pallas as pl
from jax.experimental.pallas import tpu as pltpu

D_MODEL = 1024
DEPTH = 2
D_A = 512
N_BLK_A = 8
BLK_A = 64
CONV_W = 4
C_RGLRU = 8.0
D_B = 512
N_HEADS_B = 4
DK_B = 128
DV_B = 128
P_IN = 2 * D_A + 4 * D_B
EPS = 1e-6
F_MIN = 1e-30

LANES = 128
SUBLANES = 8
VMEM_LIMIT = 56 * 1024 * 1024

BF16 = jnp.bfloat16
F32 = jnp.float32


def _dot(a, b):
    return jnp.dot(a, b, preferred_element_type=F32)


def _dot_nt(a, b):
    return lax.dot_general(a, b, (((1,), (1,)), ((), ())), preferred_element_type=F32)


def _dot_tn(a, b):
    return lax.dot_general(a, b, (((0,), (0,)), ((), ())), preferred_element_type=F32)


def _rms(x, w):
    return x * lax.rsqrt(jnp.mean(x * x, axis=-1, keepdims=True) + EPS) * w


def _silu(x):
    return x * jax.nn.sigmoid(x)


def _mod_kernel(c_ref, w_ref, b_ref, o_ref):
    o_ref[...] = _dot(_silu(c_ref[...]).astype(BF16), w_ref[...]) + b_ref[...]


def _modulation(c_all, ada_w_bf, ada_b):
    nb = c_all.shape[0]
    return pl.pallas_call(
        _mod_kernel,
        grid=(DEPTH, 3),
        in_specs=[
            pl.BlockSpec((nb, D_MODEL), lambda l, k: (0, 0)),
            pl.BlockSpec((None, D_MODEL, D_MODEL), lambda l, k: (l, 0, k)),
            pl.BlockSpec((None, 1, D_MODEL), lambda l, k: (l, 0, k)),
        ],
        out_specs=pl.BlockSpec((None, None, nb, D_MODEL), lambda l, k: (l, k, 0, 0)),
        out_shape=jax.ShapeDtypeStruct((DEPTH, 3, nb, D_MODEL), F32),
        name="adaln_mod",
    )(c_all, ada_w_bf, ada_b)


def _layer_kernel(x_ref, mod_ref, h0_ref, conv0_ref, s0_ref, prew_ref, postw_ref, win_ref, convw_ref,
                  convb_ref, wg_ref, ba_ref, bx_ref, lam_ref, lbl_ref, onw_ref, wout_ref,
                  y_ref, hn_ref, convn_ref, sn_ref,
                  hin_s, xpad_s, ga_s, a_s, u_s, q_s, kk_s, b_s, gb_s, bl_s, hcar_s, st_s,
                  xl_s, qe_s, kd_s, v_s, o_s, m_s,
                  *, layer, bt, tl, tlp, n_levels):
    j = pl.program_id(1)
    nj = pl.num_programs(1)
    rows = tl * bt
    nh = N_HEADS_B

    @pl.when(j == 0)
    def _():
        hcar_s[...] = h0_ref[...]
        xpad_s[pl.ds(0, (CONV_W - 1) * bt), :] = conv0_ref[...].reshape((CONV_W - 1) * bt, D_A)

        def load_state(b, c):
            for hd in range(nh):
                st_s[b, hd] = s0_ref[b, hd].T
            return c
        lax.fori_loop(0, bt, load_state, 0)

    x3 = x_ref[...]
    shift, scale = mod_ref[0], mod_ref[1]
    hin = _rms(x3, prew_ref[...]) * (1.0 + scale)[None] + shift[None]
    hin_s[...] = hin.reshape(rows, D_MODEL).astype(BF16)

    def proj(c0, width):
        return _dot(hin_s[...], win_ref[:, c0:c0 + width])
    xpad_s[pl.ds((CONV_W - 1) * bt, rows), :] = proj(0, D_A)
    ga_s[...] = proj(D_A, D_A)
    q_s[...] = proj(2 * D_A, D_B)
    kk_s[...] = proj(2 * D_A + D_B, D_B)
    iv = proj(2 * D_A + 2 * D_B, D_B)
    for hd in range(nh):
        v_s[hd, pl.ds(0, rows), :] = iv[:, hd * DV_B:(hd + 1) * DV_B]
    gb_s[...] = proj(2 * D_A + 3 * D_B, D_B)

    xc = convb_ref[...] + sum(xpad_s[pl.ds(k * bt, rows), :] * convw_ref[k:k + 1, :] for k in range(CONV_W))
    xcb = xc.astype(BF16)
    for c in range(D_A // LANES):
        g2 = _dot(xcb[:, c * LANES:(c + 1) * LANES], wg_ref[c])
        a_s[:, c * LANES:(c + 1) * LANES] = g2[:, :LANES]
        u_s[:, c * LANES:(c + 1) * LANES] = g2[:, LANES:]
    r = jax.nn.sigmoid(a_s[...] + ba_ref[...])
    i = jax.nn.sigmoid(u_s[...] + bx_ref[...])
    log_a = (-C_RGLRU * jax.nn.softplus(-lam_ref[...])) * r
    a = jnp.exp(log_a)
    a_s[...] = a
    u_s[...] = jnp.sqrt(jnp.maximum(jnp.tanh(-log_a) * (1.0 + a * a), 0.0)) * (i * xc)
    conv_tail = xpad_s[pl.ds(rows, (CONV_W - 1) * bt), :]
    xpad_s[pl.ds(0, (CONV_W - 1) * bt), :] = conv_tail

    def scan_step(t, h):
        sl = pl.ds(pl.multiple_of(t * bt, bt), bt)
        h = a_s[sl, :] * h + u_s[sl, :]
        u_s[sl, :] = h
        return h
    h_last = lax.fori_loop(0, tl, scan_step, hcar_s[...])
    hcar_s[...] = h_last
    m_s[:, :D_A] = (u_s[...] * _silu(ga_s[...])).astype(BF16)

    logits = lbl_ref[...]
    pe = jnp.exp(logits - jnp.max(logits, axis=0, keepdims=True))
    probs = pe / jnp.sum(pe, axis=0, keepdims=True)
    lb = jnp.sum(probs[1:layer + 1], axis=0, keepdims=True) if layer > 0 else jnp.zeros((1, D_B), F32)
    fr = kk_s[...]
    f = lb + (1.0 - lb) * jax.nn.sigmoid(fr)
    b_s[...] = jnp.log(jnp.maximum(f, F_MIN))
    kk_s[...] = (1.0 - lb) * jax.nn.sigmoid(-fr)

    def cum_step(t, acc):
        sl = pl.ds(pl.multiple_of(t * bt, bt), bt)
        acc = acc + b_s[sl, :]
        b_s[sl, :] = acc
        return acc
    bl = lax.fori_loop(0, tl, cum_step, jnp.zeros((bt, D_B), F32))
    ebl = jnp.exp(bl)

    b3 = b_s[...].reshape(tl, bt, D_B)
    q3 = q_s[...].reshape(tl, bt, D_B)
    k3 = kk_s[...].reshape(tl, bt, D_B)
    qe = (q3 * jnp.exp(b3)).reshape(rows, D_B)
    kd = (k3 * jnp.exp(bl[None] - b3)).reshape(rows, D_B)
    if tlp > tl:
        zpad = jnp.zeros(((tlp - tl) * bt, LANES), F32)
    for hd in range(nh):
        cs = slice(hd * DK_B, (hd + 1) * DK_B)
        qe_s[hd, pl.ds(0, rows), :] = qe[:, cs]
        kd_s[hd, pl.ds(0, rows), :] = kd[:, cs]
        bl_s[hd] = ebl[:, cs]
        if tlp > tl:
            qe_s[hd, pl.ds(rows, (tlp - tl) * bt), :] = zpad
            kd_s[hd, pl.ds(rows, (tlp - tl) * bt), :] = zpad
            v_s[hd, pl.ds(rows, (tlp - tl) * bt), :] = zpad
    for lv in range(n_levels):
        h = 1 << lv
        nblk = tl // (2 * h)
        b4 = b3.reshape(nblk, 2 * h, bt, D_B)
        bmid = b4[:, h - 1:h]
        lo = k3.reshape(nblk, 2 * h, bt, D_B)[:, :h] * jnp.exp(bmid - b4[:, :h])
        up = q3.reshape(nblk, 2 * h, bt, D_B)[:, h:] * jnp.exp(b4[:, h:] - bmid)
        xl = jnp.concatenate([lo, up], axis=1).reshape(rows, D_B)
        for hd in range(nh):
            xl_s[lv, hd, pl.ds(0, rows), :] = xl[:, hd * DK_B:(hd + 1) * DK_B]
            if tlp > tl:
                xl_s[lv, hd, pl.ds(rows, (tlp - tl) * bt), :] = zpad

    ti = lax.broadcasted_iota(jnp.int32, (tlp, tlp), 0)
    si = lax.broadcasted_iota(jnp.int32, (tlp, tlp), 1)
    masks = []
    for lv in range(n_levels):
        h = 1 << lv
        masks.append(((ti // (2 * h)) == (si // (2 * h))) & ((ti // h) % 2 == 1) & ((si // h) % 2 == 0))

    def bh_step(b, c):
        sl = pl.ds(b, tlp, stride=bt)
        for hd in range(nh):
            amat = jnp.zeros((tlp, tlp), F32)
            for lv in range(n_levels):
                xv = xl_s[lv, hd, sl, :].astype(BF16)
                amat = amat + jnp.where(masks[lv], _dot_nt(xv, xv), 0.0)
            v = v_s[hd, sl, :].astype(BF16)
            st = st_s[b, hd]
            o = _dot(amat.astype(BF16), v) + _dot_nt(qe_s[hd, sl, :].astype(BF16), st.astype(BF16))
            o_s[hd, sl, :] = o
            st_s[b, hd] = st * bl_s[hd, pl.ds(b, 1), :] + _dot_tn(v, kd_s[hd, sl, :].astype(BF16))
        return c
    lax.fori_loop(0, bt, bh_step, 0)

    for hd in range(nh):
        cs = slice(hd * DK_B, (hd + 1) * DK_B)
        vh = v_s[hd, pl.ds(0, rows), :]
        oh = o_s[hd, pl.ds(0, rows), :] + jnp.sum(q_s[:, cs] * kk_s[:, cs], axis=-1, keepdims=True) * vh
        yb = _rms(oh, onw_ref[:, cs]) * _silu(gb_s[:, cs])
        m_s[:, D_A + hd * DV_B:D_A + (hd + 1) * DV_B] = yb.astype(BF16)

    m = _dot(m_s[...], wout_ref[...])
    gate = mod_ref[2]
    y = x_ref[...] + gate[None] * _rms(m, postw_ref[...]).reshape(tl, bt, D_MODEL)
    y_ref[...] = y

    @pl.when(j == nj - 1)
    def _():
        hn_ref[...] = h_last
        convn_ref[...] = conv_tail.reshape(CONV_W - 1, bt, D_A)

        def store_state(b, c):
            for hd in range(nh):
                sn_ref[b, hd] = st_s[b, hd].T
            return c
        lax.fori_loop(0, bt, store_state, 0)


def _layer(x_tm, mod, h0, conv0_tm, s0, w, *, layer, bt, tl):
    L, B, _ = x_tm.shape
    assert L % tl == 0 and B % bt == 0 and bt % SUBLANES == 0 and tl >= CONV_W - 1
    n_levels = tl.bit_length() - 1
    assert (1 << n_levels) == tl
    tlp = max(tl, SUBLANES)
    rows = tl * bt
    prows = tlp * bt
    grid = (B // bt, L // tl)
    const2 = lambda i, j: (0, 0)
    const3 = lambda i, j: (0, 0, 0)
    in_specs = [
        pl.BlockSpec((tl, bt, D_MODEL), lambda i, j: (j, i, 0)),
        pl.BlockSpec((3, bt, D_MODEL), lambda i, j: (0, i, 0)),
        pl.BlockSpec((bt, D_A), lambda i, j: (i, 0)),
        pl.BlockSpec((CONV_W - 1, bt, D_A), lambda i, j: (0, i, 0)),
        pl.BlockSpec((bt, N_HEADS_B, DK_B, DV_B), lambda i, j: (i, 0, 0, 0)),
        pl.BlockSpec((1, D_MODEL), const2),
        pl.BlockSpec((1, D_MODEL), const2),
        pl.BlockSpec((D_MODEL, P_IN), const2),
        pl.BlockSpec((CONV_W, D_A), const2),
        pl.BlockSpec((1, D_A), const2),
        pl.BlockSpec((D_A // LANES, LANES, 2 * LANES), const3),
        pl.BlockSpec((1, D_A), const2),
        pl.BlockSpec((1, D_A), const2),
        pl.BlockSpec((1, D_A), const2),
        pl.BlockSpec((DEPTH, D_B), const2),
        pl.BlockSpec((1, D_B), const2),
        pl.BlockSpec((D_MODEL, D_MODEL), const2),
    ]
    out_specs = [
        pl.BlockSpec((tl, bt, D_MODEL), lambda i, j: (j, i, 0)),
        pl.BlockSpec((bt, D_A), lambda i, j: (i, 0)),
        pl.BlockSpec((CONV_W - 1, bt, D_A), lambda i, j: (0, i, 0)),
        pl.BlockSpec((bt, N_HEADS_B, DK_B, DV_B), lambda i, j: (i, 0, 0, 0)),
    ]
    out_shape = [
        jax.ShapeDtypeStruct((L, B, D_MODEL), F32),
        jax.ShapeDtypeStruct((B, D_A), F32),
        jax.ShapeDtypeStruct((CONV_W - 1, B, D_A), F32),
        jax.ShapeDtypeStruct((B, N_HEADS_B, DK_B, DV_B), F32),
    ]
    scratch = [
        pltpu.VMEM((rows, D_MODEL), BF16),
        pltpu.VMEM((rows + (CONV_W - 1) * bt, D_A), F32),
        pltpu.VMEM((rows, D_A), F32),
        pltpu.VMEM((rows, D_A), F32),
        pltpu.VMEM((rows, D_A), F32),
        pltpu.VMEM((rows, D_B), F32),
        pltpu.VMEM((rows, D_B), F32),
        pltpu.VMEM((rows, D_B), F32),
        pltpu.VMEM((rows, D_B), F32),
        pltpu.VMEM((N_HEADS_B, bt, DK_B), F32),
        pltpu.VMEM((bt, D_A), F32),
        pltpu.VMEM((bt, N_HEADS_B, DV_B, DK_B), F32),
        pltpu.VMEM((n_levels, N_HEADS_B, prows, LANES), F32),
        pltpu.VMEM((N_HEADS_B, prows, LANES), F32),
        pltpu.VMEM((N_HEADS_B, prows, LANES), F32),
        pltpu.VMEM((N_HEADS_B, prows, LANES), F32),
        pltpu.VMEM((N_HEADS_B, prows, LANES), F32),
        pltpu.VMEM((rows, D_MODEL), BF16),
    ]
    body = functools.partial(_layer_kernel, layer=layer, bt=bt, tl=tl, tlp=tlp, n_levels=n_levels)
    return pl.pallas_call(
        body,
        grid=grid,
        in_specs=in_specs,
        out_specs=out_specs,
        out_shape=out_shape,
        scratch_shapes=scratch,
        compiler_params=pltpu.CompilerParams(
            dimension_semantics=("parallel", "arbitrary"), vmem_limit_bytes=VMEM_LIMIT),
        name=f"layer{layer}_bt{bt}_tl{tl}",
    )(x_tm, mod, h0, conv0_tm, s0, *w)


def _gate_weights(wa, wx):
    per = LANES // BLK_A

    def bd(w):
        w = w.reshape(D_A // LANES, per, BLK_A, BLK_A)
        eye = jnp.eye(per, dtype=w.dtype)
        return jnp.einsum('jpcd,pq->jpcqd', w, eye).reshape(D_A // LANES, LANES, LANES)
    return jnp.concatenate([bd(wa), bd(wx)], axis=-1).astype(BF16)


def kernel(x_prompt, x_sample, state_rglru_h, state_rglru_conv, state_hgrn_S, c_prompt, c_sample, ada_w, ada_b,
           pre_norm_w, post_norm_w, w_in, conv_w, conv_b, rg_wa, rg_ba, rg_wx, rg_bx, rg_lambda, hg_lb_logits,
           hg_onorm_w, w_out):
    bp, bs = x_prompt.shape[0], x_sample.shape[0]
    mod = _modulation(jnp.concatenate([c_prompt, c_sample], axis=0), ada_w.astype(BF16),
                      ada_b.reshape(DEPTH, 1, 3 * D_MODEL))
    xp = jnp.transpose(x_prompt, (1, 0, 2))
    xs = jnp.transpose(x_sample, (1, 0, 2))
    hp0 = jnp.zeros((bp, D_A), F32)
    cp0 = jnp.zeros((CONV_W - 1, bp, D_A), F32)
    sp0 = jnp.zeros((bp, N_HEADS_B, DK_B, DV_B), F32)
    outs_p, outs_s = [], []
    for l in range(DEPTH):
        w = (pre_norm_w[l][None], post_norm_w[l][None], w_in[l].astype(BF16), conv_w[l], conv_b[l][None],
             _gate_weights(rg_wa[l], rg_wx[l]), rg_ba[l][None], rg_bx[l][None], rg_lambda[l][None],
             hg_lb_logits, hg_onorm_w[l][None], w_out[l].astype(BF16))
        xp, h, cv, s = _layer(xp, mod[l, :, :bp], hp0, cp0, sp0, w, layer=l, bt=bp, tl=64)
        outs_p.append((h, jnp.transpose(cv, (1, 0, 2)), s))
        xs, h, cv, s = _layer(xs, mod[l, :, bp:], state_rglru_h[l], jnp.transpose(state_rglru_conv[l], (1, 0, 2)),
                              state_hgrn_S[l], w, layer=l, bt=16, tl=x_sample.shape[1])
        outs_s.append((h, jnp.transpose(cv, (1, 0, 2)), s))
    stack = lambda outs, k: jnp.stack([o[k] for o in outs])
    return (jnp.transpose(xp, (1, 0, 2)), jnp.transpose(xs, (1, 0, 2)),
            stack(outs_p, 0), stack(outs_p, 1), stack(outs_p, 2),
            stack(outs_s, 0), stack(outs_s, 1), stack(outs_s, 2))
```

```python
import functools

import jax
import jax.numpy as jnp
from jax import lax
from jax.experimental import pallas as pl
from jax.experimental.pallas import tpu as pltpu

D_MODEL = 1024
DEPTH = 2
D_A = 512
N_BLK_A = 8
BLK_A = 64
CONV_W = 4
C_RGLRU = 8.0
D_B = 512
N_HEADS_B = 4
DK_B = 128
DV_B = 128
P_IN = 2 * D_A + 4 * D_B
EPS = 1e-6
F_MIN = 1e-30

LANES = 128
SUBLANES = 8
VMEM_LIMIT = 58 * 1024 * 1024

BF16 = jnp.bfloat16
F32 = jnp.float32


def _dot(a, b):
    return jnp.dot(a, b, preferred_element_type=F32)


def _dot_nt(a, b):
    return lax.dot_general(a, b, (((1,), (1,)), ((), ())), preferred_element_type=F32)


def _dot_tn(a, b):
    return lax.dot_general(a, b, (((0,), (0,)), ((), ())), preferred_element_type=F32)


def _rms(x, w):
    return x * lax.rsqrt(jnp.mean(x * x, axis=-1, keepdims=True) + EPS) * w


def _silu(x):
    return x * jax.nn.sigmoid(x)


def _mod_kernel(c_ref, w_ref, b_ref, o_ref):
    o_ref[...] = _dot(_silu(c_ref[...]).astype(BF16), w_ref[...]) + b_ref[...]


def _modulation(c_all, ada_w_bf, ada_b):
    nb = c_all.shape[0]
    return pl.pallas_call(
        _mod_kernel,
        grid=(DEPTH, 3),
        in_specs=[
            pl.BlockSpec((nb, D_MODEL), lambda l, k: (0, 0)),
            pl.BlockSpec((None, D_MODEL, D_MODEL), lambda l, k: (l, 0, k)),
            pl.BlockSpec((None, 1, D_MODEL), lambda l, k: (l, 0, k)),
        ],
        out_specs=pl.BlockSpec((None, None, nb, D_MODEL), lambda l, k: (l, k, 0, 0)),
        out_shape=jax.ShapeDtypeStruct((DEPTH, 3, nb, D_MODEL), F32),
        name="adaln_mod",
    )(c_all, ada_w_bf, ada_b)


def _layer_kernel(x_ref, mod_ref, h0_ref, conv0_ref, s0_ref, prew_ref, postw_ref, win_ref, convw_ref,
                  convb_ref, wg_ref, ba_ref, bx_ref, lam_ref, lbl_ref, onw_ref, wout_ref,
                  y_ref, hn_ref, convn_ref, sn_ref,
                  hin_s, xpad_s, ga_s, a_s, u_s, q_s, kk_s, b_s, gb_s, bl_s, hcar_s, st_s,
                  xl_s, qe_s, kd_s, v_s, o_s, m_s,
                  *, layer, bt, tl, tlp, n_levels):
    j = pl.program_id(1)
    nj = pl.num_programs(1)
    rows = tl * bt
    nh = N_HEADS_B

    @pl.when(j == 0)
    def _():
        hcar_s[...] = h0_ref[...]
        xpad_s[pl.ds(0, (CONV_W - 1) * bt), :] = conv0_ref[...].reshape((CONV_W - 1) * bt, D_A)

        def load_state(b, c):
            for hd in range(nh):
                st_s[b, hd] = s0_ref[b, hd].T
            return c
        lax.fori_loop(0, bt, load_state, 0)

    x3 = x_ref[...]
    shift, scale = mod_ref[0], mod_ref[1]
    hin = _rms(x3, prew_ref[...]) * (1.0 + scale)[None] + shift[None]
    hin_s[...] = hin.reshape(rows, D_MODEL).astype(BF16)

    def proj(c0, width):
        return _dot(hin_s[...], win_ref[:, c0:c0 + width])
    xpad_s[pl.ds((CONV_W - 1) * bt, rows), :] = proj(0, D_A)
    ga_s[...] = proj(D_A, D_A)
    q_s[...] = proj(2 * D_A, D_B)
    kk_s[...] = proj(2 * D_A + D_B, D_B)
    iv = proj(2 * D_A + 2 * D_B, D_B)
    for hd in range(nh):
        v_s[hd, pl.ds(0, rows), :] = iv[:, hd * DV_B:(hd + 1) * DV_B]
    gb_s[...] = proj(2 * D_A + 3 * D_B, D_B)

    xc = convb_ref[...] + sum(xpad_s[pl.ds(k * bt, rows), :] * convw_ref[k:k + 1, :] for k in range(CONV_W))
    xcb = xc.astype(BF16)
    for c in range(D_A // LANES):
        g2 = _dot(xcb[:, c * LANES:(c + 1) * LANES], wg_ref[c])
        a_s[:, c * LANES:(c + 1) * LANES] = g2[:, :LANES]
        u_s[:, c * LANES:(c + 1) * LANES] = g2[:, LANES:]
    r = jax.nn.sigmoid(a_s[...] + ba_ref[...])
    i = jax.nn.sigmoid(u_s[...] + bx_ref[...])
    log_a = (-C_RGLRU * jax.nn.softplus(-lam_ref[...])) * r
    a = jnp.exp(log_a)
    a_s[...] = a
    u_s[...] = jnp.sqrt(jnp.maximum(jnp.tanh(-log_a) * (1.0 + a * a), 0.0)) * (i * xc)
    conv_tail = xpad_s[pl.ds(rows, (CONV_W - 1) * bt), :]
    xpad_s[pl.ds(0, (CONV_W - 1) * bt), :] = conv_tail

    def scan_step(t, h):
        sl = pl.ds(pl.multiple_of(t * bt, bt), bt)
        h = a_s[sl, :] * h + u_s[sl, :]
        u_s[sl, :] = h
        return h
    h_last = lax.fori_loop(0, tl, scan_step, hcar_s[...])
    hcar_s[...] = h_last
    m_s[:, :D_A] = (u_s[...] * _silu(ga_s[...])).astype(BF16)

    logits = lbl_ref[...]
    pe = jnp.exp(logits - jnp.max(logits, axis=0, keepdims=True))
    probs = pe / jnp.sum(pe, axis=0, keepdims=True)
    lb = jnp.sum(probs[1:layer + 1], axis=0, keepdims=True) if layer > 0 else jnp.zeros((1, D_B), F32)
    fr = kk_s[...]
    f = lb + (1.0 - lb) * jax.nn.sigmoid(fr)
    b_s[...] = jnp.log(jnp.maximum(f, F_MIN))
    kk_s[...] = (1.0 - lb) * jax.nn.sigmoid(-fr)

    def cum_step(t, acc):
        sl = pl.ds(pl.multiple_of(t * bt, bt), bt)
        acc = acc + b_s[sl, :]
        b_s[sl, :] = acc
        return acc
    bl = lax.fori_loop(0, tl, cum_step, jnp.zeros((bt, D_B), F32))
    ebl = jnp.exp(bl)

    b3 = b_s[...].reshape(tl, bt, D_B)
    q3 = q_s[...].reshape(tl, bt, D_B)
    k3 = kk_s[...].reshape(tl, bt, D_B)
    qe = (q3 * jnp.exp(b3)).reshape(rows, D_B)
    kd = (k3 * jnp.exp(bl[None] - b3)).reshape(rows, D_B)
    if tlp > tl:
        zpad = jnp.zeros(((tlp - tl) * bt, LANES), F32)
    for hd in range(nh):
        cs = slice(hd * DK_B, (hd + 1) * DK_B)
        qe_s[hd, pl.ds(0, rows), :] = qe[:, cs]
        kd_s[hd, pl.ds(0, rows), :] = kd[:, cs]
        bl_s[hd] = ebl[:, cs]
        if tlp > tl:
            qe_s[hd, pl.ds(rows, (tlp - tl) * bt), :] = zpad
            kd_s[hd, pl.ds(rows, (tlp - tl) * bt), :] = zpad
            v_s[hd, pl.ds(rows, (tlp - tl) * bt), :] = zpad
    for lv in range(n_levels):
        h = 1 << lv
        nblk = tl // (2 * h)
        b4 = b3.reshape(nblk, 2 * h, bt, D_B)
        bmid = b4[:, h - 1:h]
        lo = k3.reshape(nblk, 2 * h, bt, D_B)[:, :h] * jnp.exp(bmid - b4[:, :h])
        up = q3.reshape(nblk, 2 * h, bt, D_B)[:, h:] * jnp.exp(b4[:, h:] - bmid)
        xl = jnp.concatenate([lo, up], axis=1).reshape(rows, D_B)
        for hd in range(nh):
            xl_s[lv, hd, pl.ds(0, rows), :] = xl[:, hd * DK_B:(hd + 1) * DK_B]
            if tlp > tl:
                xl_s[lv, hd, pl.ds(rows, (tlp - tl) * bt), :] = zpad

    ti = lax.broadcasted_iota(jnp.int32, (tlp, tlp), 0)
    si = lax.broadcasted_iota(jnp.int32, (tlp, tlp), 1)
    masks = []
    for lv in range(n_levels):
        h = 1 << lv
        masks.append(((ti // (2 * h)) == (si // (2 * h))) & ((ti // h) % 2 == 1) & ((si // h) % 2 == 0))

    def bh_step(b, c):
        sl = pl.ds(b, tlp, stride=bt)
        for hd in range(nh):
            amat = jnp.zeros((tlp, tlp), F32)
            for lv in range(n_levels):
                xv = xl_s[lv, hd, sl, :].astype(BF16)
                amat = amat + jnp.where(masks[lv], _dot_nt(xv, xv), 0.0)
            v = v_s[hd, sl, :].astype(BF16)
            st = st_s[b, hd]
            o = _dot(amat.astype(BF16), v) + _dot_nt(qe_s[hd, sl, :].astype(BF16), st.astype(BF16))
            o_s[hd, sl, :] = o
            st_s[b, hd] = st * bl_s[hd, pl.ds(b, 1), :] + _dot_tn(v, kd_s[hd, sl, :].astype(BF16))
        return c
    lax.fori_loop(0, bt, bh_step, 0)

    for hd in range(nh):
        cs = slice(hd * DK_B, (hd + 1) * DK_B)
        vh = v_s[hd, pl.ds(0, rows), :]
        oh = o_s[hd, pl.ds(0, rows), :] + jnp.sum(q_s[:, cs] * kk_s[:, cs], axis=-1, keepdims=True) * vh
        yb = _rms(oh, onw_ref[:, cs]) * _silu(gb_s[:, cs])
        m_s[:, D_A + hd * DV_B:D_A + (hd + 1) * DV_B] = yb.astype(BF16)

    m = _dot(m_s[...], wout_ref[...])
    gate = mod_ref[2]
    y = x_ref[...] + gate[None] * _rms(m, postw_ref[...]).reshape(tl, bt, D_MODEL)
    y_ref[...] = y

    @pl.when(j == nj - 1)
    def _():
        hn_ref[...] = h_last
        convn_ref[...] = conv_tail.reshape(CONV_W - 1, bt, D_A)

        def store_state(b, c):
            for hd in range(nh):
                sn_ref[b, hd] = st_s[b, hd].T
            return c
        lax.fori_loop(0, bt, store_state, 0)


def _layer(x_tm, mod, h0, conv0_tm, s0, w, *, layer, bt, tl):
    L, B, _ = x_tm.shape
    assert L % tl == 0 and B % bt == 0 and bt % SUBLANES == 0 and tl >= CONV_W - 1
    n_levels = tl.bit_length() - 1
    assert (1 << n_levels) == tl
    tlp = max(tl, SUBLANES)
    rows = tl * bt
    prows = tlp * bt
    grid = (B // bt, L // tl)
    const2 = lambda i, j: (0, 0)
    const3 = lambda i, j: (0, 0, 0)
    in_specs = [
        pl.BlockSpec((tl, bt, D_MODEL), lambda i, j: (j, i, 0)),
        pl.BlockSpec((3, bt, D_MODEL), lambda i, j: (0, i, 0)),
        pl.BlockSpec((bt, D_A), lambda i, j: (i, 0)),
        pl.BlockSpec((CONV_W - 1, bt, D_A), lambda i, j: (0, i, 0)),
        pl.BlockSpec((bt, N_HEADS_B, DK_B, DV_B), lambda i, j: (i, 0, 0, 0)),
        pl.BlockSpec((1, D_MODEL), const2),
        pl.BlockSpec((1, D_MODEL), const2),
        pl.BlockSpec((D_MODEL, P_IN), const2),
        pl.BlockSpec((CONV_W, D_A), const2),
        pl.BlockSpec((1, D_A), const2),
        pl.BlockSpec((D_A // LANES, LANES, 2 * LANES), const3),
        pl.BlockSpec((1, D_A), const2),
        pl.BlockSpec((1, D_A), const2),
        pl.BlockSpec((1, D_A), const2),
        pl.BlockSpec((DEPTH, D_B), const2),
        pl.BlockSpec((1, D_B), const2),
        pl.BlockSpec((D_MODEL, D_MODEL), const2),
    ]
    out_specs = [
        pl.BlockSpec((tl, bt, D_MODEL), lambda i, j: (j, i, 0)),
        pl.BlockSpec((bt, D_A), lambda i, j: (i, 0)),
        pl.BlockSpec((CONV_W - 1, bt, D_A), lambda i, j: (0, i, 0)),
        pl.BlockSpec((bt, N_HEADS_B, DK_B, DV_B), lambda i, j: (i, 0, 0, 0)),
    ]
    out_shape = [
        jax.ShapeDtypeStruct((L, B, D_MODEL), F32),
        jax.ShapeDtypeStruct((B, D_A), F32),
        jax.ShapeDtypeStruct((CONV_W - 1, B, D_A), F32),
        jax.ShapeDtypeStruct((B, N_HEADS_B, DK_B, DV_B), F32),
    ]
    scratch = [
        pltpu.VMEM((rows, D_MODEL), BF16),
        pltpu.VMEM((rows + (CONV_W - 1) * bt, D_A), F32),
        pltpu.VMEM((rows, D_A), F32),
        pltpu.VMEM((rows, D_A), F32),
        pltpu.VMEM((rows, D_A), F32),
        pltpu.VMEM((rows, D_B), F32),
        pltpu.VMEM((rows, D_B), F32),
        pltpu.VMEM((rows, D_B), F32),
        pltpu.VMEM((rows, D_B), F32),
        pltpu.VMEM((N_HEADS_B, bt, DK_B), F32),
        pltpu.VMEM((bt, D_A), F32),
        pltpu.VMEM((bt, N_HEADS_B, DV_B, DK_B), F32),
        pltpu.VMEM((n_levels, N_HEADS_B, prows, LANES), F32),
        pltpu.VMEM((N_HEADS_B, prows, LANES), F32),
        pltpu.VMEM((N_HEADS_B, prows, LANES), F32),
        pltpu.VMEM((N_HEADS_B, prows, LANES), F32),
        pltpu.VMEM((N_HEADS_B, prows, LANES), F32),
        pltpu.VMEM((rows, D_MODEL), BF16),
    ]
    body = functools.partial(_layer_kernel, layer=layer, bt=bt, tl=tl, tlp=tlp, n_levels=n_levels)
    return pl.pallas_call(
        body,
        grid=grid,
        in_specs=in_specs,
        out_specs=out_specs,
        out_shape=out_shape,
        scratch_shapes=scratch,
        compiler_params=pltpu.CompilerParams(
            dimension_semantics=("parallel", "arbitrary"), vmem_limit_bytes=VMEM_LIMIT),
        name=f"layer{layer}_bt{bt}_tl{tl}",
    )(x_tm, mod, h0, conv0_tm, s0, *w)


CT = SUBLANES


def _sig(x):
    return 0.5 * jnp.tanh(0.5 * x) + 0.5


def _prompt_kernel(x_ref, mod_ref, prew_ref, postw_ref, win_ref, convw_ref, convb_ref, wg_ref, ba_ref, bx_ref,
                   lam_ref, lbl_ref, onw_ref, wout_ref,
                   y_ref, hn_ref, convn_ref, sn_ref,
                   xt_s, hin_s, xpad_s, ccar_s, ga_s, a_s, u_s, xc_s, q_s, kk_s, b_s, gb_s, bl_s, hcar_s, st_s,
                   xl_s, qe_s, kd_s, v_s, o_s,
                   *, bt, tl, n_levels):
    m_s = hin_s
    j = pl.program_id(0)
    nj = pl.num_programs(0)
    rows = tl * bt
    ch = CT * bt
    n_ch = tl // CT
    nh = N_HEADS_B
    nt = D_MODEL // LANES
    tail = (CONV_W - 1) * bt

    @pl.when(j == 0)
    def _():
        hcar_s[...] = jnp.zeros_like(hcar_s)
        ccar_s[...] = jnp.zeros_like(ccar_s)
        st_s[...] = jnp.zeros_like(st_s)

    for c in range(n_ch):
        for jl in range(nt):
            for b in range(bt):
                xt_s[jl, pl.ds(c * ch + b, CT, stride=bt), :] = x_ref[b, c * CT:(c + 1) * CT, jl * LANES:(jl + 1) * LANES]

    ti = lax.broadcasted_iota(jnp.int32, (tl, tl), 0)
    si = lax.broadcasted_iota(jnp.int32, (tl, tl), 1)
    masks = []
    for lv in range(n_levels):
        h = 1 << lv
        masks.append(((ti // (2 * h)) == (si // (2 * h))) & ((ti // h) % 2 == 1) & ((si // h) % 2 == 0))

    def bcast(v):
        return jnp.concatenate([v] * CT, axis=0)

    for l in range(DEPTH):
        shift, scale, gate = mod_ref[l, 0], mod_ref[l, 1], mod_ref[l, 2]
        pm = prew_ref[l] * (1.0 + scale)
        gp = postw_ref[l] * gate

        for c in range(n_ch):
            rs = pl.ds(c * ch, ch)
            ss = sum(jnp.sum(jnp.square(xt_s[jl, rs, :]), axis=-1, keepdims=True) for jl in range(nt))
            inv = lax.rsqrt(ss * (1.0 / D_MODEL) + EPS)
            for jl in range(nt):
                ls = slice(jl * LANES, (jl + 1) * LANES)
                hin_s[rs, ls] = (xt_s[jl, rs, :] * inv * bcast(pm[:, ls]) + bcast(shift[:, ls])).astype(BF16)

        def proj(c0, width):
            return _dot(hin_s[...], win_ref[l, :, c0:c0 + width])
        xpad_s[pl.ds(0, tail), :] = ccar_s[l]
        xpad_s[pl.ds(tail, rows), :] = proj(0, D_A)
        ga_s[...] = proj(D_A, D_A)
        q_s[...] = proj(2 * D_A, D_B)
        kk_s[...] = proj(2 * D_A + D_B, D_B)
        iv = proj(2 * D_A + 2 * D_B, D_B)
        for hd in range(nh):
            v_s[hd] = iv[:, hd * DV_B:(hd + 1) * DV_B]
        gb_s[...] = proj(2 * D_A + 3 * D_B, D_B)

        cw = convw_ref[l]
        for c in range(n_ch):
            xc = convb_ref[l] + sum(xpad_s[pl.ds(c * ch + k * bt, ch), :] * cw[k:k + 1, :] for k in range(CONV_W))
            xc_s[pl.ds(c * ch, ch), :] = xc
        ctail = xpad_s[pl.ds(rows, tail), :]
        ccar_s[l] = ctail
        for c in range(D_A // LANES):
            g2 = _dot(xc_s[:, c * LANES:(c + 1) * LANES].astype(BF16), wg_ref[l, c])
            a_s[:, c * LANES:(c + 1) * LANES] = g2[:, :LANES]
            u_s[:, c * LANES:(c + 1) * LANES] = g2[:, LANES:]
        coef = -C_RGLRU * jax.nn.softplus(-lam_ref[l])
        ba, bx = ba_ref[l], bx_ref[l]

        def scan_chunk(c, h):
            for tp in range(CT // 2):
                yas = []
                for t2 in range(2):
                    sl = pl.ds(pl.multiple_of(c * ch + (2 * tp + t2) * bt, bt), bt)
                    r = _sig(a_s[sl, :] + ba)
                    i = _sig(u_s[sl, :] + bx)
                    log_a = coef * r
                    a = jnp.exp(log_a)
                    u = jnp.sqrt(jnp.maximum(jnp.tanh(-log_a) * (1.0 + a * a), 0.0)) * (i * xc_s[sl, :])
                    h = a * h + u
                    ga = ga_s[sl, :]
                    yas.append(h * (ga * _sig(ga)))
                sl2 = pl.ds(pl.multiple_of(c * ch + 2 * tp * bt, 2 * bt), 2 * bt)
                m_s[sl2, :D_A] = jnp.concatenate(yas, axis=0).astype(BF16)
            return h
        h_last = lax.fori_loop(0, n_ch, scan_chunk, hcar_s[l])
        hcar_s[l] = h_last

        logits = lbl_ref[...]
        pe = jnp.exp(logits - jnp.max(logits, axis=0, keepdims=True))
        probs = pe / jnp.sum(pe, axis=0, keepdims=True)
        lb = jnp.sum(probs[1:l + 1], axis=0, keepdims=True) if l > 0 else jnp.zeros((1, D_B), F32)
        c0 = 0.5 * (1.0 + lb)
        c1 = 0.5 * (1.0 - lb)

        def cum_chunk(c, acc):
            for t in range(CT):
                sl = pl.ds(pl.multiple_of(c * ch + t * bt, bt), bt)
                p = c1 * jnp.tanh(0.5 * kk_s[sl, :])
                acc = acc + jnp.log(jnp.maximum(c0 + p, F_MIN))
                b_s[sl, :] = acc
                kk_s[sl, :] = c1 - p
            return acc
        bl = lax.fori_loop(0, n_ch, cum_chunk, jnp.zeros((bt, D_B), F32))
        ebl = jnp.exp(bl)
        for hd in range(nh):
            bl_s[hd] = ebl[:, hd * DK_B:(hd + 1) * DK_B]

        for c in range(n_ch):
            rs = pl.ds(c * ch, ch)
            for hd in range(nh):
                ls = slice(hd * DK_B, (hd + 1) * DK_B)
                bc, qc, kc = b_s[rs, ls], q_s[rs, ls], kk_s[rs, ls]
                qe_s[hd, rs, :] = qc * jnp.exp(bc)
                kd_s[hd, rs, :] = kc * jnp.exp(bcast(bl[:, ls]) - bc)
                for lv in range(n_levels):
                    h = 1 << lv
                    if 2 * h <= CT:
                        parts = []
                        for t in range(CT):
                            mid = (t // (2 * h)) * 2 * h + h - 1
                            ts, ms = slice(t * bt, (t + 1) * bt), slice(mid * bt, (mid + 1) * bt)
                            if t % (2 * h) >= h:
                                parts.append(qc[ts] * jnp.exp(bc[ts] - bc[ms]))
                            else:
                                parts.append(kc[ts] * jnp.exp(bc[ms] - bc[ts]))
                        xl = jnp.concatenate(parts, axis=0)
                    else:
                        t0 = c * CT
                        mid = (t0 // (2 * h)) * 2 * h + h - 1
                        bmid = bcast(b_s[pl.ds(mid * bt, bt), ls])
                        xl = qc * jnp.exp(bc - bmid) if t0 % (2 * h) >= h else kc * jnp.exp(bmid - bc)
                    xl_s[lv, hd, rs, :] = xl

        for b in range(bt):
            sl = pl.ds(b, tl, stride=bt)
            res = []
            for hd in range(nh):
                amat = jnp.zeros((tl, tl), F32)
                for lv in range(n_levels):
                    xv = xl_s[lv, hd, sl, :].astype(BF16)
                    amat = amat + jnp.where(masks[lv], _dot_nt(xv, xv), 0.0)
                v = v_s[hd, sl, :].astype(BF16)
                st = st_s[l, b, hd]
                o = _dot(amat.astype(BF16), v) + _dot_nt(qe_s[hd, sl, :].astype(BF16), st.astype(BF16))
                st_new = st * bl_s[hd, b:b + 1, :] + _dot_tn(v, kd_s[hd, sl, :].astype(BF16))
                res.append((o, st_new))
            for hd in range(nh):
                o_s[hd, sl, :] = res[hd][0]
                st_s[l, b, hd] = res[hd][1]

        for c in range(n_ch):
            rs = pl.ds(c * ch, ch)
            for hd in range(nh):
                ls = slice(hd * DK_B, (hd + 1) * DK_B)
                dsum = jnp.sum(q_s[rs, ls] * kk_s[rs, ls], axis=-1, keepdims=True)
                oh = o_s[hd, rs, :] + dsum * v_s[hd, rs, :]
                gb = gb_s[rs, ls]
                yb = _rms(oh, onw_ref[l][:, ls]) * (gb * _sig(gb))
                m_s[rs, D_A + hd * DV_B:D_A + (hd + 1) * DV_B] = yb.astype(BF16)

        mo = _dot(m_s[...], wout_ref[l])
        for jl in range(nt):
            xl_s[jl // nh, jl % nh] = mo[:, jl * LANES:(jl + 1) * LANES]
        for c in range(n_ch):
            rs = pl.ds(c * ch, ch)
            ss = sum(jnp.sum(jnp.square(xl_s[jl // nh, jl % nh, rs, :]), axis=-1, keepdims=True) for jl in range(nt))
            inv = lax.rsqrt(ss * (1.0 / D_MODEL) + EPS)
            for jl in range(nt):
                ls = slice(jl * LANES, (jl + 1) * LANES)
                xt_s[jl, rs, :] = xt_s[jl, rs, :] + xl_s[jl // nh, jl % nh, rs, :] * inv * bcast(gp[:, ls])

        @pl.when(j == nj - 1)
        def _():
            hn_ref[l] = h_last
            convn_ref[l] = ctail.reshape(CONV_W - 1, bt, D_A)
            for b in range(bt):
                for hd in range(nh):
                    sn_ref[l, b, hd] = st_s[l, b, hd].T

    for c in range(n_ch):
        for jl in range(nt):
            for b in range(bt):
                y_ref[b, c * CT:(c + 1) * CT, jl * LANES:(jl + 1) * LANES] = xt_s[jl, pl.ds(c * ch + b, CT, stride=bt), :]


def _prompt(x, mod, w, *, tl):
    B, L, _ = x.shape
    bt = B
    assert bt == SUBLANES and L % tl == 0 and tl % CT == 0
    n_levels = tl.bit_length() - 1
    assert (1 << n_levels) == tl and n_levels * N_HEADS_B >= D_MODEL // LANES
    rows = tl * bt
    tail = (CONV_W - 1) * bt

    def const(shape):
        nd = len(shape)
        return pl.BlockSpec(shape, lambda j: (0,) * nd, pipeline_mode=pl.Buffered(1))
    in_specs = [pl.BlockSpec((bt, tl, D_MODEL), lambda j: (0, j, 0)), const(mod.shape)] + [const(a.shape) for a in w]
    out_specs = [
        pl.BlockSpec((bt, tl, D_MODEL), lambda j: (0, j, 0)),
        pl.BlockSpec((DEPTH, bt, D_A), lambda j: (0, 0, 0)),
        pl.BlockSpec((DEPTH, CONV_W - 1, bt, D_A), lambda j: (0, 0, 0, 0)),
        pl.BlockSpec((DEPTH, bt, N_HEADS_B, DK_B, DV_B), lambda j: (0, 0, 0, 0, 0)),
    ]
    out_shape = [
        jax.ShapeDtypeStruct((B, L, D_MODEL), F32),
        jax.ShapeDtypeStruct((DEPTH, B, D_A), F32),
        jax.ShapeDtypeStruct((DEPTH, CONV_W - 1, B, D_A), F32),
        jax.ShapeDtypeStruct((DEPTH, B, N_HEADS_B, DK_B, DV_B), F32),
    ]
    slab = lambda n: pltpu.VMEM((n, rows, LANES), F32)
    wide = lambda: pltpu.VMEM((rows, D_A), F32)
    scratch = [
        slab(D_MODEL // LANES),
        pltpu.VMEM((rows, D_MODEL), BF16),
        pltpu.VMEM((rows + tail, D_A), F32),
        pltpu.VMEM((DEPTH, tail, D_A), F32),
        wide(), wide(), wide(), wide(),
        wide(), wide(), wide(), wide(),
        pltpu.VMEM((N_HEADS_B, bt, DK_B), F32),
        pltpu.VMEM((DEPTH, bt, D_A), F32),
        pltpu.VMEM((DEPTH, bt, N_HEADS_B, DV_B, DK_B), F32),
        pltpu.VMEM((n_levels, N_HEADS_B, rows, LANES), F32),
        slab(N_HEADS_B), slab(N_HEADS_B), slab(N_HEADS_B), slab(N_HEADS_B),
    ]
    body = functools.partial(_prompt_kernel, bt=bt, tl=tl, n_levels=n_levels)
    return pl.pallas_call(
        body,
        grid=(L // tl,),
        in_specs=in_specs,
        out_specs=out_specs,
        out_shape=out_shape,
        scratch_shapes=scratch,
        compiler_params=pltpu.CompilerParams(dimension_semantics=("arbitrary",), vmem_limit_bytes=VMEM_LIMIT),
        name="prompt_layers",
    )(x, mod, *w)


def _gate_weights(wa, wx):
    per = LANES // BLK_A

    def bd(w):
        w = w.reshape(D_A // LANES, per, BLK_A, BLK_A)
        eye = jnp.eye(per, dtype=w.dtype)
        return jnp.einsum('jpcd,pq->jpcqd', w, eye).reshape(D_A // LANES, LANES, LANES)
    return jnp.concatenate([bd(wa), bd(wx)], axis=-1).astype(BF16)


def kernel(x_prompt, x_sample, state_rglru_h, state_rglru_conv, state_hgrn_S, c_prompt, c_sample, ada_w, ada_b,
           pre_norm_w, post_norm_w, w_in, conv_w, conv_b, rg_wa, rg_ba, rg_wx, rg_bx, rg_lambda, hg_lb_logits,
           hg_onorm_w, w_out):
    bp, bs = x_prompt.shape[0], x_sample.shape[0]
    mod = _modulation(jnp.concatenate([c_prompt, c_sample], axis=0), ada_w.astype(BF16),
                      ada_b.reshape(DEPTH, 1, 3 * D_MODEL))
    w_in_bf, w_out_bf = w_in.astype(BF16), w_out.astype(BF16)
    wg = jnp.stack([_gate_weights(rg_wa[l], rg_wx[l]) for l in range(DEPTH)])
    row = lambda a: a[:, None, :]
    wp = (row(pre_norm_w), row(post_norm_w), w_in_bf, conv_w, row(conv_b), wg, row(rg_ba), row(rg_bx),
          row(rg_lambda), hg_lb_logits, row(hg_onorm_w), w_out_bf)
    yp, hp, cvp, sp = _prompt(x_prompt, mod[:, :, :bp], wp, tl=64)

    xs = jnp.transpose(x_sample, (1, 0, 2))
    outs_s = []
    for l in range(DEPTH):
        w = (pre_norm_w[l][None], post_norm_w[l][None], w_in_bf[l], conv_w[l], conv_b[l][None],
             wg[l], rg_ba[l][None], rg_bx[l][None], rg_lambda[l][None],
             hg_lb_logits, hg_onorm_w[l][None], w_out_bf[l])
        xs, h, cv, s = _layer(xs, mod[l, :, bp:], state_rglru_h[l], jnp.transpose(state_rglru_conv[l], (1, 0, 2)),
                              state_hgrn_S[l], w, layer=l, bt=16, tl=x_sample.shape[1])
        outs_s.append((h, jnp.transpose(cv, (1, 0, 2)), s))
    stack = lambda outs, k: jnp.stack([o[k] for o in outs])
    return (yp, jnp.transpose(xs, (1, 0, 2)), hp, jnp.transpose(cvp, (0, 2, 1, 3)), sp,
            stack(outs_s, 0), stack(outs_s, 1), stack(outs_s, 2))
```

```python
import functools

import jax
import jax.numpy as jnp
from jax import lax
from jax.experimental import pallas as pl
from jax.experimental.pallas import tpu as pltpu

D_MODEL = 1024
DEPTH = 2
D_A = 512
N_BLK_A = 8
BLK_A = 64
CONV_W = 4
C_RGLRU = 8.0
D_B = 512
N_HEADS_B = 4
DK_B = 128
DV_B = 128
P_IN = 2 * D_A + 4 * D_B
EPS = 1e-6
F_MIN = 1e-30

LANES = 128
SUBLANES = 8
VMEM_LIMIT = 60 * 1024 * 1024

BF16 = jnp.bfloat16
F32 = jnp.float32


def _dot(a, b):
    return jnp.dot(a, b, preferred_element_type=F32)


def _dot_nt(a, b):
    return lax.dot_general(a, b, (((1,), (1,)), ((), ())), preferred_element_type=F32)


def _dot_tn(a, b):
    return lax.dot_general(a, b, (((0,), (0,)), ((), ())), preferred_element_type=F32)


def _rms(x, w):
    return x * lax.rsqrt(jnp.mean(x * x, axis=-1, keepdims=True) + EPS) * w


def _silu(x):
    return x * jax.nn.sigmoid(x)


def _mod_kernel(c_ref, w_ref, b_ref, o_ref):
    o_ref[...] = _dot(_silu(c_ref[...]).astype(BF16), w_ref[...]) + b_ref[...]


def _modulation(c_all, ada_w_bf, ada_b):
    nb = c_all.shape[0]
    return pl.pallas_call(
        _mod_kernel,
        grid=(DEPTH, 3),
        in_specs=[
            pl.BlockSpec((nb, D_MODEL), lambda l, k: (0, 0)),
            pl.BlockSpec((None, D_MODEL, D_MODEL), lambda l, k: (l, 0, k)),
            pl.BlockSpec((None, 1, D_MODEL), lambda l, k: (l, 0, k)),
        ],
        out_specs=pl.BlockSpec((None, None, nb, D_MODEL), lambda l, k: (l, k, 0, 0)),
        out_shape=jax.ShapeDtypeStruct((DEPTH, 3, nb, D_MODEL), F32),
        name="adaln_mod",
    )(c_all, ada_w_bf, ada_b)


def _layer_kernel(x_ref, mod_ref, h0_ref, conv0_ref, s0_ref, prew_ref, postw_ref, win_ref, convw_ref,
                  convb_ref, wg_ref, ba_ref, bx_ref, lam_ref, lbl_ref, onw_ref, wout_ref,
                  y_ref, hn_ref, convn_ref, sn_ref,
                  hin_s, xpad_s, ga_s, a_s, u_s, q_s, kk_s, b_s, gb_s, bl_s, hcar_s, st_s,
                  xl_s, qe_s, kd_s, v_s, o_s, m_s,
                  *, layer, bt, tl, tlp, n_levels):
    j = pl.program_id(1)
    nj = pl.num_programs(1)
    rows = tl * bt
    nh = N_HEADS_B

    @pl.when(j == 0)
    def _():
        hcar_s[...] = h0_ref[...]
        xpad_s[pl.ds(0, (CONV_W - 1) * bt), :] = conv0_ref[...].reshape((CONV_W - 1) * bt, D_A)

        def load_state(b, c):
            for hd in range(nh):
                st_s[b, hd] = s0_ref[b, hd].T
            return c
        lax.fori_loop(0, bt, load_state, 0)

    x3 = x_ref[...]
    shift, scale = mod_ref[0], mod_ref[1]
    hin = _rms(x3, prew_ref[...]) * (1.0 + scale)[None] + shift[None]
    hin_s[...] = hin.reshape(rows, D_MODEL).astype(BF16)

    def proj(c0, width):
        return _dot(hin_s[...], win_ref[:, c0:c0 + width])
    xpad_s[pl.ds((CONV_W - 1) * bt, rows), :] = proj(0, D_A)
    ga_s[...] = proj(D_A, D_A)
    q_s[...] = proj(2 * D_A, D_B)
    kk_s[...] = proj(2 * D_A + D_B, D_B)
    iv = proj(2 * D_A + 2 * D_B, D_B)
    for hd in range(nh):
        v_s[hd, pl.ds(0, rows), :] = iv[:, hd * DV_B:(hd + 1) * DV_B]
    gb_s[...] = proj(2 * D_A + 3 * D_B, D_B)

    xc = convb_ref[...] + sum(xpad_s[pl.ds(k * bt, rows), :] * convw_ref[k:k + 1, :] for k in range(CONV_W))
    xcb = xc.astype(BF16)
    for c in range(D_A // LANES):
        g2 = _dot(xcb[:, c * LANES:(c + 1) * LANES], wg_ref[c])
        a_s[:, c * LANES:(c + 1) * LANES] = g2[:, :LANES]
        u_s[:, c * LANES:(c + 1) * LANES] = g2[:, LANES:]
    r = jax.nn.sigmoid(a_s[...] + ba_ref[...])
    i = jax.nn.sigmoid(u_s[...] + bx_ref[...])
    log_a = (-C_RGLRU * jax.nn.softplus(-lam_ref[...])) * r
    a = jnp.exp(log_a)
    a_s[...] = a
    u_s[...] = jnp.sqrt(jnp.maximum(jnp.tanh(-log_a) * (1.0 + a * a), 0.0)) * (i * xc)
    conv_tail = xpad_s[pl.ds(rows, (CONV_W - 1) * bt), :]
    xpad_s[pl.ds(0, (CONV_W - 1) * bt), :] = conv_tail

    def scan_step(t, h):
        sl = pl.ds(pl.multiple_of(t * bt, bt), bt)
        h = a_s[sl, :] * h + u_s[sl, :]
        u_s[sl, :] = h
        return h
    h_last = lax.fori_loop(0, tl, scan_step, hcar_s[...])
    hcar_s[...] = h_last
    m_s[:, :D_A] = (u_s[...] * _silu(ga_s[...])).astype(BF16)

    logits = lbl_ref[...]
    pe = jnp.exp(logits - jnp.max(logits, axis=0, keepdims=True))
    probs = pe / jnp.sum(pe, axis=0, keepdims=True)
    lb = jnp.sum(probs[1:layer + 1], axis=0, keepdims=True) if layer > 0 else jnp.zeros((1, D_B), F32)
    fr = kk_s[...]
    f = lb + (1.0 - lb) * jax.nn.sigmoid(fr)
    b_s[...] = jnp.log(jnp.maximum(f, F_MIN))
    kk_s[...] = (1.0 - lb) * jax.nn.sigmoid(-fr)

    def cum_step(t, acc):
        sl = pl.ds(pl.multiple_of(t * bt, bt), bt)
        acc = acc + b_s[sl, :]
        b_s[sl, :] = acc
        return acc
    bl = lax.fori_loop(0, tl, cum_step, jnp.zeros((bt, D_B), F32))
    ebl = jnp.exp(bl)

    b3 = b_s[...].reshape(tl, bt, D_B)
    q3 = q_s[...].reshape(tl, bt, D_B)
    k3 = kk_s[...].reshape(tl, bt, D_B)
    qe = (q3 * jnp.exp(b3)).reshape(rows, D_B)
    kd = (k3 * jnp.exp(bl[None] - b3)).reshape(rows, D_B)
    if tlp > tl:
        zpad = jnp.zeros(((tlp - tl) * bt, LANES), F32)
    for hd in range(nh):
        cs = slice(hd * DK_B, (hd + 1) * DK_B)
        qe_s[hd, pl.ds(0, rows), :] = qe[:, cs]
        kd_s[hd, pl.ds(0, rows), :] = kd[:, cs]
        bl_s[hd] = ebl[:, cs]
        if tlp > tl:
            qe_s[hd, pl.ds(rows, (tlp - tl) * bt), :] = zpad
            kd_s[hd, pl.ds(rows, (tlp - tl) * bt), :] = zpad
            v_s[hd, pl.ds(rows, (tlp - tl) * bt), :] = zpad
    for lv in range(n_levels):
        h = 1 << lv
        nblk = tl // (2 * h)
        b4 = b3.reshape(nblk, 2 * h, bt, D_B)
        bmid = b4[:, h - 1:h]
        lo = k3.reshape(nblk, 2 * h, bt, D_B)[:, :h] * jnp.exp(bmid - b4[:, :h])
        up = q3.reshape(nblk, 2 * h, bt, D_B)[:, h:] * jnp.exp(b4[:, h:] - bmid)
        xl = jnp.concatenate([lo, up], axis=1).reshape(rows, D_B)
        for hd in range(nh):
            xl_s[lv, hd, pl.ds(0, rows), :] = xl[:, hd * DK_B:(hd + 1) * DK_B]
            if tlp > tl:
                xl_s[lv, hd, pl.ds(rows, (tlp - tl) * bt), :] = zpad

    ti = lax.broadcasted_iota(jnp.int32, (tlp, tlp), 0)
    si = lax.broadcasted_iota(jnp.int32, (tlp, tlp), 1)
    masks = []
    for lv in range(n_levels):
        h = 1 << lv
        masks.append(((ti // (2 * h)) == (si // (2 * h))) & ((ti // h) % 2 == 1) & ((si // h) % 2 == 0))

    def bh_step(b, c):
        sl = pl.ds(b, tlp, stride=bt)
        for hd in range(nh):
            amat = jnp.zeros((tlp, tlp), F32)
            for lv in range(n_levels):
                xv = xl_s[lv, hd, sl, :].astype(BF16)
                amat = amat + jnp.where(masks[lv], _dot_nt(xv, xv), 0.0)
            v = v_s[hd, sl, :].astype(BF16)
            st = st_s[b, hd]
            o = _dot(amat.astype(BF16), v) + _dot_nt(qe_s[hd, sl, :].astype(BF16), st.astype(BF16))
            o_s[hd, sl, :] = o
            st_s[b, hd] = st * bl_s[hd, pl.ds(b, 1), :] + _dot_tn(v, kd_s[hd, sl, :].astype(BF16))
        return c
    lax.fori_loop(0, bt, bh_step, 0)

    for hd in range(nh):
        cs = slice(hd * DK_B, (hd + 1) * DK_B)
        vh = v_s[hd, pl.ds(0, rows), :]
        oh = o_s[hd, pl.ds(0, rows), :] + jnp.sum(q_s[:, cs] * kk_s[:, cs], axis=-1, keepdims=True) * vh
        yb = _rms(oh, onw_ref[:, cs]) * _silu(gb_s[:, cs])
        m_s[:, D_A + hd * DV_B:D_A + (hd + 1) * DV_B] = yb.astype(BF16)

    m = _dot(m_s[...], wout_ref[...])
    gate = mod_ref[2]
    y = x_ref[...] + gate[None] * _rms(m, postw_ref[...]).reshape(tl, bt, D_MODEL)
    y_ref[...] = y

    @pl.when(j == nj - 1)
    def _():
        hn_ref[...] = h_last
        convn_ref[...] = conv_tail.reshape(CONV_W - 1, bt, D_A)

        def store_state(b, c):
            for hd in range(nh):
                sn_ref[b, hd] = st_s[b, hd].T
            return c
        lax.fori_loop(0, bt, store_state, 0)


def _layer(x_tm, mod, h0, conv0_tm, s_all, s_prev, w, *, layer, bt, tl):
    L, B, _ = x_tm.shape
    assert L % tl == 0 and B % bt == 0 and bt % SUBLANES == 0 and tl >= CONV_W - 1
    n_levels = tl.bit_length() - 1
    assert (1 << n_levels) == tl
    tlp = max(tl, SUBLANES)
    rows = tl * bt
    prows = tlp * bt
    grid = (B // bt, L // tl)
    const2 = lambda i, j: (0, 0)
    const3 = lambda i, j: (0, 0, 0)
    in_specs = [
        pl.BlockSpec((tl, bt, D_MODEL), lambda i, j: (j, i, 0)),
        pl.BlockSpec((3, bt, D_MODEL), lambda i, j: (0, i, 0)),
        pl.BlockSpec((bt, D_A), lambda i, j: (i, 0)),
        pl.BlockSpec((CONV_W - 1, bt, D_A), lambda i, j: (0, i, 0)),
        pl.BlockSpec((None, bt, N_HEADS_B, DK_B, DV_B), lambda i, j: (layer, i, 0, 0, 0)),
        pl.BlockSpec((1, D_MODEL), const2),
        pl.BlockSpec((1, D_MODEL), const2),
        pl.BlockSpec((D_MODEL, P_IN), const2),
        pl.BlockSpec((CONV_W, D_A), const2),
        pl.BlockSpec((1, D_A), const2),
        pl.BlockSpec((D_A // LANES, LANES, 2 * LANES), const3),
        pl.BlockSpec((1, D_A), const2),
        pl.BlockSpec((1, D_A), const2),
        pl.BlockSpec((1, D_A), const2),
        pl.BlockSpec((DEPTH, D_B), const2),
        pl.BlockSpec((1, D_B), const2),
        pl.BlockSpec((D_MODEL, D_MODEL), const2),
    ]
    out_specs = [
        pl.BlockSpec((tl, bt, D_MODEL), lambda i, j: (j, i, 0)),
        pl.BlockSpec((bt, D_A), lambda i, j: (i, 0)),
        pl.BlockSpec((CONV_W - 1, bt, D_A), lambda i, j: (0, i, 0)),
        pl.BlockSpec((None, bt, N_HEADS_B, DK_B, DV_B), lambda i, j: (layer, i, 0, 0, 0)),
    ]
    out_shape = [
        jax.ShapeDtypeStruct((L, B, D_MODEL), F32),
        jax.ShapeDtypeStruct((B, D_A), F32),
        jax.ShapeDtypeStruct((CONV_W - 1, B, D_A), F32),
        jax.ShapeDtypeStruct((DEPTH, B, N_HEADS_B, DK_B, DV_B), F32),
    ]
    scratch = [
        pltpu.VMEM((rows, D_MODEL), BF16),
        pltpu.VMEM((rows + (CONV_W - 1) * bt, D_A), F32),
        pltpu.VMEM((rows, D_A), F32),
        pltpu.VMEM((rows, D_A), F32),
        pltpu.VMEM((rows, D_A), F32),
        pltpu.VMEM((rows, D_B), F32),
        pltpu.VMEM((rows, D_B), F32),
        pltpu.VMEM((rows, D_B), F32),
        pltpu.VMEM((rows, D_B), F32),
        pltpu.VMEM((N_HEADS_B, bt, DK_B), F32),
        pltpu.VMEM((bt, D_A), F32),
        pltpu.VMEM((bt, N_HEADS_B, DV_B, DK_B), F32),
        pltpu.VMEM((n_levels, N_HEADS_B, prows, LANES), F32),
        pltpu.VMEM((N_HEADS_B, prows, LANES), F32),
        pltpu.VMEM((N_HEADS_B, prows, LANES), F32),
        pltpu.VMEM((N_HEADS_B, prows, LANES), F32),
        pltpu.VMEM((N_HEADS_B, prows, LANES), F32),
        pltpu.VMEM((rows, D_MODEL), BF16),
    ]
    body = functools.partial(_layer_kernel, layer=layer, bt=bt, tl=tl, tlp=tlp, n_levels=n_levels)
    args = [x_tm, mod, h0, conv0_tm, s_all, *w]
    aliases = {}
    if s_prev is not None:
        in_specs = in_specs + [pl.BlockSpec(memory_space=pl.ANY)]
        aliases = {len(args): 3}
        args.append(s_prev)
        inner = body
        body = lambda *refs: inner(*refs[:len(args) - 1], *refs[len(args):])
    return pl.pallas_call(
        body,
        grid=grid,
        in_specs=in_specs,
        out_specs=out_specs,
        out_shape=out_shape,
        scratch_shapes=scratch,
        input_output_aliases=aliases,
        compiler_params=pltpu.CompilerParams(
            dimension_semantics=("parallel", "arbitrary"), vmem_limit_bytes=VMEM_LIMIT),
        name=f"layer{layer}_bt{bt}_tl{tl}",
    )(*args)


CT = SUBLANES


def _sig(x):
    return 0.5 * jnp.tanh(0.5 * x) + 0.5


def _prompt_kernel(x_ref, mod_ref, prew_ref, postw_ref, win_ref, convw_ref, convb_ref, wg_ref, ba_ref, bx_ref,
                   lam_ref, lbl_ref, onw_ref, wout_ref,
                   y_ref, hn_ref, convn_ref, sn_ref,
                   xt_s, hin_s, xpad_s, ccar_s, ga_s, a_s, u_s, xc_s, q_s, kk_s, b_s, gb_s, bl_s, hcar_s, st_s,
                   xl_s, qe_s, kd_s, v_s,
                   *, bt, tl, n_levels):
    o_s = b_s
    m_s = hin_s
    j = pl.program_id(0)
    nj = pl.num_programs(0)
    rows = tl * bt
    ch = CT * bt
    n_ch = tl // CT
    nh = N_HEADS_B
    nt = D_MODEL // LANES
    tail = (CONV_W - 1) * bt

    @pl.when(j == 0)
    def _():
        hcar_s[...] = jnp.zeros_like(hcar_s)
        ccar_s[...] = jnp.zeros_like(ccar_s)
        st_s[...] = jnp.zeros_like(st_s)

    for c in range(n_ch):
        for jl in range(nt):
            for b in range(bt):
                xt_s[jl, pl.ds(c * ch + b, CT, stride=bt), :] = x_ref[b, c * CT:(c + 1) * CT, jl * LANES:(jl + 1) * LANES]

    ti = lax.broadcasted_iota(jnp.int32, (tl, tl), 0)
    si = lax.broadcasted_iota(jnp.int32, (tl, tl), 1)
    masks = []
    for lv in range(n_levels):
        h = 1 << lv
        masks.append(((ti // (2 * h)) == (si // (2 * h))) & ((ti // h) % 2 == 1) & ((si // h) % 2 == 0))

    def bcast(v):
        return jnp.concatenate([v] * CT, axis=0)

    for l in range(DEPTH):
        shift, scale, gate = mod_ref[l, 0], mod_ref[l, 1], mod_ref[l, 2]
        pm = prew_ref[l] * (1.0 + scale)
        gp = postw_ref[l] * gate

        for c in range(n_ch):
            rs = pl.ds(c * ch, ch)
            ss = jnp.sum(sum(jnp.square(xt_s[jl, rs, :]) for jl in range(nt)), axis=-1, keepdims=True)
            inv = lax.rsqrt(ss * (1.0 / D_MODEL) + EPS)
            for jl in range(nt):
                ls = slice(jl * LANES, (jl + 1) * LANES)
                hin_s[rs, ls] = (xt_s[jl, rs, :] * inv * bcast(pm[:, ls]) + bcast(shift[:, ls])).astype(BF16)

        def proj(c0, width):
            return _dot(hin_s[...], win_ref[l, :, c0:c0 + width])
        q_s[...] = proj(2 * D_A, D_B)
        kk_s[...] = proj(2 * D_A + D_B, D_B)
        iv = proj(2 * D_A + 2 * D_B, D_B)
        for hd in range(nh):
            v_s[hd] = iv[:, hd * DV_B:(hd + 1) * DV_B]
        gb_s[...] = proj(2 * D_A + 3 * D_B, D_B)

        logits = lbl_ref[...]
        pe = jnp.exp(logits - jnp.max(logits, axis=0, keepdims=True))
        probs = pe / jnp.sum(pe, axis=0, keepdims=True)
        lb = jnp.sum(probs[1:l + 1], axis=0, keepdims=True) if l > 0 else jnp.zeros((1, D_B), F32)
        c0 = 0.5 * (1.0 + lb)
        c1 = 0.5 * (1.0 - lb)
        bl = jnp.zeros((bt, D_B), F32)
        for t in range(tl):
            sl = pl.ds(t * bt, bt)
            p = c1 * jnp.tanh(0.5 * kk_s[sl, :])
            bl = bl + jnp.log(jnp.maximum(c0 + p, F_MIN))
            for hd in range(nh):
                b_s[hd, sl, :] = bl[:, hd * DK_B:(hd + 1) * DK_B]
            kk_s[sl, :] = c1 - p
        ebl = jnp.exp(bl)
        for hd in range(nh):
            bl_s[hd] = ebl[:, hd * DK_B:(hd + 1) * DK_B]

        for c in range(n_ch):
            rs = pl.ds(c * ch, ch)
            for hd in range(nh):
                ls = slice(hd * DK_B, (hd + 1) * DK_B)
                bc, qc, kc = b_s[hd, rs, :], q_s[rs, ls], kk_s[rs, ls]
                qe_s[hd, rs, :] = qc * jnp.exp(bc)
                kd_s[hd, rs, :] = kc * jnp.exp(bcast(bl[:, ls]) - bc)
                for lv in range(n_levels):
                    h = 1 << lv
                    if 2 * h <= CT:
                        parts = []
                        for t in range(CT):
                            mid = (t // (2 * h)) * 2 * h + h - 1
                            ts, ms = slice(t * bt, (t + 1) * bt), slice(mid * bt, (mid + 1) * bt)
                            if t % (2 * h) >= h:
                                parts.append(qc[ts] * jnp.exp(bc[ts] - bc[ms]))
                            else:
                                parts.append(kc[ts] * jnp.exp(bc[ms] - bc[ts]))
                        xl = jnp.concatenate(parts, axis=0)
                    else:
                        t0 = c * CT
                        mid = (t0 // (2 * h)) * 2 * h + h - 1
                        bmid = bcast(b_s[hd, pl.ds(mid * bt, bt), :])
                        xl = qc * jnp.exp(bc - bmid) if t0 % (2 * h) >= h else kc * jnp.exp(bmid - bc)
                    xl_s[lv, hd, rs, :] = xl

        xpad_s[pl.ds(0, tail), :] = ccar_s[l]
        xpad_s[pl.ds(tail, rows), :] = proj(0, D_A)
        ga_s[...] = proj(D_A, D_A)
        cw = convw_ref[l]
        for c in range(n_ch):
            xc = convb_ref[l] + sum(xpad_s[pl.ds(c * ch + k * bt, ch), :] * cw[k:k + 1, :] for k in range(CONV_W))
            xc_s[pl.ds(c * ch, ch), :] = xc
        ctail = xpad_s[pl.ds(rows, tail), :]
        ccar_s[l] = ctail
        for c in range(D_A // LANES):
            g2 = _dot(xc_s[:, c * LANES:(c + 1) * LANES].astype(BF16), wg_ref[l, c])
            a_s[:, c * LANES:(c + 1) * LANES] = g2[:, :LANES]
            u_s[:, c * LANES:(c + 1) * LANES] = g2[:, LANES:]
        coef = -C_RGLRU * jax.nn.softplus(-lam_ref[l])
        ba, bx = ba_ref[l], bx_ref[l]

        def scan_chunk(c, h):
            for tp in range(CT // 2):
                yas = []
                for t2 in range(2):
                    sl = pl.ds(c * ch + (2 * tp + t2) * bt, bt)
                    r = _sig(a_s[sl, :] + ba)
                    i = _sig(u_s[sl, :] + bx)
                    log_a = coef * r
                    a = jnp.exp(log_a)
                    u = jnp.sqrt(jnp.maximum(jnp.tanh(-log_a) * (1.0 + a * a), 0.0)) * (i * xc_s[sl, :])
                    h = a * h + u
                    ga = ga_s[sl, :]
                    yas.append(h * (ga * _sig(ga)))
                m_s[pl.ds(c * ch + 2 * tp * bt, 2 * bt), :D_A] = jnp.concatenate(yas, axis=0).astype(BF16)
            return h

        def chains(b):
            sl = pl.ds(b, tl, stride=bt)
            res = []
            for hd in range(nh):
                amat = jnp.zeros((tl, tl), F32)
                for lv in range(n_levels):
                    xv = xl_s[lv, hd, sl, :].astype(BF16)
                    amat = amat + jnp.where(masks[lv], _dot_nt(xv, xv), 0.0)
                v = v_s[hd, sl, :].astype(BF16)
                st = st_s[l, b, hd]
                o = _dot(amat.astype(BF16), v) + _dot_nt(qe_s[hd, sl, :].astype(BF16), st.astype(BF16))
                st_new = st * bl_s[hd, b:b + 1, :] + _dot_tn(v, kd_s[hd, sl, :].astype(BF16))
                res.append((o, st_new))
            for hd in range(nh):
                o_s[hd, sl, :] = res[hd][0]
                st_s[l, b, hd] = res[hd][1]

        h_last = hcar_s[l]
        for idx in range(max(bt, n_ch)):
            if idx < bt:
                chains(idx)
            if idx < n_ch:
                h_last = scan_chunk(idx, h_last)
        hcar_s[l] = h_last

        for c in range(n_ch):
            rs = pl.ds(c * ch, ch)
            for hd in range(nh):
                ls = slice(hd * DK_B, (hd + 1) * DK_B)
                dsum = jnp.sum(q_s[rs, ls] * kk_s[rs, ls], axis=-1, keepdims=True)
                oh = o_s[hd, rs, :] + dsum * v_s[hd, rs, :]
                gb = gb_s[rs, ls]
                yb = _rms(oh, onw_ref[l][:, ls]) * (gb * _sig(gb))
                m_s[rs, D_A + hd * DV_B:D_A + (hd + 1) * DV_B] = yb.astype(BF16)

        mo = _dot(m_s[...], wout_ref[l])
        for jl in range(nt):
            xl_s[jl // nh, jl % nh] = mo[:, jl * LANES:(jl + 1) * LANES]
        for c in range(n_ch):
            rs = pl.ds(c * ch, ch)
            ss = jnp.sum(sum(jnp.square(xl_s[jl // nh, jl % nh, rs, :]) for jl in range(nt)), axis=-1, keepdims=True)
            inv = lax.rsqrt(ss * (1.0 / D_MODEL) + EPS)
            for jl in range(nt):
                ls = slice(jl * LANES, (jl + 1) * LANES)
                xt_s[jl, rs, :] = xt_s[jl, rs, :] + xl_s[jl // nh, jl % nh, rs, :] * inv * bcast(gp[:, ls])

        @pl.when(j == nj - 1)
        def _():
            hn_ref[l] = h_last
            convn_ref[l] = ctail.reshape(CONV_W - 1, bt, D_A)
            for b in range(bt):
                for hd in range(nh):
                    sn_ref[l, b, hd] = st_s[l, b, hd].T

    for c in range(n_ch):
        for jl in range(nt):
            for b in range(bt):
                y_ref[b, c * CT:(c + 1) * CT, jl * LANES:(jl + 1) * LANES] = xt_s[jl, pl.ds(c * ch + b, CT, stride=bt), :]


def _prompt(x, mod, w, *, tl):
    B, L, _ = x.shape
    bt = B
    assert bt == SUBLANES and L % tl == 0 and tl % CT == 0
    n_levels = tl.bit_length() - 1
    assert (1 << n_levels) == tl and n_levels * N_HEADS_B >= D_MODEL // LANES
    rows = tl * bt
    tail = (CONV_W - 1) * bt

    def const(shape):
        nd = len(shape)
        return pl.BlockSpec(shape, lambda j: (0,) * nd, pipeline_mode=pl.Buffered(1))
    in_specs = [pl.BlockSpec((bt, tl, D_MODEL), lambda j: (0, j, 0)), const(mod.shape)] + [const(a.shape) for a in w]
    out_specs = [
        pl.BlockSpec((bt, tl, D_MODEL), lambda j: (0, j, 0)),
        pl.BlockSpec((DEPTH, bt, D_A), lambda j: (0, 0, 0)),
        pl.BlockSpec((DEPTH, CONV_W - 1, bt, D_A), lambda j: (0, 0, 0, 0)),
        pl.BlockSpec((DEPTH, bt, N_HEADS_B, DK_B, DV_B), lambda j: (0, 0, 0, 0, 0)),
    ]
    out_shape = [
        jax.ShapeDtypeStruct((B, L, D_MODEL), F32),
        jax.ShapeDtypeStruct((DEPTH, B, D_A), F32),
        jax.ShapeDtypeStruct((DEPTH, CONV_W - 1, B, D_A), F32),
        jax.ShapeDtypeStruct((DEPTH, B, N_HEADS_B, DK_B, DV_B), F32),
    ]
    slab = lambda n: pltpu.VMEM((n, rows, LANES), F32)
    wide = lambda: pltpu.VMEM((rows, D_A), F32)
    scratch = [
        slab(D_MODEL // LANES),
        pltpu.VMEM((rows, D_MODEL), BF16),
        pltpu.VMEM((rows + tail, D_A), F32),
        pltpu.VMEM((DEPTH, tail, D_A), F32),
        wide(), wide(), wide(), wide(),
        wide(), wide(), slab(N_HEADS_B), wide(),
        pltpu.VMEM((N_HEADS_B, bt, DK_B), F32),
        pltpu.VMEM((DEPTH, bt, D_A), F32),
        pltpu.VMEM((DEPTH, bt, N_HEADS_B, DV_B, DK_B), F32),
        pltpu.VMEM((n_levels, N_HEADS_B, rows, LANES), F32),
        slab(N_HEADS_B), slab(N_HEADS_B), slab(N_HEADS_B),
    ]
    body = functools.partial(_prompt_kernel, bt=bt, tl=tl, n_levels=n_levels)
    return pl.pallas_call(
        body,
        grid=(L // tl,),
        in_specs=in_specs,
        out_specs=out_specs,
        out_shape=out_shape,
        scratch_shapes=scratch,
        compiler_params=pltpu.CompilerParams(dimension_semantics=("arbitrary",), vmem_limit_bytes=VMEM_LIMIT),
        name="prompt_layers",
    )(x, mod, *w)


def _gate_weights(wa, wx):
    per = LANES // BLK_A

    def bd(w):
        w = w.reshape(D_A // LANES, per, BLK_A, BLK_A)
        eye = jnp.eye(per, dtype=w.dtype)
        return jnp.einsum('jpcd,pq->jpcqd', w, eye).reshape(D_A // LANES, LANES, LANES)
    return jnp.concatenate([bd(wa), bd(wx)], axis=-1).astype(BF16)


def kernel(x_prompt, x_sample, state_rglru_h, state_rglru_conv, state_hgrn_S, c_prompt, c_sample, ada_w, ada_b,
           pre_norm_w, post_norm_w, w_in, conv_w, conv_b, rg_wa, rg_ba, rg_wx, rg_bx, rg_lambda, hg_lb_logits,
           hg_onorm_w, w_out):
    bp, bs = x_prompt.shape[0], x_sample.shape[0]
    mod = _modulation(jnp.concatenate([c_prompt, c_sample], axis=0), ada_w.astype(BF16),
                      ada_b.reshape(DEPTH, 1, 3 * D_MODEL))
    w_in_bf, w_out_bf = w_in.astype(BF16), w_out.astype(BF16)
    wg = jnp.stack([_gate_weights(rg_wa[l], rg_wx[l]) for l in range(DEPTH)])
    row = lambda a: a[:, None, :]
    wp = (row(pre_norm_w), row(post_norm_w), w_in_bf, conv_w, row(conv_b), wg, row(rg_ba), row(rg_bx),
          row(rg_lambda), hg_lb_logits, row(hg_onorm_w), w_out_bf)
    yp, hp, cvp, sp = _prompt(x_prompt, mod[:, :, :bp], wp, tl=64)

    xs = jnp.transpose(x_sample, (1, 0, 2))
    outs_s = []
    ss = None
    for l in range(DEPTH):
        w = (pre_norm_w[l][None], post_norm_w[l][None], w_in_bf[l], conv_w[l], conv_b[l][None],
             wg[l], rg_ba[l][None], rg_bx[l][None], rg_lambda[l][None],
             hg_lb_logits, hg_onorm_w[l][None], w_out_bf[l])
        xs, h, cv, ss = _layer(xs, mod[l, :, bp:], state_rglru_h[l], jnp.transpose(state_rglru_conv[l], (1, 0, 2)),
                               state_hgrn_S, ss, w, layer=l, bt=16, tl=x_sample.shape[1])
        outs_s.append((h, jnp.transpose(cv, (1, 0, 2))))
    stack = lambda outs, k: jnp.stack([o[k] for o in outs])
    return (yp, jnp.transpose(xs, (1, 0, 2)), hp, jnp.transpose(cvp, (0, 2, 1, 3)), sp,
            stack(outs_s, 0), stack(outs_s, 1), ss)
```

```python
import functools

import jax
import jax.numpy as jnp
from jax import lax
from jax.experimental import pallas as pl
from jax.experimental.pallas import tpu as pltpu

D_MODEL = 1024
DEPTH = 2
D_A = 512
N_BLK_A = 8
BLK_A = 64
CONV_W = 4
C_RGLRU = 8.0
D_B = 512
N_HEADS_B = 4
DK_B = 128
DV_B = 128
P_IN = 2 * D_A + 4 * D_B
EPS = 1e-6
F_MIN = 1e-30

LANES = 128
SUBLANES = 8
VMEM_LIMIT = 60 * 1024 * 1024

BF16 = jnp.bfloat16
F32 = jnp.float32


def _dot(a, b):
    return jnp.dot(a, b, preferred_element_type=F32)


def _dot_nt(a, b):
    return lax.dot_general(a, b, (((1,), (1,)), ((), ())), preferred_element_type=F32)


def _dot_tn(a, b):
    return lax.dot_general(a, b, (((0,), (0,)), ((), ())), preferred_element_type=F32)


def _rms(x, w):
    return x * lax.rsqrt(jnp.mean(x * x, axis=-1, keepdims=True) + EPS) * w


def _silu(x):
    return x * jax.nn.sigmoid(x)


def _mod_kernel(c_ref, w_ref, b_ref, o_ref):
    o_ref[...] = _dot(_silu(c_ref[...]).astype(BF16), w_ref[...]) + b_ref[...]


def _modulation(c_all, ada_w_bf, ada_b):
    nb = c_all.shape[0]
    return pl.pallas_call(
        _mod_kernel,
        grid=(DEPTH, 3),
        in_specs=[
            pl.BlockSpec((nb, D_MODEL), lambda l, k: (0, 0)),
            pl.BlockSpec((None, D_MODEL, D_MODEL), lambda l, k: (l, 0, k)),
            pl.BlockSpec((None, 1, D_MODEL), lambda l, k: (l, 0, k)),
        ],
        out_specs=pl.BlockSpec((None, None, nb, D_MODEL), lambda l, k: (l, k, 0, 0)),
        out_shape=jax.ShapeDtypeStruct((DEPTH, 3, nb, D_MODEL), F32),
        name="adaln_mod",
    )(c_all, ada_w_bf, ada_b)


TP = SUBLANES


def _sig(x):
    return 0.5 * jnp.tanh(0.5 * x) + 0.5


def _decode_kernel(x_ref, mod_ref, h0_ref, conv0_ref, s0_ref, prew_ref, postw_ref, win_ref, convw_ref,
                   convb_ref, wg_ref, ba_ref, bx_ref, lam_ref, lbl_ref, onw_ref, wout_ref,
                   y_ref, hn_ref, convn_ref, sn_ref,
                   hin_s, xpad_s, ga_s, a_s, u_s, q_s, kk_s, b_s, gb_s, oi_s, qe8_s, kdx8_s, v8_s, oc8_s,
                   *, layer, nb, bt, tl):
    m_s = hin_s
    i = pl.program_id(0)
    ni = pl.num_programs(0)
    rows = tl * nb
    tail = (CONV_W - 1) * nb
    nh = N_HEADS_B
    heads = [slice(hd * DK_B, (hd + 1) * DK_B) for hd in range(nh)]

    @pl.when(i == 0)
    def _():
        shift, scale = mod_ref[0], mod_ref[1]
        pm = prew_ref[...] * (1.0 + scale)
        for t in range(tl):
            xt = x_ref[t]
            inv = lax.rsqrt(jnp.mean(xt * xt, axis=-1, keepdims=True) + EPS)
            hin_s[pl.ds(t * nb, nb), :] = (xt * inv * pm + shift).astype(BF16)

        def proj(c0, width):
            return _dot(hin_s[...], win_ref[:, c0:c0 + width])
        xpad_s[pl.ds(0, tail), :] = conv0_ref[...].reshape(tail, D_A)
        xpad_s[pl.ds(tail, rows), :] = proj(0, D_A)
        ga_s[...] = proj(D_A, D_A)
        q_s[...] = proj(2 * D_A, D_B)
        kk_s[...] = proj(2 * D_A + D_B, D_B)
        iv = proj(2 * D_A + 2 * D_B, D_B)
        gb_s[...] = proj(2 * D_A + 3 * D_B, D_B)

        xc = convb_ref[...] + sum(xpad_s[pl.ds(k * nb, rows), :] * convw_ref[k:k + 1, :] for k in range(CONV_W))
        convn_ref[...] = xpad_s[pl.ds(rows, tail), :].reshape(CONV_W - 1, nb, D_A)
        xcb = xc.astype(BF16)
        for c in range(D_A // LANES):
            g2 = _dot(xcb[:, c * LANES:(c + 1) * LANES], wg_ref[c])
            a_s[:, c * LANES:(c + 1) * LANES] = g2[:, :LANES]
            u_s[:, c * LANES:(c + 1) * LANES] = g2[:, LANES:]
        coef = -C_RGLRU * jax.nn.softplus(-lam_ref[...])
        h = h0_ref[...]
        for t in range(tl):
            sl = pl.ds(t * nb, nb)
            r = _sig(a_s[sl, :] + ba_ref[...])
            ig = _sig(u_s[sl, :] + bx_ref[...])
            log_a = coef * r
            a = jnp.exp(log_a)
            u = jnp.sqrt(jnp.maximum(jnp.tanh(-log_a) * (1.0 + a * a), 0.0)) * (ig * xc[t * nb:(t + 1) * nb])
            h = a * h + u
            ga = ga_s[sl, :]
            m_s[sl, :D_A] = (h * (ga * _sig(ga))).astype(BF16)
        hn_ref[...] = h

        logits = lbl_ref[...]
        pe = jnp.exp(logits - jnp.max(logits, axis=0, keepdims=True))
        probs = pe / jnp.sum(pe, axis=0, keepdims=True)
        lb = jnp.sum(probs[1:layer + 1], axis=0, keepdims=True) if layer > 0 else jnp.zeros((1, D_B), F32)
        c0 = 0.5 * (1.0 + lb)
        c1 = 0.5 * (1.0 - lb)
        bl = jnp.zeros((nb, D_B), F32)
        for t in range(tl):
            sl = pl.ds(t * nb, nb)
            p = c1 * jnp.tanh(0.5 * kk_s[sl, :])
            bl = bl + jnp.log(jnp.maximum(c0 + p, F_MIN))
            b_s[sl, :] = bl
            kk_s[sl, :] = c1 - p

        for t in range(tl):
            st_ = pl.ds(t * nb, nb)
            acc = [jnp.zeros((nb, DV_B), F32) for _ in range(nh)]
            for s in range(t + 1):
                ss_ = pl.ds(s * nb, nb)
                w = q_s[st_, :] * kk_s[ss_, :]
                if s < t:
                    w = w * jnp.exp(b_s[st_, :] - b_s[ss_, :])
                for hd in range(nh):
                    acc[hd] = acc[hd] + jnp.sum(w[:, heads[hd]], axis=-1, keepdims=True) * iv[s * nb:(s + 1) * nb, heads[hd]]
            for hd in range(nh):
                oi_s[st_, heads[hd]] = acc[hd]

        zrow = jnp.zeros((nb, LANES), F32)
        ebl = jnp.exp(bl)
        e_hi = ebl.astype(BF16).astype(F32)
        e_mid = (ebl - e_hi).astype(BF16).astype(F32)
        e_lo = ebl - e_hi - e_mid
        for hd in range(nh):
            for t in range(TP):
                dst = pl.ds(t, nb, stride=TP)
                if t < tl:
                    st_ = pl.ds(t * nb, nb)
                    bt_ = b_s[st_, heads[hd]]
                    qe8_s[hd, dst, :] = q_s[st_, heads[hd]] * jnp.exp(bt_)
                    kdx8_s[hd, dst, :] = kk_s[st_, heads[hd]] * jnp.exp(bl[:, heads[hd]] - bt_)
                    v8_s[hd, dst, :] = iv[t * nb:(t + 1) * nb, heads[hd]]
                else:
                    qe8_s[hd, dst, :] = zrow
                    v8_s[hd, dst, :] = zrow
                    dec = (e_hi, e_mid, e_lo)
                    kdx8_s[hd, dst, :] = dec[t - tl][:, heads[hd]] if t - tl < len(dec) else zrow

    rid = lax.broadcasted_iota(jnp.int32, (TP, LANES), 0)
    ones_rows = jnp.where((rid >= tl) & (rid < tl + 3), 1.0, 0.0)
    for bl_ in range(bt):
        r8 = pl.ds(pl.multiple_of((i * bt + bl_) * TP, TP), TP)
        for hd in range(nh):
            s_old = s0_ref[bl_, hd]
            oc8_s[hd, r8, :] = _dot(qe8_s[hd, r8, :].astype(BF16), s_old.astype(BF16))
            rhs = jnp.concatenate([v8_s[hd, r8, :], ones_rows], axis=1).astype(BF16)
            g = _dot_tn(kdx8_s[hd, r8, :].astype(BF16), rhs)
            sn_ref[bl_, hd] = s_old * g[:, DV_B:] + g[:, :DV_B]

    @pl.when(i == ni - 1)
    def _():
        for t in range(tl):
            st_ = pl.ds(t * nb, nb)
            for hd in range(nh):
                oh = oi_s[st_, heads[hd]] + oc8_s[hd, pl.ds(t, nb, stride=TP), :]
                gb = gb_s[st_, heads[hd]]
                yb = _rms(oh, onw_ref[:, heads[hd]]) * (gb * _sig(gb))
                m_s[st_, D_A + hd * DV_B:D_A + (hd + 1) * DV_B] = yb.astype(BF16)
        mo = _dot(m_s[...], wout_ref[...])
        gp = postw_ref[...] * mod_ref[2]
        for t in range(tl):
            mt = mo[t * nb:(t + 1) * nb]
            inv = lax.rsqrt(jnp.mean(mt * mt, axis=-1, keepdims=True) + EPS)
            y_ref[t] = x_ref[t] + mt * inv * gp


def _decode_layer(x_tm, mod, h0, conv0_tm, s_all, s_prev, w, *, layer, bt):
    tl, nb, _ = x_tm.shape
    assert nb % bt == 0 and nb % SUBLANES == 0 and CONV_W - 1 <= tl and tl + 3 <= TP
    rows = tl * nb

    def const(shape):
        nd = len(shape)
        return pl.BlockSpec(shape, lambda i: (0,) * nd, pipeline_mode=pl.Buffered(1))
    state_spec = pl.BlockSpec((None, bt, N_HEADS_B, DK_B, DV_B), lambda i: (layer, i, 0, 0, 0))
    args = [x_tm, mod, h0, conv0_tm, s_all, *w]
    in_specs = [const(a.shape) for a in args]
    in_specs[4] = state_spec
    out_specs = [
        pl.BlockSpec((tl, nb, D_MODEL), lambda i: (0, 0, 0)),
        pl.BlockSpec((nb, D_A), lambda i: (0, 0)),
        pl.BlockSpec((CONV_W - 1, nb, D_A), lambda i: (0, 0, 0)),
        state_spec,
    ]
    out_shape = [
        jax.ShapeDtypeStruct((tl, nb, D_MODEL), F32),
        jax.ShapeDtypeStruct((nb, D_A), F32),
        jax.ShapeDtypeStruct((CONV_W - 1, nb, D_A), F32),
        jax.ShapeDtypeStruct(s_all.shape, F32),
    ]
    wide = lambda: pltpu.VMEM((rows, D_A), F32)
    slab8 = lambda: pltpu.VMEM((N_HEADS_B, nb * TP, LANES), F32)
    scratch = [
        pltpu.VMEM((rows, D_MODEL), BF16),
        pltpu.VMEM((rows + (CONV_W - 1) * nb, D_A), F32),
        wide(), wide(), wide(),
        wide(), wide(), wide(), wide(),
        wide(),
        slab8(), slab8(), slab8(), slab8(),
    ]
    body = functools.partial(_decode_kernel, layer=layer, nb=nb, bt=bt, tl=tl)
    aliases = {}
    if s_prev is not None:
        in_specs = in_specs + [pl.BlockSpec(memory_space=pl.ANY)]
        aliases = {len(args): 3}
        args.append(s_prev)
        inner = body
        body = lambda *refs: inner(*refs[:len(args) - 1], *refs[len(args):])
    return pl.pallas_call(
        body,
        grid=(nb // bt,),
        in_specs=in_specs,
        out_specs=out_specs,
        out_shape=out_shape,
        scratch_shapes=scratch,
        input_output_aliases=aliases,
        compiler_params=pltpu.CompilerParams(dimension_semantics=("arbitrary",), vmem_limit_bytes=VMEM_LIMIT),
        name=f"decode_layer{layer}",
    )(*args)


CT = SUBLANES


def _prompt_kernel(x_ref, mod_ref, prew_ref, postw_ref, win_ref, convw_ref, convb_ref, wg_ref, ba_ref, bx_ref,
                   lam_ref, lbl_ref, onw_ref, wout_ref,
                   y_ref, hn_ref, convn_ref, sn_ref,
                   xt_s, hin_s, xpad_s, ccar_s, ga_s, a_s, u_s, xc_s, q_s, kk_s, b_s, gb_s, bl_s, hcar_s, st_s,
                   xl_s, qe_s, kd_s, v_s,
                   *, bt, tl, n_levels):
    o_s = b_s
    m_s = hin_s
    j = pl.program_id(0)
    nj = pl.num_programs(0)
    rows = tl * bt
    ch = CT * bt
    n_ch = tl // CT
    nh = N_HEADS_B
    nt = D_MODEL // LANES
    tail = (CONV_W - 1) * bt

    @pl.when(j == 0)
    def _():
        hcar_s[...] = jnp.zeros_like(hcar_s)
        ccar_s[...] = jnp.zeros_like(ccar_s)
        st_s[...] = jnp.zeros_like(st_s)

    for c in range(n_ch):
        for jl in range(nt):
            for b in range(bt):
                xt_s[jl, pl.ds(c * ch + b, CT, stride=bt), :] = x_ref[b, c * CT:(c + 1) * CT, jl * LANES:(jl + 1) * LANES]

    ti = lax.broadcasted_iota(jnp.int32, (tl, tl), 0)
    si = lax.broadcasted_iota(jnp.int32, (tl, tl), 1)
    masks = []
    for lv in range(n_levels):
        h = 1 << lv
        masks.append(((ti // (2 * h)) == (si // (2 * h))) & ((ti // h) % 2 == 1) & ((si // h) % 2 == 0))

    def bcast(v):
        return jnp.concatenate([v] * CT, axis=0)

    for l in range(DEPTH):
        shift, scale, gate = mod_ref[l, 0], mod_ref[l, 1], mod_ref[l, 2]
        pm = prew_ref[l] * (1.0 + scale)
        gp = postw_ref[l] * gate

        for c in range(n_ch):
            rs = pl.ds(c * ch, ch)
            ss = jnp.sum(sum(jnp.square(xt_s[jl, rs, :]) for jl in range(nt)), axis=-1, keepdims=True)
            inv = lax.rsqrt(ss * (1.0 / D_MODEL) + EPS)
            for jl in range(nt):
                ls = slice(jl * LANES, (jl + 1) * LANES)
                hin_s[rs, ls] = (xt_s[jl, rs, :] * inv * bcast(pm[:, ls]) + bcast(shift[:, ls])).astype(BF16)

        def proj(c0, width):
            return _dot(hin_s[...], win_ref[l, :, c0:c0 + width])
        q_s[...] = proj(2 * D_A, D_B)
        kk_s[...] = proj(2 * D_A + D_B, D_B)
        iv = proj(2 * D_A + 2 * D_B, D_B)
        for hd in range(nh):
            v_s[hd] = iv[:, hd * DV_B:(hd + 1) * DV_B]
        gb_s[...] = proj(2 * D_A + 3 * D_B, D_B)

        logits = lbl_ref[...]
        pe = jnp.exp(logits - jnp.max(logits, axis=0, keepdims=True))
        probs = pe / jnp.sum(pe, axis=0, keepdims=True)
        lb = jnp.sum(probs[1:l + 1], axis=0, keepdims=True) if l > 0 else jnp.zeros((1, D_B), F32)
        c0 = 0.5 * (1.0 + lb)
        c1 = 0.5 * (1.0 - lb)
        bl = jnp.zeros((bt, D_B), F32)
        for t in range(tl):
            sl = pl.ds(t * bt, bt)
            p = c1 * jnp.tanh(0.5 * kk_s[sl, :])
            bl = bl + jnp.log(jnp.maximum(c0 + p, F_MIN))
            for hd in range(nh):
                b_s[hd, sl, :] = bl[:, hd * DK_B:(hd + 1) * DK_B]
            kk_s[sl, :] = c1 - p
        ebl = jnp.exp(bl)
        for hd in range(nh):
            bl_s[hd] = ebl[:, hd * DK_B:(hd + 1) * DK_B]

        for c in range(n_ch):
            rs = pl.ds(c * ch, ch)
            for hd in range(nh):
                ls = slice(hd * DK_B, (hd + 1) * DK_B)
                bc, qc, kc = b_s[hd, rs, :], q_s[rs, ls], kk_s[rs, ls]
                qe_s[hd, rs, :] = qc * jnp.exp(bc)
                kd_s[hd, rs, :] = kc * jnp.exp(bcast(bl[:, ls]) - bc)
                for lv in range(n_levels):
                    h = 1 << lv
                    if 2 * h <= CT:
                        parts = []
                        for t in range(CT):
                            mid = (t // (2 * h)) * 2 * h + h - 1
                            ts, ms = slice(t * bt, (t + 1) * bt), slice(mid * bt, (mid + 1) * bt)
                            if t % (2 * h) >= h:
                                parts.append(qc[ts] * jnp.exp(bc[ts] - bc[ms]))
                            else:
                                parts.append(kc[ts] * jnp.exp(bc[ms] - bc[ts]))
                        xl = jnp.concatenate(parts, axis=0)
                    else:
                        t0 = c * CT
                        mid = (t0 // (2 * h)) * 2 * h + h - 1
                        bmid = bcast(b_s[hd, pl.ds(mid * bt, bt), :])
                        xl = qc * jnp.exp(bc - bmid) if t0 % (2 * h) >= h else kc * jnp.exp(bmid - bc)
                    xl_s[lv, hd, rs, :] = xl

        xpad_s[pl.ds(0, tail), :] = ccar_s[l]
        xpad_s[pl.ds(tail, rows), :] = proj(0, D_A)
        ga_s[...] = proj(D_A, D_A)
        cw = convw_ref[l]
        for c in range(n_ch):
            xc = convb_ref[l] + sum(xpad_s[pl.ds(c * ch + k * bt, ch), :] * cw[k:k + 1, :] for k in range(CONV_W))
            xc_s[pl.ds(c * ch, ch), :] = xc
        ctail = xpad_s[pl.ds(rows, tail), :]
        ccar_s[l] = ctail
        for c in range(D_A // LANES):
            g2 = _dot(xc_s[:, c * LANES:(c + 1) * LANES].astype(BF16), wg_ref[l, c])
            a_s[:, c * LANES:(c + 1) * LANES] = g2[:, :LANES]
            u_s[:, c * LANES:(c + 1) * LANES] = g2[:, LANES:]
        coef = -C_RGLRU * jax.nn.softplus(-lam_ref[l])
        ba, bx = ba_ref[l], bx_ref[l]

        def scan_chunk(c, h):
            for tp in range(CT // 2):
                yas = []
                for t2 in range(2):
                    sl = pl.ds(c * ch + (2 * tp + t2) * bt, bt)
                    r = _sig(a_s[sl, :] + ba)
                    i = _sig(u_s[sl, :] + bx)
                    log_a = coef * r
                    a = jnp.exp(log_a)
                    u = jnp.sqrt(jnp.maximum(jnp.tanh(-log_a) * (1.0 + a * a), 0.0)) * (i * xc_s[sl, :])
                    h = a * h + u
                    ga = ga_s[sl, :]
                    yas.append(h * (ga * _sig(ga)))
                m_s[pl.ds(c * ch + 2 * tp * bt, 2 * bt), :D_A] = jnp.concatenate(yas, axis=0).astype(BF16)
            return h

        def chains(b):
            sl = pl.ds(b, tl, stride=bt)
            res = []
            for hd in range(nh):
                amat = jnp.zeros((tl, tl), F32)
                for lv in range(n_levels):
                    xv = xl_s[lv, hd, sl, :].astype(BF16)
                    amat = amat + jnp.where(masks[lv], _dot_nt(xv, xv), 0.0)
                v = v_s[hd, sl, :].astype(BF16)
                st = st_s[l, b, hd]
                o = _dot(amat.astype(BF16), v) + _dot_nt(qe_s[hd, sl, :].astype(BF16), st.astype(BF16))
                st_new = st * bl_s[hd, b:b + 1, :] + _dot_tn(v, kd_s[hd, sl, :].astype(BF16))
                res.append((o, st_new))
            for hd in range(nh):
                o_s[hd, sl, :] = res[hd][0]
                st_s[l, b, hd] = res[hd][1]

        h_last = hcar_s[l]
        for idx in range(max(bt, n_ch)):
            if idx < bt:
                chains(idx)
            if idx < n_ch:
                h_last = scan_chunk(idx, h_last)
        hcar_s[l] = h_last

        for c in range(n_ch):
            rs = pl.ds(c * ch, ch)
            for hd in range(nh):
                ls = slice(hd * DK_B, (hd + 1) * DK_B)
                dsum = jnp.sum(q_s[rs, ls] * kk_s[rs, ls], axis=-1, keepdims=True)
                oh = o_s[hd, rs, :] + dsum * v_s[hd, rs, :]
                gb = gb_s[rs, ls]
                yb = _rms(oh, onw_ref[l][:, ls]) * (gb * _sig(gb))
                m_s[rs, D_A + hd * DV_B:D_A + (hd + 1) * DV_B] = yb.astype(BF16)

        mo = _dot(m_s[...], wout_ref[l])
        for jl in range(nt):
            xl_s[jl // nh, jl % nh] = mo[:, jl * LANES:(jl + 1) * LANES]
        for c in range(n_ch):
            rs = pl.ds(c * ch, ch)
            ss = jnp.sum(sum(jnp.square(xl_s[jl // nh, jl % nh, rs, :]) for jl in range(nt)), axis=-1, keepdims=True)
            inv = lax.rsqrt(ss * (1.0 / D_MODEL) + EPS)
            for jl in range(nt):
                ls = slice(jl * LANES, (jl + 1) * LANES)
                xt_s[jl, rs, :] = xt_s[jl, rs, :] + xl_s[jl // nh, jl % nh, rs, :] * inv * bcast(gp[:, ls])

        @pl.when(j == nj - 1)
        def _():
            hn_ref[l] = h_last
            convn_ref[l] = ctail.reshape(CONV_W - 1, bt, D_A)
            for b in range(bt):
                for hd in range(nh):
                    sn_ref[l, b, hd] = st_s[l, b, hd].T

    for c in range(n_ch):
        for jl in range(nt):
            for b in range(bt):
                y_ref[b, c * CT:(c + 1) * CT, jl * LANES:(jl + 1) * LANES] = xt_s[jl, pl.ds(c * ch + b, CT, stride=bt), :]


def _prompt(x, mod, w, *, tl):
    B, L, _ = x.shape
    bt = B
    assert bt == SUBLANES and L % tl == 0 and tl % CT == 0
    n_levels = tl.bit_length() - 1
    assert (1 << n_levels) == tl and n_levels * N_HEADS_B >= D_MODEL // LANES
    rows = tl * bt
    tail = (CONV_W - 1) * bt

    def const(shape):
        nd = len(shape)
        return pl.BlockSpec(shape, lambda j: (0,) * nd, pipeline_mode=pl.Buffered(1))
    in_specs = [pl.BlockSpec((bt, tl, D_MODEL), lambda j: (0, j, 0)), const(mod.shape)] + [const(a.shape) for a in w]
    out_specs = [
        pl.BlockSpec((bt, tl, D_MODEL), lambda j: (0, j, 0)),
        pl.BlockSpec((DEPTH, bt, D_A), lambda j: (0, 0, 0)),
        pl.BlockSpec((DEPTH, CONV_W - 1, bt, D_A), lambda j: (0, 0, 0, 0)),
        pl.BlockSpec((DEPTH, bt, N_HEADS_B, DK_B, DV_B), lambda j: (0, 0, 0, 0, 0)),
    ]
    out_shape = [
        jax.ShapeDtypeStruct((B, L, D_MODEL), F32),
        jax.ShapeDtypeStruct((DEPTH, B, D_A), F32),
        jax.ShapeDtypeStruct((DEPTH, CONV_W - 1, B, D_A), F32),
        jax.ShapeDtypeStruct((DEPTH, B, N_HEADS_B, DK_B, DV_B), F32),
    ]
    slab = lambda n: pltpu.VMEM((n, rows, LANES), F32)
    wide = lambda: pltpu.VMEM((rows, D_A), F32)
    scratch = [
        slab(D_MODEL // LANES),
        pltpu.VMEM((rows, D_MODEL), BF16),
        pltpu.VMEM((rows + tail, D_A), F32),
        pltpu.VMEM((DEPTH, tail, D_A), F32),
        wide(), wide(), wide(), wide(),
        wide(), wide(), slab(N_HEADS_B), wide(),
        pltpu.VMEM((N_HEADS_B, bt, DK_B), F32),
        pltpu.VMEM((DEPTH, bt, D_A), F32),
        pltpu.VMEM((DEPTH, bt, N_HEADS_B, DV_B, DK_B), F32),
        pltpu.VMEM((n_levels, N_HEADS_B, rows, LANES), F32),
        slab(N_HEADS_B), slab(N_HEADS_B), slab(N_HEADS_B),
    ]
    body = functools.partial(_prompt_kernel, bt=bt, tl=tl, n_levels=n_levels)
    return pl.pallas_call(
        body,
        grid=(L // tl,),
        in_specs=in_specs,
        out_specs=out_specs,
        out_shape=out_shape,
        scratch_shapes=scratch,
        compiler_params=pltpu.CompilerParams(dimension_semantics=("arbitrary",), vmem_limit_bytes=VMEM_LIMIT),
        name="prompt_layers",
    )(x, mod, *w)


def _gate_weights(wa, wx):
    per = LANES // BLK_A

    def bd(w):
        w = w.reshape(D_A // LANES, per, BLK_A, BLK_A)
        eye = jnp.eye(per, dtype=w.dtype)
        return jnp.einsum('jpcd,pq->jpcqd', w, eye).reshape(D_A // LANES, LANES, LANES)
    return jnp.concatenate([bd(wa), bd(wx)], axis=-1).astype(BF16)


def kernel(x_prompt, x_sample, state_rglru_h, state_rglru_conv, state_hgrn_S, c_prompt, c_sample, ada_w, ada_b,
           pre_norm_w, post_norm_w, w_in, conv_w, conv_b, rg_wa, rg_ba, rg_wx, rg_bx, rg_lambda, hg_lb_logits,
           hg_onorm_w, w_out):
    bp, bs = x_prompt.shape[0], x_sample.shape[0]
    mod = _modulation(jnp.concatenate([c_prompt, c_sample], axis=0), ada_w.astype(BF16),
                      ada_b.reshape(DEPTH, 1, 3 * D_MODEL))
    w_in_bf, w_out_bf = w_in.astype(BF16), w_out.astype(BF16)
    wg = jnp.stack([_gate_weights(rg_wa[l], rg_wx[l]) for l in range(DEPTH)])
    row = lambda a: a[:, None, :]
    wp = (row(pre_norm_w), row(post_norm_w), w_in_bf, conv_w, row(conv_b), wg, row(rg_ba), row(rg_bx),
          row(rg_lambda), hg_lb_logits, row(hg_onorm_w), w_out_bf)
    yp, hp, cvp, sp = _prompt(x_prompt, mod[:, :, :bp], wp, tl=64)

    xs = jnp.transpose(x_sample, (1, 0, 2))
    outs_s = []
    ss = None
    for l in range(DEPTH):
        w = (pre_norm_w[l][None], post_norm_w[l][None], w_in_bf[l], conv_w[l], conv_b[l][None],
             wg[l], rg_ba[l][None], rg_bx[l][None], rg_lambda[l][None],
             hg_lb_logits, hg_onorm_w[l][None], w_out_bf[l])
        xs, h, cv, ss = _decode_layer(xs, mod[l, :, bp:], state_rglru_h[l],
                                      jnp.transpose(state_rglru_conv[l], (1, 0, 2)), state_hgrn_S, ss, w, layer=l, bt=16)
        outs_s.append((h, jnp.transpose(cv, (1, 0, 2))))
    stack = lambda outs, k: jnp.stack([o[k] for o in outs])
    return (yp, jnp.transpose(xs, (1, 0, 2)), hp, jnp.transpose(cvp, (0, 2, 1, 3)), sp,
            stack(outs_s, 0), stack(outs_s, 1), ss)
```

```python
import functools

import jax
import jax.numpy as jnp
from jax import lax
from jax.experimental import pallas as pl
from jax.experimental.pallas import tpu as pltpu

D_MODEL = 1024
DEPTH = 2
D_A = 512
N_BLK_A = 8
BLK_A = 64
CONV_W = 4
C_RGLRU = 8.0
D_B = 512
N_HEADS_B = 4
DK_B = 128
DV_B = 128
P_IN = 2 * D_A + 4 * D_B
EPS = 1e-6
F_MIN = 1e-30

LANES = 128
SUBLANES = 8
VMEM_LIMIT = 60 * 1024 * 1024

BF16 = jnp.bfloat16
F32 = jnp.float32


def _dot(a, b):
    return jnp.dot(a, b, preferred_element_type=F32)


def _dot_nt(a, b):
    return lax.dot_general(a, b, (((1,), (1,)), ((), ())), preferred_element_type=F32)


def _dot_tn(a, b):
    return lax.dot_general(a, b, (((0,), (0,)), ((), ())), preferred_element_type=F32)


def _rms(x, w):
    return x * lax.rsqrt(jnp.mean(x * x, axis=-1, keepdims=True) + EPS) * w


def _silu(x):
    return x * jax.nn.sigmoid(x)


def _mod_kernel(c_ref, w_ref, b_ref, o_ref):
    o_ref[...] = _dot(_silu(c_ref[...]).astype(BF16), w_ref[...]) + b_ref[...]


def _modulation(c_all, ada_w_bf, ada_b):
    nb = c_all.shape[0]
    return pl.pallas_call(
        _mod_kernel,
        grid=(DEPTH, 3),
        in_specs=[
            pl.BlockSpec((nb, D_MODEL), lambda l, k: (0, 0)),
            pl.BlockSpec((None, D_MODEL, D_MODEL), lambda l, k: (l, 0, k)),
            pl.BlockSpec((None, 1, D_MODEL), lambda l, k: (l, 0, k)),
        ],
        out_specs=pl.BlockSpec((None, None, nb, D_MODEL), lambda l, k: (l, k, 0, 0)),
        out_shape=jax.ShapeDtypeStruct((DEPTH, 3, nb, D_MODEL), F32),
        name="adaln_mod",
    )(c_all, ada_w_bf, ada_b)


TP = SUBLANES


def _sig(x):
    return 0.5 * jnp.tanh(0.5 * x) + 0.5


def _decode_kernel(x_ref, mod_ref, h0_ref, conv0_ref, s0_ref, prew_ref, postw_ref, win_ref, convw_ref,
                   convb_ref, wg_ref, ba_ref, bx_ref, lam_ref, lbl_ref, onw_ref, wout_ref,
                   y_ref, hn_ref, convn_ref, sn_ref,
                   xt_s, hin_s, xpad_s, ga_s, a_s, u_s, q_s, kk_s, b_s, gb_s, oi_s, qe8_s, kdx8_s, v8_s, oc8_s,
                   *, nb, bt, tl):
    m_s = hin_s
    i = pl.program_id(0)
    n_blk = nb // bt
    layer = i // n_blk
    blk = i % n_blk
    rows = tl * nb
    tail = (CONV_W - 1) * nb
    nh = N_HEADS_B
    heads = [slice(hd * DK_B, (hd + 1) * DK_B) for hd in range(nh)]

    @pl.when(i == 0)
    def _():
        xt_s[...] = x_ref[...]

    @pl.when(blk == 0)
    def _():
        shift, scale = mod_ref[0], mod_ref[1]
        pm = prew_ref[...] * (1.0 + scale)
        for t in range(tl):
            xt = xt_s[t]
            inv = lax.rsqrt(jnp.mean(xt * xt, axis=-1, keepdims=True) + EPS)
            hin_s[pl.ds(t * nb, nb), :] = (xt * inv * pm + shift).astype(BF16)

        def proj(c0, width):
            return _dot(hin_s[...], win_ref[:, c0:c0 + width])
        xpad_s[pl.ds(0, tail), :] = conv0_ref[...].reshape(tail, D_A)
        xpad_s[pl.ds(tail, rows), :] = proj(0, D_A)
        ga_s[...] = proj(D_A, D_A)
        q_s[...] = proj(2 * D_A, D_B)
        kk_s[...] = proj(2 * D_A + D_B, D_B)
        iv = proj(2 * D_A + 2 * D_B, D_B)
        gb_s[...] = proj(2 * D_A + 3 * D_B, D_B)

        xc = convb_ref[...] + sum(xpad_s[pl.ds(k * nb, rows), :] * convw_ref[k:k + 1, :] for k in range(CONV_W))
        convn_ref[...] = xpad_s[pl.ds(rows, tail), :].reshape(CONV_W - 1, nb, D_A)
        xcb = xc.astype(BF16)
        for c in range(D_A // LANES):
            g2 = _dot(xcb[:, c * LANES:(c + 1) * LANES], wg_ref[c])
            a_s[:, c * LANES:(c + 1) * LANES] = g2[:, :LANES]
            u_s[:, c * LANES:(c + 1) * LANES] = g2[:, LANES:]
        coef = -C_RGLRU * jax.nn.softplus(-lam_ref[...])
        h = h0_ref[...]
        for t in range(tl):
            sl = pl.ds(t * nb, nb)
            r = _sig(a_s[sl, :] + ba_ref[...])
            ig = _sig(u_s[sl, :] + bx_ref[...])
            log_a = coef * r
            a = jnp.exp(log_a)
            u = jnp.sqrt(jnp.maximum(jnp.tanh(-log_a) * (1.0 + a * a), 0.0)) * (ig * xc[t * nb:(t + 1) * nb])
            h = a * h + u
            ga = ga_s[sl, :]
            m_s[sl, :D_A] = (h * (ga * _sig(ga))).astype(BF16)
        hn_ref[...] = h

        logits = lbl_ref[...]
        pe = jnp.exp(logits - jnp.max(logits, axis=0, keepdims=True))
        probs = pe / jnp.sum(pe, axis=0, keepdims=True)
        lb = jnp.zeros((1, D_B), F32)
        for k in range(1, DEPTH):
            lb = lb + jnp.where(layer >= k, probs[k:k + 1], 0.0)
        c0 = 0.5 * (1.0 + lb)
        c1 = 0.5 * (1.0 - lb)
        bl = jnp.zeros((nb, D_B), F32)
        for t in range(tl):
            sl = pl.ds(t * nb, nb)
            p = c1 * jnp.tanh(0.5 * kk_s[sl, :])
            bl = bl + jnp.log(jnp.maximum(c0 + p, F_MIN))
            b_s[sl, :] = bl
            kk_s[sl, :] = c1 - p

        for t in range(tl):
            st_ = pl.ds(t * nb, nb)
            acc = [jnp.zeros((nb, DV_B), F32) for _ in range(nh)]
            for s in range(t + 1):
                ss_ = pl.ds(s * nb, nb)
                w = q_s[st_, :] * kk_s[ss_, :]
                if s < t:
                    w = w * jnp.exp(b_s[st_, :] - b_s[ss_, :])
                for hd in range(nh):
                    acc[hd] = acc[hd] + jnp.sum(w[:, heads[hd]], axis=-1, keepdims=True) * iv[s * nb:(s + 1) * nb, heads[hd]]
            for hd in range(nh):
                oi_s[st_, heads[hd]] = acc[hd]

        zrow = jnp.zeros((nb, LANES), F32)
        ebl = jnp.exp(bl)
        e_hi = ebl.astype(BF16).astype(F32)
        e_mid = (ebl - e_hi).astype(BF16).astype(F32)
        e_lo = ebl - e_hi - e_mid
        for hd in range(nh):
            for t in range(TP):
                dst = pl.ds(t, nb, stride=TP)
                if t < tl:
                    st_ = pl.ds(t * nb, nb)
                    bt_ = b_s[st_, heads[hd]]
                    qe8_s[hd, dst, :] = q_s[st_, heads[hd]] * jnp.exp(bt_)
                    kdx8_s[hd, dst, :] = kk_s[st_, heads[hd]] * jnp.exp(bl[:, heads[hd]] - bt_)
                    v8_s[hd, dst, :] = iv[t * nb:(t + 1) * nb, heads[hd]]
                else:
                    qe8_s[hd, dst, :] = zrow
                    v8_s[hd, dst, :] = zrow
                    dec = (e_hi, e_mid, e_lo)
                    kdx8_s[hd, dst, :] = dec[t - tl][:, heads[hd]] if t - tl < len(dec) else zrow

    rid = lax.broadcasted_iota(jnp.int32, (TP, LANES), 0)
    ones_rows = jnp.where((rid >= tl) & (rid < tl + 3), 1.0, 0.0)
    for bl_ in range(bt):
        r8 = pl.ds(pl.multiple_of((blk * bt + bl_) * TP, TP), TP)
        for hd in range(nh):
            s_old = s0_ref[bl_, hd]
            oc8_s[hd, r8, :] = _dot(qe8_s[hd, r8, :].astype(BF16), s_old.astype(BF16))
            rhs = jnp.concatenate([v8_s[hd, r8, :], ones_rows], axis=1).astype(BF16)
            g = _dot_tn(kdx8_s[hd, r8, :].astype(BF16), rhs)
            sn_ref[bl_, hd] = s_old * g[:, DV_B:] + g[:, :DV_B]

    @pl.when(blk == n_blk - 1)
    def _():
        for t in range(tl):
            st_ = pl.ds(t * nb, nb)
            for hd in range(nh):
                oh = oi_s[st_, heads[hd]] + oc8_s[hd, pl.ds(t, nb, stride=TP), :]
                gb = gb_s[st_, heads[hd]]
                yb = _rms(oh, onw_ref[:, heads[hd]]) * (gb * _sig(gb))
                m_s[st_, D_A + hd * DV_B:D_A + (hd + 1) * DV_B] = yb.astype(BF16)
        mo = _dot(m_s[...], wout_ref[...])
        gp = postw_ref[...] * mod_ref[2]
        for t in range(tl):
            mt = mo[t * nb:(t + 1) * nb]
            inv = lax.rsqrt(jnp.mean(mt * mt, axis=-1, keepdims=True) + EPS)
            xt_s[t] = xt_s[t] + mt * inv * gp

    @pl.when(i == pl.num_programs(0) - 1)
    def _():
        y_ref[...] = xt_s[...]


def _decode(x_tm, mod, h0, conv0_tm, s_all, w, *, bt):
    tl, nb, _ = x_tm.shape
    assert nb % bt == 0 and nb % SUBLANES == 0 and CONV_W - 1 <= tl and tl + 3 <= TP
    rows = tl * nb
    n_blk = nb // bt

    def whole(a):
        nd = a.ndim
        return pl.BlockSpec(a.shape, lambda i: (0,) * nd, pipeline_mode=pl.Buffered(1))

    def per_layer(a):
        nd = a.ndim - 1
        return pl.BlockSpec((None,) + a.shape[1:], lambda i: (i // n_blk,) + (0,) * nd, pipeline_mode=pl.Buffered(1))
    state_spec = pl.BlockSpec((None, bt, N_HEADS_B, DK_B, DV_B), lambda i: (i // n_blk, i % n_blk, 0, 0, 0))
    args = [x_tm, mod, h0, conv0_tm, s_all, *w]
    in_specs = [whole(x_tm), per_layer(mod), per_layer(h0), per_layer(conv0_tm), state_spec]
    in_specs += [whole(a) if a.shape[0] == DEPTH and a.ndim == 2 else per_layer(a) for a in w]
    out_specs = [
        pl.BlockSpec((tl, nb, D_MODEL), lambda i: (0, 0, 0)),
        pl.BlockSpec((None, nb, D_A), lambda i: (i // n_blk, 0, 0)),
        pl.BlockSpec((None, CONV_W - 1, nb, D_A), lambda i: (i // n_blk, 0, 0, 0)),
        state_spec,
    ]
    out_shape = [
        jax.ShapeDtypeStruct((tl, nb, D_MODEL), F32),
        jax.ShapeDtypeStruct((DEPTH, nb, D_A), F32),
        jax.ShapeDtypeStruct((DEPTH, CONV_W - 1, nb, D_A), F32),
        jax.ShapeDtypeStruct(s_all.shape, F32),
    ]
    wide = lambda: pltpu.VMEM((rows, D_A), F32)
    slab8 = lambda: pltpu.VMEM((N_HEADS_B, nb * TP, LANES), F32)
    scratch = [
        pltpu.VMEM((tl, nb, D_MODEL), F32),
        pltpu.VMEM((rows, D_MODEL), BF16),
        pltpu.VMEM((rows + (CONV_W - 1) * nb, D_A), F32),
        wide(), wide(), wide(),
        wide(), wide(), wide(), wide(),
        wide(),
        slab8(), slab8(), slab8(), slab8(),
    ]
    return pl.pallas_call(
        functools.partial(_decode_kernel, nb=nb, bt=bt, tl=tl),
        grid=(DEPTH * n_blk,),
        in_specs=in_specs,
        out_specs=out_specs,
        out_shape=out_shape,
        scratch_shapes=scratch,
        compiler_params=pltpu.CompilerParams(dimension_semantics=("arbitrary",), vmem_limit_bytes=VMEM_LIMIT),
        name="decode_layers",
    )(*args)


CT = SUBLANES


def _prompt_kernel(x_ref, mod_ref, prew_ref, postw_ref, win_ref, convw_ref, convb_ref, wg_ref, ba_ref, bx_ref,
                   lam_ref, lbl_ref, onw_ref, wout_ref,
                   y_ref, hn_ref, convn_ref, sn_ref,
                   xt_s, hin_s, xpad_s, ccar_s, ga_s, a_s, u_s, xc_s, q_s, kk_s, b_s, gb_s, bl_s, hcar_s, st_s,
                   xl_s, qe_s, kd_s, v_s,
                   *, bt, tl, n_levels):
    o_s = b_s
    m_s = hin_s
    j = pl.program_id(0)
    nj = pl.num_programs(0)
    rows = tl * bt
    ch = CT * bt
    n_ch = tl // CT
    nh = N_HEADS_B
    nt = D_MODEL // LANES
    tail = (CONV_W - 1) * bt

    @pl.when(j == 0)
    def _():
        hcar_s[...] = jnp.zeros_like(hcar_s)
        ccar_s[...] = jnp.zeros_like(ccar_s)
        st_s[...] = jnp.zeros_like(st_s)

    for c in range(n_ch):
        for jl in range(nt):
            for b in range(bt):
                xt_s[jl, pl.ds(c * ch + b, CT, stride=bt), :] = x_ref[b, c * CT:(c + 1) * CT, jl * LANES:(jl + 1) * LANES]

    ti = lax.broadcasted_iota(jnp.int32, (tl, tl), 0)
    si = lax.broadcasted_iota(jnp.int32, (tl, tl), 1)
    masks = []
    for lv in range(n_levels):
        h = 1 << lv
        masks.append(((ti // (2 * h)) == (si // (2 * h))) & ((ti // h) % 2 == 1) & ((si // h) % 2 == 0))

    def bcast(v):
        return jnp.concatenate([v] * CT, axis=0)

    for l in range(DEPTH):
        shift, scale, gate = mod_ref[l, 0], mod_ref[l, 1], mod_ref[l, 2]
        pm = prew_ref[l] * (1.0 + scale)
        gp = postw_ref[l] * gate

        for c in range(n_ch):
            rs = pl.ds(c * ch, ch)
            ss = jnp.sum(sum(jnp.square(xt_s[jl, rs, :]) for jl in range(nt)), axis=-1, keepdims=True)
            inv = lax.rsqrt(ss * (1.0 / D_MODEL) + EPS)
            for jl in range(nt):
                ls = slice(jl * LANES, (jl + 1) * LANES)
                hin_s[rs, ls] = (xt_s[jl, rs, :] * inv * bcast(pm[:, ls]) + bcast(shift[:, ls])).astype(BF16)

        def proj(c0, width):
            return _dot(hin_s[...], win_ref[l, :, c0:c0 + width])
        q_s[...] = proj(2 * D_A, D_B)
        kk_s[...] = proj(2 * D_A + D_B, D_B)
        iv = proj(2 * D_A + 2 * D_B, D_B)
        for hd in range(nh):
            v_s[hd] = iv[:, hd * DV_B:(hd + 1) * DV_B]
        gb_s[...] = proj(2 * D_A + 3 * D_B, D_B)

        logits = lbl_ref[...]
        pe = jnp.exp(logits - jnp.max(logits, axis=0, keepdims=True))
        probs = pe / jnp.sum(pe, axis=0, keepdims=True)
        lb = jnp.sum(probs[1:l + 1], axis=0, keepdims=True) if l > 0 else jnp.zeros((1, D_B), F32)
        c0 = 0.5 * (1.0 + lb)
        c1 = 0.5 * (1.0 - lb)
        bl = jnp.zeros((bt, D_B), F32)
        for t in range(tl):
            sl = pl.ds(t * bt, bt)
            p = c1 * jnp.tanh(0.5 * kk_s[sl, :])
            bl = bl + jnp.log(jnp.maximum(c0 + p, F_MIN))
            for hd in range(nh):
                b_s[hd, sl, :] = bl[:, hd * DK_B:(hd + 1) * DK_B]
            kk_s[sl, :] = c1 - p
        ebl = jnp.exp(bl)
        for hd in range(nh):
            bl_s[hd] = ebl[:, hd * DK_B:(hd + 1) * DK_B]

        for c in range(n_ch):
            rs = pl.ds(c * ch, ch)
            for hd in range(nh):
                ls = slice(hd * DK_B, (hd + 1) * DK_B)
                bc, qc, kc = b_s[hd, rs, :], q_s[rs, ls], kk_s[rs, ls]
                qe_s[hd, rs, :] = qc * jnp.exp(bc)
                kd_s[hd, rs, :] = kc * jnp.exp(bcast(bl[:, ls]) - bc)
                for lv in range(n_levels):
                    h = 1 << lv
                    if 2 * h <= CT:
                        parts = []
                        for t in range(CT):
                            mid = (t // (2 * h)) * 2 * h + h - 1
                            ts, ms = slice(t * bt, (t + 1) * bt), slice(mid * bt, (mid + 1) * bt)
                            if t % (2 * h) >= h:
                                parts.append(qc[ts] * jnp.exp(bc[ts] - bc[ms]))
                            else:
                                parts.append(kc[ts] * jnp.exp(bc[ms] - bc[ts]))
                        xl = jnp.concatenate(parts, axis=0)
                    else:
                        t0 = c * CT
                        mid = (t0 // (2 * h)) * 2 * h + h - 1
                        bmid = bcast(b_s[hd, pl.ds(mid * bt, bt), :])
                        xl = qc * jnp.exp(bc - bmid) if t0 % (2 * h) >= h else kc * jnp.exp(bmid - bc)
                    xl_s[lv, hd, rs, :] = xl

        xpad_s[pl.ds(0, tail), :] = ccar_s[l]
        xpad_s[pl.ds(tail, rows), :] = proj(0, D_A)
        ga_s[...] = proj(D_A, D_A)
        cw = convw_ref[l]
        for c in range(n_ch):
            xc = convb_ref[l] + sum(xpad_s[pl.ds(c * ch + k * bt, ch), :] * cw[k:k + 1, :] for k in range(CONV_W))
            xc_s[pl.ds(c * ch, ch), :] = xc
        ctail = xpad_s[pl.ds(rows, tail), :]
        ccar_s[l] = ctail
        for c in range(D_A // LANES):
            g2 = _dot(xc_s[:, c * LANES:(c + 1) * LANES].astype(BF16), wg_ref[l, c])
            a_s[:, c * LANES:(c + 1) * LANES] = g2[:, :LANES]
            u_s[:, c * LANES:(c + 1) * LANES] = g2[:, LANES:]
        coef = -C_RGLRU * jax.nn.softplus(-lam_ref[l])
        ba, bx = ba_ref[l], bx_ref[l]

        def scan_chunk(c, h):
            for tp in range(CT // 2):
                yas = []
                for t2 in range(2):
                    sl = pl.ds(c * ch + (2 * tp + t2) * bt, bt)
                    r = _sig(a_s[sl, :] + ba)
                    i = _sig(u_s[sl, :] + bx)
                    log_a = coef * r
                    a = jnp.exp(log_a)
                    u = jnp.sqrt(jnp.maximum(jnp.tanh(-log_a) * (1.0 + a * a), 0.0)) * (i * xc_s[sl, :])
                    h = a * h + u
                    ga = ga_s[sl, :]
                    yas.append(h * (ga * _sig(ga)))
                m_s[pl.ds(c * ch + 2 * tp * bt, 2 * bt), :D_A] = jnp.concatenate(yas, axis=0).astype(BF16)
            return h

        def chains(b):
            sl = pl.ds(b, tl, stride=bt)
            res = []
            for hd in range(nh):
                amat = jnp.zeros((tl, tl), F32)
                for lv in range(n_levels):
                    xv = xl_s[lv, hd, sl, :].astype(BF16)
                    amat = amat + jnp.where(masks[lv], _dot_nt(xv, xv), 0.0)
                v = v_s[hd, sl, :].astype(BF16)
                st = st_s[l, b, hd]
                o = _dot(amat.astype(BF16), v) + _dot_nt(qe_s[hd, sl, :].astype(BF16), st.astype(BF16))
                st_new = st * bl_s[hd, b:b + 1, :] + _dot_tn(v, kd_s[hd, sl, :].astype(BF16))
                res.append((o, st_new))
            for hd in range(nh):
                o_s[hd, sl, :] = res[hd][0]
                st_s[l, b, hd] = res[hd][1]

        h_last = hcar_s[l]
        for idx in range(max(bt, n_ch)):
            if idx < bt:
                chains(idx)
            if idx < n_ch:
                h_last = scan_chunk(idx, h_last)
        hcar_s[l] = h_last

        for c in range(n_ch):
            rs = pl.ds(c * ch, ch)
            for hd in range(nh):
                ls = slice(hd * DK_B, (hd + 1) * DK_B)
                dsum = jnp.sum(q_s[rs, ls] * kk_s[rs, ls], axis=-1, keepdims=True)
                oh = o_s[hd, rs, :] + dsum * v_s[hd, rs, :]
                gb = gb_s[rs, ls]
                yb = _rms(oh, onw_ref[l][:, ls]) * (gb * _sig(gb))
                m_s[rs, D_A + hd * DV_B:D_A + (hd + 1) * DV_B] = yb.astype(BF16)

        mo = _dot(m_s[...], wout_ref[l])
        for jl in range(nt):
            xl_s[jl // nh, jl % nh] = mo[:, jl * LANES:(jl + 1) * LANES]
        for c in range(n_ch):
            rs = pl.ds(c * ch, ch)
            ss = jnp.sum(sum(jnp.square(xl_s[jl // nh, jl % nh, rs, :]) for jl in range(nt)), axis=-1, keepdims=True)
            inv = lax.rsqrt(ss * (1.0 / D_MODEL) + EPS)
            for jl in range(nt):
                ls = slice(jl * LANES, (jl + 1) * LANES)
                xt_s[jl, rs, :] = xt_s[jl, rs, :] + xl_s[jl // nh, jl % nh, rs, :] * inv * bcast(gp[:, ls])

        @pl.when(j == nj - 1)
        def _():
            hn_ref[l] = h_last
            convn_ref[l] = ctail.reshape(CONV_W - 1, bt, D_A)
            for b in range(bt):
                for hd in range(nh):
                    sn_ref[l, b, hd] = st_s[l, b, hd].T

    for c in range(n_ch):
        for jl in range(nt):
            for b in range(bt):
                y_ref[b, c * CT:(c + 1) * CT, jl * LANES:(jl + 1) * LANES] = xt_s[jl, pl.ds(c * ch + b, CT, stride=bt), :]


def _prompt(x, mod, w, *, tl):
    B, L, _ = x.shape
    bt = B
    assert bt == SUBLANES and L % tl == 0 and tl % CT == 0
    n_levels = tl.bit_length() - 1
    assert (1 << n_levels) == tl and n_levels * N_HEADS_B >= D_MODEL // LANES
    rows = tl * bt
    tail = (CONV_W - 1) * bt

    def const(shape):
        nd = len(shape)
        return pl.BlockSpec(shape, lambda j: (0,) * nd, pipeline_mode=pl.Buffered(1))
    in_specs = [pl.BlockSpec((bt, tl, D_MODEL), lambda j: (0, j, 0)), const(mod.shape)] + [const(a.shape) for a in w]
    out_specs = [
        pl.BlockSpec((bt, tl, D_MODEL), lambda j: (0, j, 0)),
        pl.BlockSpec((DEPTH, bt, D_A), lambda j: (0, 0, 0)),
        pl.BlockSpec((DEPTH, CONV_W - 1, bt, D_A), lambda j: (0, 0, 0, 0)),
        pl.BlockSpec((DEPTH, bt, N_HEADS_B, DK_B, DV_B), lambda j: (0, 0, 0, 0, 0)),
    ]
    out_shape = [
        jax.ShapeDtypeStruct((B, L, D_MODEL), F32),
        jax.ShapeDtypeStruct((DEPTH, B, D_A), F32),
        jax.ShapeDtypeStruct((DEPTH, CONV_W - 1, B, D_A), F32),
        jax.ShapeDtypeStruct((DEPTH, B, N_HEADS_B, DK_B, DV_B), F32),
    ]
    slab = lambda n: pltpu.VMEM((n, rows, LANES), F32)
    wide = lambda: pltpu.VMEM((rows, D_A), F32)
    scratch = [
        slab(D_MODEL // LANES),
        pltpu.VMEM((rows, D_MODEL), BF16),
        pltpu.VMEM((rows + tail, D_A), F32),
        pltpu.VMEM((DEPTH, tail, D_A), F32),
        wide(), wide(), wide(), wide(),
        wide(), wide(), slab(N_HEADS_B), wide(),
        pltpu.VMEM((N_HEADS_B, bt, DK_B), F32),
        pltpu.VMEM((DEPTH, bt, D_A), F32),
        pltpu.VMEM((DEPTH, bt, N_HEADS_B, DV_B, DK_B), F32),
        pltpu.VMEM((n_levels, N_HEADS_B, rows, LANES), F32),
        slab(N_HEADS_B), slab(N_HEADS_B), slab(N_HEADS_B),
    ]
    body = functools.partial(_prompt_kernel, bt=bt, tl=tl, n_levels=n_levels)
    return pl.pallas_call(
        body,
        grid=(L // tl,),
        in_specs=in_specs,
        out_specs=out_specs,
        out_shape=out_shape,
        scratch_shapes=scratch,
        compiler_params=pltpu.CompilerParams(dimension_semantics=("arbitrary",), vmem_limit_bytes=VMEM_LIMIT),
        name="prompt_layers",
    )(x, mod, *w)


def _gate_weights(wa, wx):
    per = LANES // BLK_A

    def bd(w):
        w = w.reshape(D_A // LANES, per, BLK_A, BLK_A)
        eye = jnp.eye(per, dtype=w.dtype)
        return jnp.einsum('jpcd,pq->jpcqd', w, eye).reshape(D_A // LANES, LANES, LANES)
    return jnp.concatenate([bd(wa), bd(wx)], axis=-1).astype(BF16)


def kernel(x_prompt, x_sample, state_rglru_h, state_rglru_conv, state_hgrn_S, c_prompt, c_sample, ada_w, ada_b,
           pre_norm_w, post_norm_w, w_in, conv_w, conv_b, rg_wa, rg_ba, rg_wx, rg_bx, rg_lambda, hg_lb_logits,
           hg_onorm_w, w_out):
    bp, bs = x_prompt.shape[0], x_sample.shape[0]
    mod = _modulation(jnp.concatenate([c_prompt, c_sample], axis=0), ada_w.astype(BF16),
                      ada_b.reshape(DEPTH, 1, 3 * D_MODEL))
    w_in_bf, w_out_bf = w_in.astype(BF16), w_out.astype(BF16)
    wg = jnp.stack([_gate_weights(rg_wa[l], rg_wx[l]) for l in range(DEPTH)])
    row = lambda a: a[:, None, :]
    wp = (row(pre_norm_w), row(post_norm_w), w_in_bf, conv_w, row(conv_b), wg, row(rg_ba), row(rg_bx),
          row(rg_lambda), hg_lb_logits, row(hg_onorm_w), w_out_bf)
    yp, hp, cvp, sp = _prompt(x_prompt, mod[:, :, :bp], wp, tl=64)

    ys, hs, cvs, ss = _decode(jnp.transpose(x_sample, (1, 0, 2)), mod[:, :, bp:], state_rglru_h,
                              jnp.transpose(state_rglru_conv, (0, 2, 1, 3)), state_hgrn_S, wp, bt=16)
    return (yp, jnp.transpose(ys, (1, 0, 2)), hp, jnp.transpose(cvp, (0, 2, 1, 3)), sp,
            hs, jnp.transpose(cvs, (0, 2, 1, 3)), ss)
```

```python
import functools

import jax
import jax.numpy as jnp
from jax import lax
from jax.experimental import pallas as pl
from jax.experimental.pallas import tpu as pltpu

D_MODEL = 1024
DEPTH = 2
D_A = 512
N_BLK_A = 8
BLK_A = 64
CONV_W = 4
C_RGLRU = 8.0
D_B = 512
N_HEADS_B = 4
DK_B = 128
DV_B = 128
P_IN = 2 * D_A + 4 * D_B
EPS = 1e-6
F_MIN = 1e-30

LANES = 128
SUBLANES = 8
VMEM_LIMIT = 60 * 1024 * 1024

BF16 = jnp.bfloat16
F32 = jnp.float32


def _dot(a, b):
    return jnp.dot(a, b, preferred_element_type=F32)


def _dot_nt(a, b):
    return lax.dot_general(a, b, (((1,), (1,)), ((), ())), preferred_element_type=F32)


def _dot_tn(a, b):
    return lax.dot_general(a, b, (((0,), (0,)), ((), ())), preferred_element_type=F32)


def _rms(x, w):
    return x * lax.rsqrt(jnp.mean(x * x, axis=-1, keepdims=True) + EPS) * w


def _silu(x):
    return x * jax.nn.sigmoid(x)


def _mod_kernel(c_ref, w_ref, b_ref, o_ref):
    o_ref[...] = _dot(_silu(c_ref[...]).astype(BF16), w_ref[...].astype(BF16)) + b_ref[...]


def _modulation(c_all, ada_w, ada_b):
    nb = c_all.shape[0]
    return pl.pallas_call(
        _mod_kernel,
        grid=(DEPTH, 3),
        in_specs=[
            pl.BlockSpec((nb, D_MODEL), lambda l, k: (0, 0)),
            pl.BlockSpec((None, D_MODEL, D_MODEL), lambda l, k: (l, 0, k)),
            pl.BlockSpec((None, 1, D_MODEL), lambda l, k: (l, 0, k)),
        ],
        out_specs=pl.BlockSpec((None, None, nb, D_MODEL), lambda l, k: (l, k, 0, 0)),
        out_shape=jax.ShapeDtypeStruct((DEPTH, 3, nb, D_MODEL), F32),
        name="adaln_mod",
    )(c_all, ada_w, ada_b)


TP = SUBLANES


def _sig(x):
    return 0.5 * jnp.tanh(0.5 * x) + 0.5


def _decode_kernel(x_ref, mod_ref, h0_ref, conv0_ref, s0_ref, prew_ref, postw_ref, win_ref, convw_ref,
                   convb_ref, wg_ref, ba_ref, bx_ref, lam_ref, lbl_ref, onw_ref, wout_ref,
                   y_ref, hn_ref, convn_ref, sn_ref,
                   xt_s, hin_s, xpad_s, ga_s, a_s, u_s, q_s, kk_s, b_s, gb_s, oi_s, qe8_s, kdx8_s, v8_s, oc8_s,
                   *, nb, bt, tl):
    m_s = hin_s
    i = pl.program_id(0)
    n_blk = nb // bt
    layer = i // n_blk
    blk = i % n_blk
    rows = tl * nb
    tail = (CONV_W - 1) * nb
    nh = N_HEADS_B
    heads = [slice(hd * DK_B, (hd + 1) * DK_B) for hd in range(nh)]

    @pl.when(i == 0)
    def _():
        xt_s[...] = x_ref[...]

    @pl.when(blk == 0)
    def _():
        shift, scale = mod_ref[0], mod_ref[1]
        pm = prew_ref[...] * (1.0 + scale)
        for t in range(tl):
            xt = xt_s[t]
            inv = lax.rsqrt(jnp.mean(xt * xt, axis=-1, keepdims=True) + EPS)
            hin_s[pl.ds(t * nb, nb), :] = (xt * inv * pm + shift).astype(BF16)

        def proj(c0, width):
            return _dot(hin_s[...], win_ref[:, c0:c0 + width])
        xpad_s[pl.ds(0, tail), :] = conv0_ref[...].reshape(tail, D_A)
        xpad_s[pl.ds(tail, rows), :] = proj(0, D_A)
        ga_s[...] = proj(D_A, D_A)
        q_s[...] = proj(2 * D_A, D_B)
        kk_s[...] = proj(2 * D_A + D_B, D_B)
        iv = proj(2 * D_A + 2 * D_B, D_B)
        gb_s[...] = proj(2 * D_A + 3 * D_B, D_B)

        xc = convb_ref[...] + sum(xpad_s[pl.ds(k * nb, rows), :] * convw_ref[k:k + 1, :] for k in range(CONV_W))
        convn_ref[...] = xpad_s[pl.ds(rows, tail), :].reshape(CONV_W - 1, nb, D_A)
        xcb = xc.astype(BF16)
        for c in range(D_A // LANES):
            g2 = _dot(xcb[:, c * LANES:(c + 1) * LANES], wg_ref[c])
            a_s[:, c * LANES:(c + 1) * LANES] = g2[:, :LANES]
            u_s[:, c * LANES:(c + 1) * LANES] = g2[:, LANES:]
        coef = -C_RGLRU * jax.nn.softplus(-lam_ref[...])
        h = h0_ref[...]
        for t in range(tl):
            sl = pl.ds(t * nb, nb)
            r = _sig(a_s[sl, :] + ba_ref[...])
            ig = _sig(u_s[sl, :] + bx_ref[...])
            log_a = coef * r
            a = jnp.exp(log_a)
            u = jnp.sqrt(jnp.maximum(jnp.tanh(-log_a) * (1.0 + a * a), 0.0)) * (ig * xc[t * nb:(t + 1) * nb])
            h = a * h + u
            ga = ga_s[sl, :]
            m_s[sl, :D_A] = (h * (ga * _sig(ga))).astype(BF16)
        hn_ref[...] = h

        logits = lbl_ref[...]
        pe = jnp.exp(logits - jnp.max(logits, axis=0, keepdims=True))
        probs = pe / jnp.sum(pe, axis=0, keepdims=True)
        lb = jnp.zeros((1, D_B), F32)
        for k in range(1, DEPTH):
            lb = lb + jnp.where(layer >= k, probs[k:k + 1], 0.0)
        c0 = 0.5 * (1.0 + lb)
        c1 = 0.5 * (1.0 - lb)
        bl = jnp.zeros((nb, D_B), F32)
        for t in range(tl):
            sl = pl.ds(t * nb, nb)
            p = c1 * jnp.tanh(0.5 * kk_s[sl, :])
            bl = bl + jnp.log(jnp.maximum(c0 + p, F_MIN))
            b_s[sl, :] = bl
            kk_s[sl, :] = c1 - p

        for t in range(tl):
            st_ = pl.ds(t * nb, nb)
            acc = [jnp.zeros((nb, DV_B), F32) for _ in range(nh)]
            for s in range(t + 1):
                ss_ = pl.ds(s * nb, nb)
                w = q_s[st_, :] * kk_s[ss_, :]
                if s < t:
                    w = w * jnp.exp(b_s[st_, :] - b_s[ss_, :])
                for hd in range(nh):
                    acc[hd] = acc[hd] + jnp.sum(w[:, heads[hd]], axis=-1, keepdims=True) * iv[s * nb:(s + 1) * nb, heads[hd]]
            for hd in range(nh):
                oi_s[st_, heads[hd]] = acc[hd]

        zrow = jnp.zeros((nb, LANES), F32)
        ebl = jnp.exp(bl)
        e_hi = ebl.astype(BF16).astype(F32)
        e_mid = (ebl - e_hi).astype(BF16).astype(F32)
        e_lo = ebl - e_hi - e_mid
        for hd in range(nh):
            for t in range(TP):
                dst = pl.ds(t, nb, stride=TP)
                if t < tl:
                    st_ = pl.ds(t * nb, nb)
                    bt_ = b_s[st_, heads[hd]]
                    qe8_s[hd, dst, :] = q_s[st_, heads[hd]] * jnp.exp(bt_)
                    kdx8_s[hd, dst, :] = kk_s[st_, heads[hd]] * jnp.exp(bl[:, heads[hd]] - bt_)
                    v8_s[hd, dst, :] = iv[t * nb:(t + 1) * nb, heads[hd]]
                else:
                    qe8_s[hd, dst, :] = zrow
                    v8_s[hd, dst, :] = zrow
                    dec = (e_hi, e_mid, e_lo)
                    kdx8_s[hd, dst, :] = dec[t - tl][:, heads[hd]] if t - tl < len(dec) else zrow

    rid = lax.broadcasted_iota(jnp.int32, (TP, LANES), 0)
    ones_rows = jnp.where((rid >= tl) & (rid < tl + 3), 1.0, 0.0)
    for bl_ in range(bt):
        r8 = pl.ds(pl.multiple_of((blk * bt + bl_) * TP, TP), TP)
        for hd in range(nh):
            s_old = s0_ref[bl_, hd]
            oc8_s[hd, r8, :] = _dot(qe8_s[hd, r8, :].astype(BF16), s_old.astype(BF16))
            rhs = jnp.concatenate([v8_s[hd, r8, :], ones_rows], axis=1).astype(BF16)
            g = _dot_tn(kdx8_s[hd, r8, :].astype(BF16), rhs)
            sn_ref[bl_, hd] = s_old * g[:, DV_B:] + g[:, :DV_B]

    @pl.when(blk == n_blk - 1)
    def _():
        for t in range(tl):
            st_ = pl.ds(t * nb, nb)
            for hd in range(nh):
                oh = oi_s[st_, heads[hd]] + oc8_s[hd, pl.ds(t, nb, stride=TP), :]
                gb = gb_s[st_, heads[hd]]
                yb = _rms(oh, onw_ref[:, heads[hd]]) * (gb * _sig(gb))
                m_s[st_, D_A + hd * DV_B:D_A + (hd + 1) * DV_B] = yb.astype(BF16)
        mo = _dot(m_s[...], wout_ref[...])
        gp = postw_ref[...] * mod_ref[2]
        for t in range(tl):
            mt = mo[t * nb:(t + 1) * nb]
            inv = lax.rsqrt(jnp.mean(mt * mt, axis=-1, keepdims=True) + EPS)
            xt_s[t] = xt_s[t] + mt * inv * gp

    @pl.when(i == pl.num_programs(0) - 1)
    def _():
        y_ref[...] = xt_s[...]


def _decode(x_tm, mod, h0, conv0_tm, s_all, w, *, bt):
    tl, nb, _ = x_tm.shape
    assert nb % bt == 0 and nb % SUBLANES == 0 and CONV_W - 1 <= tl and tl + 3 <= TP
    rows = tl * nb
    n_blk = nb // bt

    def whole(a):
        nd = a.ndim
        return pl.BlockSpec(a.shape, lambda i: (0,) * nd, pipeline_mode=pl.Buffered(1))

    def per_layer(a):
        nd = a.ndim - 1
        return pl.BlockSpec((None,) + a.shape[1:], lambda i: (i // n_blk,) + (0,) * nd, pipeline_mode=pl.Buffered(1))
    state_spec = pl.BlockSpec((None, bt, N_HEADS_B, DK_B, DV_B), lambda i: (i // n_blk, i % n_blk, 0, 0, 0))
    args = [x_tm, mod, h0, conv0_tm, s_all, *w]
    in_specs = [whole(x_tm), per_layer(mod), per_layer(h0), per_layer(conv0_tm), state_spec]
    in_specs += [whole(a) if a.shape[0] == DEPTH and a.ndim == 2 else per_layer(a) for a in w]
    out_specs = [
        pl.BlockSpec((tl, nb, D_MODEL), lambda i: (0, 0, 0)),
        pl.BlockSpec((None, nb, D_A), lambda i: (i // n_blk, 0, 0)),
        pl.BlockSpec((None, CONV_W - 1, nb, D_A), lambda i: (i // n_blk, 0, 0, 0)),
        state_spec,
    ]
    out_shape = [
        jax.ShapeDtypeStruct((tl, nb, D_MODEL), F32),
        jax.ShapeDtypeStruct((DEPTH, nb, D_A), F32),
        jax.ShapeDtypeStruct((DEPTH, CONV_W - 1, nb, D_A), F32),
        jax.ShapeDtypeStruct(s_all.shape, F32),
    ]
    wide = lambda: pltpu.VMEM((rows, D_A), F32)
    slab8 = lambda: pltpu.VMEM((N_HEADS_B, nb * TP, LANES), F32)
    scratch = [
        pltpu.VMEM((tl, nb, D_MODEL), F32),
        pltpu.VMEM((rows, D_MODEL), BF16),
        pltpu.VMEM((rows + (CONV_W - 1) * nb, D_A), F32),
        wide(), wide(), wide(),
        wide(), wide(), wide(), wide(),
        wide(),
        slab8(), slab8(), slab8(), slab8(),
    ]
    return pl.pallas_call(
        functools.partial(_decode_kernel, nb=nb, bt=bt, tl=tl),
        grid=(DEPTH * n_blk,),
        in_specs=in_specs,
        out_specs=out_specs,
        out_shape=out_shape,
        scratch_shapes=scratch,
        compiler_params=pltpu.CompilerParams(dimension_semantics=("arbitrary",), vmem_limit_bytes=VMEM_LIMIT),
        name="decode_layers",
    )(*args)


CT = SUBLANES


def _prompt_kernel(x_ref, mod_ref, prew_ref, postw_ref, win_ref, convw_ref, convb_ref, wg_ref, ba_ref, bx_ref,
                   lam_ref, lbl_ref, onw_ref, wout_ref,
                   y_ref, hn_ref, convn_ref, sn_ref,
                   xt_s, hin_s, xpad_s, ccar_s, ga_s, a_s, u_s, xc_s, q_s, kk_s, b_s, gb_s, bl_s, hcar_s, st_s,
                   xl_s, qe_s, kd_s, v_s,
                   *, bt, tl, n_levels):
    o_s = b_s
    m_s = hin_s
    j = pl.program_id(0)
    nj = pl.num_programs(0)
    rows = tl * bt
    ch = CT * bt
    n_ch = tl // CT
    nh = N_HEADS_B
    nt = D_MODEL // LANES
    tail = (CONV_W - 1) * bt

    @pl.when(j == 0)
    def _():
        hcar_s[...] = jnp.zeros_like(hcar_s)
        ccar_s[...] = jnp.zeros_like(ccar_s)
        st_s[...] = jnp.zeros_like(st_s)

    for c in range(n_ch):
        for jl in range(nt):
            for b in range(bt):
                xt_s[jl, pl.ds(c * ch + b, CT, stride=bt), :] = x_ref[b, c * CT:(c + 1) * CT, jl * LANES:(jl + 1) * LANES]

    ti = lax.broadcasted_iota(jnp.int32, (tl, tl), 0)
    si = lax.broadcasted_iota(jnp.int32, (tl, tl), 1)
    masks = []
    for lv in range(n_levels):
        h = 1 << lv
        masks.append(((ti // (2 * h)) == (si // (2 * h))) & ((ti // h) % 2 == 1) & ((si // h) % 2 == 0))

    def bcast(v):
        return jnp.concatenate([v] * CT, axis=0)

    for l in range(DEPTH):
        shift, scale, gate = mod_ref[l, 0], mod_ref[l, 1], mod_ref[l, 2]
        pm = prew_ref[l] * (1.0 + scale)
        gp = postw_ref[l] * gate

        for c in range(n_ch):
            rs = pl.ds(c * ch, ch)
            ss = jnp.sum(sum(jnp.square(xt_s[jl, rs, :]) for jl in range(nt)), axis=-1, keepdims=True)
            inv = lax.rsqrt(ss * (1.0 / D_MODEL) + EPS)
            for jl in range(nt):
                ls = slice(jl * LANES, (jl + 1) * LANES)
                hin_s[rs, ls] = (xt_s[jl, rs, :] * inv * bcast(pm[:, ls]) + bcast(shift[:, ls])).astype(BF16)

        def proj(c0, width):
            return _dot(hin_s[...], win_ref[l, :, c0:c0 + width])
        for dst, c0_ in ((q_s, 2 * D_A), (kk_s, 2 * D_A + D_B), (v_s, 2 * D_A + 2 * D_B), (gb_s, 2 * D_A + 3 * D_B)):
            z = proj(c0_, D_B)
            for hd in range(nh):
                dst[hd] = z[:, hd * DK_B:(hd + 1) * DK_B]

        logits = lbl_ref[...]
        pe = jnp.exp(logits - jnp.max(logits, axis=0, keepdims=True))
        probs = pe / jnp.sum(pe, axis=0, keepdims=True)
        lb = jnp.sum(probs[1:l + 1], axis=0, keepdims=True) if l > 0 else jnp.zeros((1, D_B), F32)
        c0 = 0.5 * (1.0 + lb)
        c1 = 0.5 * (1.0 - lb)
        bl = [jnp.zeros((bt, DK_B), F32) for _ in range(nh)]
        for t in range(tl):
            sl = pl.ds(t * bt, bt)
            for hd in range(nh):
                ls = slice(hd * DK_B, (hd + 1) * DK_B)
                p = c1[:, ls] * jnp.tanh(0.5 * kk_s[hd, sl, :])
                bl[hd] = bl[hd] + jnp.log(jnp.maximum(c0[:, ls] + p, F_MIN))
                b_s[hd, sl, :] = bl[hd]
                kk_s[hd, sl, :] = c1[:, ls] - p
        for hd in range(nh):
            bl_s[hd] = jnp.exp(bl[hd])

        for c in range(n_ch):
            rs = pl.ds(c * ch, ch)
            for hd in range(nh):
                bc, qc, kc = b_s[hd, rs, :], q_s[hd, rs, :], kk_s[hd, rs, :]
                qe_s[hd, rs, :] = qc * jnp.exp(bc)
                kd_s[hd, rs, :] = kc * jnp.exp(bcast(bl[hd]) - bc)
                for lv in range(n_levels):
                    h = 1 << lv
                    if 2 * h <= CT:
                        parts = []
                        for t in range(CT):
                            mid = (t // (2 * h)) * 2 * h + h - 1
                            ts, ms = slice(t * bt, (t + 1) * bt), slice(mid * bt, (mid + 1) * bt)
                            if t % (2 * h) >= h:
                                parts.append(qc[ts] * jnp.exp(bc[ts] - bc[ms]))
                            else:
                                parts.append(kc[ts] * jnp.exp(bc[ms] - bc[ts]))
                        xl = jnp.concatenate(parts, axis=0)
                    else:
                        t0 = c * CT
                        mid = (t0 // (2 * h)) * 2 * h + h - 1
                        bmid = bcast(b_s[hd, pl.ds(mid * bt, bt), :])
                        xl = qc * jnp.exp(bc - bmid) if t0 % (2 * h) >= h else kc * jnp.exp(bmid - bc)
                    xl_s[lv, hd, rs, :] = xl

        xpad_s[pl.ds(0, tail), :] = ccar_s[l]
        xpad_s[pl.ds(tail, rows), :] = proj(0, D_A)
        ga_s[...] = proj(D_A, D_A)
        cw = convw_ref[l]
        for c in range(n_ch):
            xc = convb_ref[l] + sum(xpad_s[pl.ds(c * ch + k * bt, ch), :] * cw[k:k + 1, :] for k in range(CONV_W))
            xc_s[pl.ds(c * ch, ch), :] = xc
        ctail = xpad_s[pl.ds(rows, tail), :]
        ccar_s[l] = ctail
        for c in range(D_A // LANES):
            g2 = _dot(xc_s[:, c * LANES:(c + 1) * LANES].astype(BF16), wg_ref[l, c])
            a_s[:, c * LANES:(c + 1) * LANES] = g2[:, :LANES]
            u_s[:, c * LANES:(c + 1) * LANES] = g2[:, LANES:]
        coef = -C_RGLRU * jax.nn.softplus(-lam_ref[l])
        ba, bx = ba_ref[l], bx_ref[l]

        def scan_chunk(c, h):
            for tp in range(CT // 2):
                yas = []
                for t2 in range(2):
                    sl = pl.ds(c * ch + (2 * tp + t2) * bt, bt)
                    r = _sig(a_s[sl, :] + ba)
                    i = _sig(u_s[sl, :] + bx)
                    log_a = coef * r
                    a = jnp.exp(log_a)
                    u = jnp.sqrt(jnp.maximum(jnp.tanh(-log_a) * (1.0 + a * a), 0.0)) * (i * xc_s[sl, :])
                    h = a * h + u
                    ga = ga_s[sl, :]
                    yas.append(h * (ga * _sig(ga)))
                m_s[pl.ds(c * ch + 2 * tp * bt, 2 * bt), :D_A] = jnp.concatenate(yas, axis=0).astype(BF16)
            return h

        def chains(b):
            sl = pl.ds(b, tl, stride=bt)
            res = []
            for hd in range(nh):
                amat = jnp.zeros((tl, tl), F32)
                for lv in range(n_levels):
                    xv = xl_s[lv, hd, sl, :].astype(BF16)
                    amat = amat + jnp.where(masks[lv], _dot_nt(xv, xv), 0.0)
                vf = v_s[hd, sl, :]
                v = vf.astype(BF16)
                st = st_s[l, b, hd]
                o = _dot(amat.astype(BF16), v) + _dot_nt(qe_s[hd, sl, :].astype(BF16), st.astype(BF16))
                st_new = st * bl_s[hd, b:b + 1, :] + _dot_tn(v, kd_s[hd, sl, :].astype(BF16))
                dsum = jnp.sum(q_s[hd, sl, :] * kk_s[hd, sl, :], axis=-1, keepdims=True)
                gb = gb_s[hd, sl, :]
                o = _rms(o + dsum * vf, onw_ref[l][:, hd * DV_B:(hd + 1) * DV_B]) * (gb * _sig(gb))
                res.append((o, st_new))
            for hd in range(nh):
                o_s[hd, sl, :] = res[hd][0]
                st_s[l, b, hd] = res[hd][1]

        h_last = hcar_s[l]
        for idx in range(max(bt, n_ch)):
            if idx < bt:
                chains(idx)
            if idx < n_ch:
                h_last = scan_chunk(idx, h_last)
        hcar_s[l] = h_last

        for hd in range(nh):
            m_s[:, D_A + hd * DV_B:D_A + (hd + 1) * DV_B] = o_s[hd].astype(BF16)

        mo = _dot(m_s[...], wout_ref[l])
        for jl in range(nt):
            xl_s[jl // nh, jl % nh] = mo[:, jl * LANES:(jl + 1) * LANES]
        for c in range(n_ch):
            rs = pl.ds(c * ch, ch)
            ss = jnp.sum(sum(jnp.square(xl_s[jl // nh, jl % nh, rs, :]) for jl in range(nt)), axis=-1, keepdims=True)
            inv = lax.rsqrt(ss * (1.0 / D_MODEL) + EPS)
            for jl in range(nt):
                ls = slice(jl * LANES, (jl + 1) * LANES)
                xt_s[jl, rs, :] = xt_s[jl, rs, :] + xl_s[jl // nh, jl % nh, rs, :] * inv * bcast(gp[:, ls])

        @pl.when(j == nj - 1)
        def _():
            hn_ref[l] = h_last
            convn_ref[l] = ctail.reshape(CONV_W - 1, bt, D_A)
            for b in range(bt):
                for hd in range(nh):
                    sn_ref[l, b, hd] = st_s[l, b, hd].T

    for c in range(n_ch):
        for jl in range(nt):
            for b in range(bt):
                y_ref[b, c * CT:(c + 1) * CT, jl * LANES:(jl + 1) * LANES] = xt_s[jl, pl.ds(c * ch + b, CT, stride=bt), :]


def _prompt(x, mod, w, *, tl):
    B, L, _ = x.shape
    bt = B
    assert bt == SUBLANES and L % tl == 0 and tl % CT == 0
    n_levels = tl.bit_length() - 1
    assert (1 << n_levels) == tl and n_levels * N_HEADS_B >= D_MODEL // LANES
    rows = tl * bt
    tail = (CONV_W - 1) * bt

    def const(shape):
        nd = len(shape)
        return pl.BlockSpec(shape, lambda j: (0,) * nd, pipeline_mode=pl.Buffered(1))
    in_specs = [pl.BlockSpec((bt, tl, D_MODEL), lambda j: (0, j, 0)), const(mod.shape)] + [const(a.shape) for a in w]
    out_specs = [
        pl.BlockSpec((bt, tl, D_MODEL), lambda j: (0, j, 0)),
        pl.BlockSpec((DEPTH, bt, D_A), lambda j: (0, 0, 0)),
        pl.BlockSpec((DEPTH, CONV_W - 1, bt, D_A), lambda j: (0, 0, 0, 0)),
        pl.BlockSpec((DEPTH, bt, N_HEADS_B, DK_B, DV_B), lambda j: (0, 0, 0, 0, 0)),
    ]
    out_shape = [
        jax.ShapeDtypeStruct((B, L, D_MODEL), F32),
        jax.ShapeDtypeStruct((DEPTH, B, D_A), F32),
        jax.ShapeDtypeStruct((DEPTH, CONV_W - 1, B, D_A), F32),
        jax.ShapeDtypeStruct((DEPTH, B, N_HEADS_B, DK_B, DV_B), F32),
    ]
    slab = lambda n: pltpu.VMEM((n, rows, LANES), F32)
    wide = lambda: pltpu.VMEM((rows, D_A), F32)
    scratch = [
        slab(D_MODEL // LANES),
        pltpu.VMEM((rows, D_MODEL), BF16),
        pltpu.VMEM((rows + tail, D_A), F32),
        pltpu.VMEM((DEPTH, tail, D_A), F32),
        wide(), wide(), wide(), wide(),
        slab(N_HEADS_B), slab(N_HEADS_B), slab(N_HEADS_B), slab(N_HEADS_B),
        pltpu.VMEM((N_HEADS_B, bt, DK_B), F32),
        pltpu.VMEM((DEPTH, bt, D_A), F32),
        pltpu.VMEM((DEPTH, bt, N_HEADS_B, DV_B, DK_B), F32),
        pltpu.VMEM((n_levels, N_HEADS_B, rows, LANES), F32),
        slab(N_HEADS_B), slab(N_HEADS_B), slab(N_HEADS_B),
    ]
    body = functools.partial(_prompt_kernel, bt=bt, tl=tl, n_levels=n_levels)
    return pl.pallas_call(
        body,
        grid=(L // tl,),
        in_specs=in_specs,
        out_specs=out_specs,
        out_shape=out_shape,
        scratch_shapes=scratch,
        compiler_params=pltpu.CompilerParams(dimension_semantics=("arbitrary",), vmem_limit_bytes=VMEM_LIMIT),
        name="prompt_layers",
    )(x, mod, *w)


def _gate_weights(wa, wx):
    per = LANES // BLK_A

    def bd(w):
        w = w.reshape(D_A // LANES, per, BLK_A, BLK_A)
        eye = jnp.eye(per, dtype=w.dtype)
        return jnp.einsum('jpcd,pq->jpcqd', w, eye).reshape(D_A // LANES, LANES, LANES)
    return jnp.concatenate([bd(wa), bd(wx)], axis=-1).astype(BF16)


def kernel(x_prompt, x_sample, state_rglru_h, state_rglru_conv, state_hgrn_S, c_prompt, c_sample, ada_w, ada_b,
           pre_norm_w, post_norm_w, w_in, conv_w, conv_b, rg_wa, rg_ba, rg_wx, rg_bx, rg_lambda, hg_lb_logits,
           hg_onorm_w, w_out):
    bp, bs = x_prompt.shape[0], x_sample.shape[0]
    mod = _modulation(jnp.concatenate([c_prompt, c_sample], axis=0), ada_w,
                      ada_b.reshape(DEPTH, 1, 3 * D_MODEL))
    w_in_bf, w_out_bf = w_in.astype(BF16), w_out.astype(BF16)
    wg = jnp.stack([_gate_weights(rg_wa[l], rg_wx[l]) for l in range(DEPTH)])
    row = lambda a: a[:, None, :]
    wp = (row(pre_norm_w), row(post_norm_w), w_in_bf, conv_w, row(conv_b), wg, row(rg_ba), row(rg_bx),
          row(rg_lambda), hg_lb_logits, row(hg_onorm_w), w_out_bf)
    yp, hp, cvp, sp = _prompt(x_prompt, mod[:, :, :bp], wp, tl=64)

    ys, hs, cvs, ss = _decode(jnp.transpose(x_sample, (1, 0, 2)), mod[:, :, bp:], state_rglru_h,
                              jnp.transpose(state_rglru_conv, (0, 2, 1, 3)), state_hgrn_S, wp, bt=16)
    return (yp, jnp.transpose(ys, (1, 0, 2)), hp, jnp.transpose(cvp, (0, 2, 1, 3)), sp,
            hs, jnp.transpose(cvs, (0, 2, 1, 3)), ss)
```

```python
import functools

import jax
import jax.numpy as jnp
from jax import lax
from jax.experimental import pallas as pl
from jax.experimental.pallas import tpu as pltpu

D_MODEL = 1024
DEPTH = 2
D_A = 512
N_BLK_A = 8
BLK_A = 64
CONV_W = 4
C_RGLRU = 8.0
D_B = 512
N_HEADS_B = 4
DK_B = 128
DV_B = 128
P_IN = 2 * D_A + 4 * D_B
EPS = 1e-6
F_MIN = 1e-30

LANES = 128
SUBLANES = 8
VMEM_LIMIT = 60 * 1024 * 1024

BF16 = jnp.bfloat16
F32 = jnp.float32


def _dot(a, b):
    return jnp.dot(a, b, preferred_element_type=F32)


def _dot_nt(a, b):
    return lax.dot_general(a, b, (((1,), (1,)), ((), ())), preferred_element_type=F32)


def _dot_tn(a, b):
    return lax.dot_general(a, b, (((0,), (0,)), ((), ())), preferred_element_type=F32)


def _rms(x, w):
    return x * lax.rsqrt(jnp.mean(x * x, axis=-1, keepdims=True) + EPS) * w


def _silu(x):
    return x * jax.nn.sigmoid(x)


def _mod_kernel(c_ref, w_ref, b_ref, o_ref):
    o_ref[...] = _dot(_silu(c_ref[...]).astype(BF16), w_ref[...].astype(BF16)) + b_ref[...]


def _modulation(c_all, ada_w, ada_b):
    nb = c_all.shape[0]
    return pl.pallas_call(
        _mod_kernel,
        grid=(DEPTH, 3),
        in_specs=[
            pl.BlockSpec((nb, D_MODEL), lambda l, k: (0, 0)),
            pl.BlockSpec((None, D_MODEL, D_MODEL), lambda l, k: (l, 0, k)),
            pl.BlockSpec((None, 1, D_MODEL), lambda l, k: (l, 0, k)),
        ],
        out_specs=pl.BlockSpec((None, None, nb, D_MODEL), lambda l, k: (l, k, 0, 0)),
        out_shape=jax.ShapeDtypeStruct((DEPTH, 3, nb, D_MODEL), F32),
        name="adaln_mod",
    )(c_all, ada_w, ada_b)


TP = SUBLANES


def _sig(x):
    return 0.5 * jnp.tanh(0.5 * x) + 0.5


def _decode_kernel(x_ref, mod_ref, h0_ref, conv0_ref, s0_ref, prew_ref, postw_ref, win_ref, convw_ref,
                   convb_ref, wg_ref, ba_ref, bx_ref, lam_ref, lbl_ref, onw_ref, wout_ref,
                   y_ref, hn_ref, convn_ref, sn_ref,
                   xt_s, hin_s, xpad_s, ga_s, a_s, u_s, q_s, kk_s, b_s, gb_s, oi_s, qe8_s, kdx8_s, v8_s, oc8_s,
                   *, nb, bt, tl):
    m_s = hin_s
    i = pl.program_id(0)
    n_blk = nb // bt
    layer = i // n_blk
    blk = i % n_blk
    rows = tl * nb
    tail = (CONV_W - 1) * nb
    nh = N_HEADS_B
    heads = [slice(hd * DK_B, (hd + 1) * DK_B) for hd in range(nh)]

    @pl.when(i == 0)
    def _():
        xt_s[...] = x_ref[...]

    @pl.when(blk == 0)
    def _():
        shift, scale = mod_ref[0], mod_ref[1]
        pm = prew_ref[...] * (1.0 + scale)
        for t in range(tl):
            xt = xt_s[t]
            inv = lax.rsqrt(jnp.mean(xt * xt, axis=-1, keepdims=True) + EPS)
            hin_s[pl.ds(t * nb, nb), :] = (xt * inv * pm + shift).astype(BF16)

        def proj(c0, width):
            return _dot(hin_s[...], win_ref[:, c0:c0 + width])
        xpad_s[pl.ds(0, tail), :] = conv0_ref[...].reshape(tail, D_A)
        xpad_s[pl.ds(tail, rows), :] = proj(0, D_A)
        ga_s[...] = proj(D_A, D_A)
        q_s[...] = proj(2 * D_A, D_B)
        kk_s[...] = proj(2 * D_A + D_B, D_B)
        iv = proj(2 * D_A + 2 * D_B, D_B)
        gb_s[...] = proj(2 * D_A + 3 * D_B, D_B)

        xc = convb_ref[...] + sum(xpad_s[pl.ds(k * nb, rows), :] * convw_ref[k:k + 1, :] for k in range(CONV_W))
        convn_ref[...] = xpad_s[pl.ds(rows, tail), :].reshape(CONV_W - 1, nb, D_A)
        xcb = xc.astype(BF16)
        for c in range(D_A // LANES):
            g2 = _dot(xcb[:, c * LANES:(c + 1) * LANES], wg_ref[c])
            a_s[:, c * LANES:(c + 1) * LANES] = g2[:, :LANES]
            u_s[:, c * LANES:(c + 1) * LANES] = g2[:, LANES:]
        coef = -C_RGLRU * jax.nn.softplus(-lam_ref[...])
        h = h0_ref[...]
        for t in range(tl):
            sl = pl.ds(t * nb, nb)
            r = _sig(a_s[sl, :] + ba_ref[...])
            ig = _sig(u_s[sl, :] + bx_ref[...])
            log_a = coef * r
            a = jnp.exp(log_a)
            u = jnp.sqrt(jnp.maximum(jnp.tanh(-log_a) * (1.0 + a * a), 0.0)) * (ig * xc[t * nb:(t + 1) * nb])
            h = a * h + u
            ga = ga_s[sl, :]
            m_s[sl, :D_A] = (h * (ga * _sig(ga))).astype(BF16)
        hn_ref[...] = h

        logits = lbl_ref[...]
        pe = jnp.exp(logits - jnp.max(logits, axis=0, keepdims=True))
        probs = pe / jnp.sum(pe, axis=0, keepdims=True)
        lb = jnp.zeros((1, D_B), F32)
        for k in range(1, DEPTH):
            lb = lb + jnp.where(layer >= k, probs[k:k + 1], 0.0)
        c0 = 0.5 * (1.0 + lb)
        c1 = 0.5 * (1.0 - lb)
        bl = jnp.zeros((nb, D_B), F32)
        for t in range(tl):
            sl = pl.ds(t * nb, nb)
            p = c1 * jnp.tanh(0.5 * kk_s[sl, :])
            bl = bl + jnp.log(jnp.maximum(c0 + p, F_MIN))
            b_s[sl, :] = bl
            kk_s[sl, :] = c1 - p

        for t in range(tl):
            st_ = pl.ds(t * nb, nb)
            acc = [jnp.zeros((nb, DV_B), F32) for _ in range(nh)]
            for s in range(t + 1):
                ss_ = pl.ds(s * nb, nb)
                w = q_s[st_, :] * kk_s[ss_, :]
                if s < t:
                    w = w * jnp.exp(b_s[st_, :] - b_s[ss_, :])
                for hd in range(nh):
                    acc[hd] = acc[hd] + jnp.sum(w[:, heads[hd]], axis=-1, keepdims=True) * iv[s * nb:(s + 1) * nb, heads[hd]]
            for hd in range(nh):
                oi_s[st_, heads[hd]] = acc[hd]

        zrow = jnp.zeros((nb, LANES), F32)
        ebl = jnp.exp(bl)
        e_hi = ebl.astype(BF16).astype(F32)
        e_mid = (ebl - e_hi).astype(BF16).astype(F32)
        e_lo = ebl - e_hi - e_mid
        for hd in range(nh):
            for t in range(TP):
                dst = pl.ds(t, nb, stride=TP)
                if t < tl:
                    st_ = pl.ds(t * nb, nb)
                    bt_ = b_s[st_, heads[hd]]
                    qe8_s[hd, dst, :] = q_s[st_, heads[hd]] * jnp.exp(bt_)
                    kdx8_s[hd, dst, :] = kk_s[st_, heads[hd]] * jnp.exp(bl[:, heads[hd]] - bt_)
                    v8_s[hd, dst, :] = iv[t * nb:(t + 1) * nb, heads[hd]]
                else:
                    qe8_s[hd, dst, :] = zrow
                    v8_s[hd, dst, :] = zrow
                    dec = (e_hi, e_mid, e_lo)
                    kdx8_s[hd, dst, :] = dec[t - tl][:, heads[hd]] if t - tl < len(dec) else zrow

    rid = lax.broadcasted_iota(jnp.int32, (TP, LANES), 0)
    ones_rows = jnp.where((rid >= tl) & (rid < tl + 3), 1.0, 0.0)
    for bl_ in range(bt):
        r8 = pl.ds(pl.multiple_of((blk * bt + bl_) * TP, TP), TP)
        for hd in range(nh):
            s_old = s0_ref[bl_, hd]
            oc8_s[hd, r8, :] = _dot(qe8_s[hd, r8, :].astype(BF16), s_old.astype(BF16))
            rhs = jnp.concatenate([v8_s[hd, r8, :], ones_rows], axis=1).astype(BF16)
            g = _dot_tn(kdx8_s[hd, r8, :].astype(BF16), rhs)
            sn_ref[bl_, hd] = s_old * g[:, DV_B:] + g[:, :DV_B]

    @pl.when(blk == n_blk - 1)
    def _():
        for t in range(tl):
            st_ = pl.ds(t * nb, nb)
            for hd in range(nh):
                oh = oi_s[st_, heads[hd]] + oc8_s[hd, pl.ds(t, nb, stride=TP), :]
                gb = gb_s[st_, heads[hd]]
                yb = _rms(oh, onw_ref[:, heads[hd]]) * (gb * _sig(gb))
                m_s[st_, D_A + hd * DV_B:D_A + (hd + 1) * DV_B] = yb.astype(BF16)
        mo = _dot(m_s[...], wout_ref[...])
        gp = postw_ref[...] * mod_ref[2]
        for t in range(tl):
            mt = mo[t * nb:(t + 1) * nb]
            inv = lax.rsqrt(jnp.mean(mt * mt, axis=-1, keepdims=True) + EPS)
            xt_s[t] = xt_s[t] + mt * inv * gp

    @pl.when(i == pl.num_programs(0) - 1)
    def _():
        y_ref[...] = xt_s[...]


def _decode(x_tm, mod, h0, conv0_tm, s_all, w, *, bt):
    tl, nb, _ = x_tm.shape
    assert nb % bt == 0 and nb % SUBLANES == 0 and CONV_W - 1 <= tl and tl + 3 <= TP
    rows = tl * nb
    n_blk = nb // bt

    def whole(a):
        nd = a.ndim
        return pl.BlockSpec(a.shape, lambda i: (0,) * nd, pipeline_mode=pl.Buffered(1))

    def per_layer(a):
        nd = a.ndim - 1
        return pl.BlockSpec((None,) + a.shape[1:], lambda i: (i // n_blk,) + (0,) * nd, pipeline_mode=pl.Buffered(1))
    state_spec = pl.BlockSpec((None, bt, N_HEADS_B, DK_B, DV_B), lambda i: (i // n_blk, i % n_blk, 0, 0, 0))
    args = [x_tm, mod, h0, conv0_tm, s_all, *w]
    in_specs = [whole(x_tm), per_layer(mod), per_layer(h0), per_layer(conv0_tm), state_spec]
    in_specs += [whole(a) if a.shape[0] == DEPTH and a.ndim == 2 else per_layer(a) for a in w]
    out_specs = [
        pl.BlockSpec((tl, nb, D_MODEL), lambda i: (0, 0, 0)),
        pl.BlockSpec((None, nb, D_A), lambda i: (i // n_blk, 0, 0)),
        pl.BlockSpec((None, CONV_W - 1, nb, D_A), lambda i: (i // n_blk, 0, 0, 0)),
        state_spec,
    ]
    out_shape = [
        jax.ShapeDtypeStruct((tl, nb, D_MODEL), F32),
        jax.ShapeDtypeStruct((DEPTH, nb, D_A), F32),
        jax.ShapeDtypeStruct((DEPTH, CONV_W - 1, nb, D_A), F32),
        jax.ShapeDtypeStruct(s_all.shape, F32),
    ]
    wide = lambda: pltpu.VMEM((rows, D_A), F32)
    slab8 = lambda: pltpu.VMEM((N_HEADS_B, nb * TP, LANES), F32)
    scratch = [
        pltpu.VMEM((tl, nb, D_MODEL), F32),
        pltpu.VMEM((rows, D_MODEL), BF16),
        pltpu.VMEM((rows + (CONV_W - 1) * nb, D_A), F32),
        wide(), wide(), wide(),
        wide(), wide(), wide(), wide(),
        wide(),
        slab8(), slab8(), slab8(), slab8(),
    ]
    return pl.pallas_call(
        functools.partial(_decode_kernel, nb=nb, bt=bt, tl=tl),
        grid=(DEPTH * n_blk,),
        in_specs=in_specs,
        out_specs=out_specs,
        out_shape=out_shape,
        scratch_shapes=scratch,
        compiler_params=pltpu.CompilerParams(dimension_semantics=("arbitrary",), vmem_limit_bytes=VMEM_LIMIT),
        name="decode_layers",
    )(*args)


CT = SUBLANES


def _prompt_kernel(x_ref, mod_ref, prew_ref, postw_ref, win_ref, convw_ref, convb_ref, wg_ref, ba_ref, bx_ref,
                   lam_ref, lbl_ref, onw_ref, wout_ref,
                   y_ref, hn_ref, convn_ref, sn_ref,
                   xt_s, hin_s, xpad_s, ccar_s, ga_s, a_s, u_s, xc_s, q_s, kk_s, b_s, gb_s, bl_s, hcar_s, st_s,
                   xl_s, qe_s, kd_s, v_s, oi_s,
                   *, bt, tl, n_levels):
    o_s = b_s
    m_s = hin_s
    j = pl.program_id(0)
    nj = pl.num_programs(0)
    rows = tl * bt
    ch = CT * bt
    n_ch = tl // CT
    nh = N_HEADS_B
    nt = D_MODEL // LANES
    tail = (CONV_W - 1) * bt

    @pl.when(j == 0)
    def _():
        hcar_s[...] = jnp.zeros_like(hcar_s)
        ccar_s[...] = jnp.zeros_like(ccar_s)
        st_s[...] = jnp.zeros_like(st_s)

    for c in range(n_ch):
        for jl in range(nt):
            for b in range(bt):
                xt_s[jl, pl.ds(c * ch + b, CT, stride=bt), :] = x_ref[b, c * CT:(c + 1) * CT, jl * LANES:(jl + 1) * LANES]

    ti = lax.broadcasted_iota(jnp.int32, (tl, tl), 0)
    si = lax.broadcasted_iota(jnp.int32, (tl, tl), 1)
    masks = []
    for lv in range(n_levels):
        h = CT << lv
        masks.append(((ti // (2 * h)) == (si // (2 * h))) & ((ti // h) % 2 == 1) & ((si // h) % 2 == 0))

    def bcast(v):
        return jnp.concatenate([v] * CT, axis=0)

    for l in range(DEPTH):
        shift, scale, gate = mod_ref[l, 0], mod_ref[l, 1], mod_ref[l, 2]
        pm = prew_ref[l] * (1.0 + scale)
        gp = postw_ref[l] * gate

        for c in range(n_ch):
            rs = pl.ds(c * ch, ch)
            ss = jnp.sum(sum(jnp.square(xt_s[jl, rs, :]) for jl in range(nt)), axis=-1, keepdims=True)
            inv = lax.rsqrt(ss * (1.0 / D_MODEL) + EPS)
            for jl in range(nt):
                ls = slice(jl * LANES, (jl + 1) * LANES)
                hin_s[rs, ls] = (xt_s[jl, rs, :] * inv * bcast(pm[:, ls]) + bcast(shift[:, ls])).astype(BF16)

        def proj(c0, width):
            return _dot(hin_s[...], win_ref[l, :, c0:c0 + width])
        for dst, c0_ in ((q_s, 2 * D_A), (kk_s, 2 * D_A + D_B), (v_s, 2 * D_A + 2 * D_B), (gb_s, 2 * D_A + 3 * D_B)):
            z = proj(c0_, D_B)
            for hd in range(nh):
                dst[hd] = z[:, hd * DK_B:(hd + 1) * DK_B]

        logits = lbl_ref[...]
        pe = jnp.exp(logits - jnp.max(logits, axis=0, keepdims=True))
        probs = pe / jnp.sum(pe, axis=0, keepdims=True)
        lb = jnp.sum(probs[1:l + 1], axis=0, keepdims=True) if l > 0 else jnp.zeros((1, D_B), F32)
        c0 = 0.5 * (1.0 + lb)
        c1 = 0.5 * (1.0 - lb)
        bl = [jnp.zeros((bt, DK_B), F32) for _ in range(nh)]
        for t in range(tl):
            sl = pl.ds(t * bt, bt)
            for hd in range(nh):
                ls = slice(hd * DK_B, (hd + 1) * DK_B)
                p = c1[:, ls] * jnp.tanh(0.5 * kk_s[hd, sl, :])
                bl[hd] = bl[hd] + jnp.log(jnp.maximum(c0[:, ls] + p, F_MIN))
                b_s[hd, sl, :] = bl[hd]
                kk_s[hd, sl, :] = c1[:, ls] - p
        for hd in range(nh):
            bl_s[hd] = jnp.exp(bl[hd])

        for c in range(n_ch):
            rs = pl.ds(c * ch, ch)
            for hd in range(nh):
                bc, qc, kc, vc = b_s[hd, rs, :], q_s[hd, rs, :], kk_s[hd, rs, :], v_s[hd, rs, :]
                qe_s[hd, rs, :] = qc * jnp.exp(bc)
                kd_s[hd, rs, :] = kc * jnp.exp(bcast(bl[hd]) - bc)
                near = []
                for t in range(CT):
                    ts = slice(t * bt, (t + 1) * bt)
                    acc = jnp.sum(qc[ts] * kc[ts], axis=-1, keepdims=True) * vc[ts]
                    for u in range(t):
                        us = slice(u * bt, (u + 1) * bt)
                        w = qc[ts] * kc[us] * jnp.exp(bc[ts] - bc[us])
                        acc = acc + jnp.sum(w, axis=-1, keepdims=True) * vc[us]
                    near.append(acc)
                oi_s[hd, rs, :] = jnp.concatenate(near, axis=0)
                t0 = c * CT
                for lv in range(n_levels):
                    h = CT << lv
                    mid = (t0 // (2 * h)) * 2 * h + h - 1
                    bmid = bcast(b_s[hd, pl.ds(mid * bt, bt), :])
                    xl_s[lv, hd, rs, :] = qc * jnp.exp(bc - bmid) if t0 % (2 * h) >= h else kc * jnp.exp(bmid - bc)

        xpad_s[pl.ds(0, tail), :] = ccar_s[l]
        xpad_s[pl.ds(tail, rows), :] = proj(0, D_A)
        ga_s[...] = proj(D_A, D_A)
        cw = convw_ref[l]
        for c in range(n_ch):
            xc = convb_ref[l] + sum(xpad_s[pl.ds(c * ch + k * bt, ch), :] * cw[k:k + 1, :] for k in range(CONV_W))
            xc_s[pl.ds(c * ch, ch), :] = xc
        ctail = xpad_s[pl.ds(rows, tail), :]
        ccar_s[l] = ctail
        for c in range(D_A // LANES):
            g2 = _dot(xc_s[:, c * LANES:(c + 1) * LANES].astype(BF16), wg_ref[l, c])
            a_s[:, c * LANES:(c + 1) * LANES] = g2[:, :LANES]
            u_s[:, c * LANES:(c + 1) * LANES] = g2[:, LANES:]
        coef = -C_RGLRU * jax.nn.softplus(-lam_ref[l])
        ba, bx = ba_ref[l], bx_ref[l]

        def scan_chunk(c, h):
            for tp in range(CT // 2):
                yas = []
                for t2 in range(2):
                    sl = pl.ds(c * ch + (2 * tp + t2) * bt, bt)
                    r = _sig(a_s[sl, :] + ba)
                    i = _sig(u_s[sl, :] + bx)
                    log_a = coef * r
                    a = jnp.exp(log_a)
                    u = jnp.sqrt(jnp.maximum(jnp.tanh(-log_a) * (1.0 + a * a), 0.0)) * (i * xc_s[sl, :])
                    h = a * h + u
                    ga = ga_s[sl, :]
                    yas.append(h * (ga * _sig(ga)))
                m_s[pl.ds(c * ch + 2 * tp * bt, 2 * bt), :D_A] = jnp.concatenate(yas, axis=0).astype(BF16)
            return h

        def chains(b):
            sl = pl.ds(b, tl, stride=bt)
            res = []
            for hd in range(nh):
                amat = jnp.zeros((tl, tl), F32)
                for lv in range(n_levels):
                    xv = xl_s[lv, hd, sl, :].astype(BF16)
                    amat = amat + jnp.where(masks[lv], _dot_nt(xv, xv), 0.0)
                v = v_s[hd, sl, :].astype(BF16)
                st = st_s[l, b, hd]
                o = _dot(amat.astype(BF16), v) + _dot_nt(qe_s[hd, sl, :].astype(BF16), st.astype(BF16))
                st_new = st * bl_s[hd, b:b + 1, :] + _dot_tn(v, kd_s[hd, sl, :].astype(BF16))
                gb = gb_s[hd, sl, :]
                o = _rms(o + oi_s[hd, sl, :], onw_ref[l][:, hd * DV_B:(hd + 1) * DV_B]) * (gb * _sig(gb))
                res.append((o, st_new))
            for hd in range(nh):
                o_s[hd, sl, :] = res[hd][0]
                st_s[l, b, hd] = res[hd][1]

        h_last = hcar_s[l]
        for idx in range(max(bt, n_ch)):
            if idx < bt:
                chains(idx)
            if idx < n_ch:
                h_last = scan_chunk(idx, h_last)
        hcar_s[l] = h_last

        for hd in range(nh):
            m_s[:, D_A + hd * DV_B:D_A + (hd + 1) * DV_B] = o_s[hd].astype(BF16)

        mo = _dot(m_s[...], wout_ref[l])
        for jl in range(nt):
            xl_s[jl // nh, jl % nh] = mo[:, jl * LANES:(jl + 1) * LANES]
        for c in range(n_ch):
            rs = pl.ds(c * ch, ch)
            ss = jnp.sum(sum(jnp.square(xl_s[jl // nh, jl % nh, rs, :]) for jl in range(nt)), axis=-1, keepdims=True)
            inv = lax.rsqrt(ss * (1.0 / D_MODEL) + EPS)
            for jl in range(nt):
                ls = slice(jl * LANES, (jl + 1) * LANES)
                xt_s[jl, rs, :] = xt_s[jl, rs, :] + xl_s[jl // nh, jl % nh, rs, :] * inv * bcast(gp[:, ls])

        @pl.when(j == nj - 1)
        def _():
            hn_ref[l] = h_last
            convn_ref[l] = ctail.reshape(CONV_W - 1, bt, D_A)
            for b in range(bt):
                for hd in range(nh):
                    sn_ref[l, b, hd] = st_s[l, b, hd].T

    for c in range(n_ch):
        for jl in range(nt):
            for b in range(bt):
                y_ref[b, c * CT:(c + 1) * CT, jl * LANES:(jl + 1) * LANES] = xt_s[jl, pl.ds(c * ch + b, CT, stride=bt), :]


def _prompt(x, mod, w, *, tl):
    B, L, _ = x.shape
    bt = B
    assert bt == SUBLANES and L % tl == 0 and tl % CT == 0
    n_levels = (tl // CT).bit_length() - 1
    assert (CT << n_levels) == tl and n_levels * N_HEADS_B >= D_MODEL // LANES
    rows = tl * bt
    tail = (CONV_W - 1) * bt

    def const(shape):
        nd = len(shape)
        return pl.BlockSpec(shape, lambda j: (0,) * nd, pipeline_mode=pl.Buffered(1))
    in_specs = [pl.BlockSpec((bt, tl, D_MODEL), lambda j: (0, j, 0)), const(mod.shape)] + [const(a.shape) for a in w]
    out_specs = [
        pl.BlockSpec((bt, tl, D_MODEL), lambda j: (0, j, 0)),
        pl.BlockSpec((DEPTH, bt, D_A), lambda j: (0, 0, 0)),
        pl.BlockSpec((DEPTH, CONV_W - 1, bt, D_A), lambda j: (0, 0, 0, 0)),
        pl.BlockSpec((DEPTH, bt, N_HEADS_B, DK_B, DV_B), lambda j: (0, 0, 0, 0, 0)),
    ]
    out_shape = [
        jax.ShapeDtypeStruct((B, L, D_MODEL), F32),
        jax.ShapeDtypeStruct((DEPTH, B, D_A), F32),
        jax.ShapeDtypeStruct((DEPTH, CONV_W - 1, B, D_A), F32),
        jax.ShapeDtypeStruct((DEPTH, B, N_HEADS_B, DK_B, DV_B), F32),
    ]
    slab = lambda n: pltpu.VMEM((n, rows, LANES), F32)
    wide = lambda: pltpu.VMEM((rows, D_A), F32)
    scratch = [
        slab(D_MODEL // LANES),
        pltpu.VMEM((rows, D_MODEL), BF16),
        pltpu.VMEM((rows + tail, D_A), F32),
        pltpu.VMEM((DEPTH, tail, D_A), F32),
        wide(), wide(), wide(), wide(),
        slab(N_HEADS_B), slab(N_HEADS_B), slab(N_HEADS_B), slab(N_HEADS_B),
        pltpu.VMEM((N_HEADS_B, bt, DK_B), F32),
        pltpu.VMEM((DEPTH, bt, D_A), F32),
        pltpu.VMEM((DEPTH, bt, N_HEADS_B, DV_B, DK_B), F32),
        pltpu.VMEM((n_levels, N_HEADS_B, rows, LANES), F32),
        slab(N_HEADS_B), slab(N_HEADS_B), slab(N_HEADS_B), slab(N_HEADS_B),
    ]
    body = functools.partial(_prompt_kernel, bt=bt, tl=tl, n_levels=n_levels)
    return pl.pallas_call(
        body,
        grid=(L // tl,),
        in_specs=in_specs,
        out_specs=out_specs,
        out_shape=out_shape,
        scratch_shapes=scratch,
        compiler_params=pltpu.CompilerParams(dimension_semantics=("arbitrary",), vmem_limit_bytes=VMEM_LIMIT),
        name="prompt_layers",
    )(x, mod, *w)


def _gate_weights(wa, wx):
    per = LANES // BLK_A

    def bd(w):
        w = w.reshape(D_A // LANES, per, BLK_A, BLK_A)
        eye = jnp.eye(per, dtype=w.dtype)
        return jnp.einsum('jpcd,pq->jpcqd', w, eye).reshape(D_A // LANES, LANES, LANES)
    return jnp.concatenate([bd(wa), bd(wx)], axis=-1).astype(BF16)


def kernel(x_prompt, x_sample, state_rglru_h, state_rglru_conv, state_hgrn_S, c_prompt, c_sample, ada_w, ada_b,
           pre_norm_w, post_norm_w, w_in, conv_w, conv_b, rg_wa, rg_ba, rg_wx, rg_bx, rg_lambda, hg_lb_logits,
           hg_onorm_w, w_out):
    bp, bs = x_prompt.shape[0], x_sample.shape[0]
    mod = _modulation(jnp.concatenate([c_prompt, c_sample], axis=0), ada_w,
                      ada_b.reshape(DEPTH, 1, 3 * D_MODEL))
    w_in_bf, w_out_bf = w_in.astype(BF16), w_out.astype(BF16)
    wg = jnp.stack([_gate_weights(rg_wa[l], rg_wx[l]) for l in range(DEPTH)])
    row = lambda a: a[:, None, :]
    wp = (row(pre_norm_w), row(post_norm_w), w_in_bf, conv_w, row(conv_b), wg, row(rg_ba), row(rg_bx),
          row(rg_lambda), hg_lb_logits, row(hg_onorm_w), w_out_bf)
    yp, hp, cvp, sp = _prompt(x_prompt, mod[:, :, :bp], wp, tl=64)

    ys, hs, cvs, ss = _decode(jnp.transpose(x_sample, (1, 0, 2)), mod[:, :, bp:], state_rglru_h,
                              jnp.transpose(state_rglru_conv, (0, 2, 1, 3)), state_hgrn_S, wp, bt=16)
    return (yp, jnp.transpose(ys, (1, 0, 2)), hp, jnp.transpose(cvp, (0, 2, 1, 3)), sp,
            hs, jnp.transpose(cvs, (0, 2, 1, 3)), ss)
```

```python
import functools

import jax
import jax.numpy as jnp
from jax import lax
from jax.experimental import pallas as pl
from jax.experimental.pallas import tpu as pltpu

D_MODEL = 1024
DEPTH = 2
D_A = 512
N_BLK_A = 8
BLK_A = 64
CONV_W = 4
C_RGLRU = 8.0
D_B = 512
N_HEADS_B = 4
DK_B = 128
DV_B = 128
P_IN = 2 * D_A + 4 * D_B
EPS = 1e-6
F_MIN = 1e-30

LANES = 128
SUBLANES = 8
VMEM_LIMIT = 60 * 1024 * 1024

BF16 = jnp.bfloat16
F32 = jnp.float32


def _dot(a, b):
    return jnp.dot(a, b, preferred_element_type=F32)


def _dot_nt(a, b):
    return lax.dot_general(a, b, (((1,), (1,)), ((), ())), preferred_element_type=F32)


def _dot_tn(a, b):
    return lax.dot_general(a, b, (((0,), (0,)), ((), ())), preferred_element_type=F32)


def _rms(x, w):
    return x * lax.rsqrt(jnp.mean(x * x, axis=-1, keepdims=True) + EPS) * w


def _silu(x):
    return x * jax.nn.sigmoid(x)


def _mod_kernel(c_ref, w_ref, b_ref, o_ref):
    o_ref[...] = _dot(_silu(c_ref[...]).astype(BF16), w_ref[...].astype(BF16)) + b_ref[...]


def _modulation(c_all, ada_w, ada_b):
    nb = c_all.shape[0]
    return pl.pallas_call(
        _mod_kernel,
        grid=(DEPTH, 3),
        in_specs=[
            pl.BlockSpec((nb, D_MODEL), lambda l, k: (0, 0)),
            pl.BlockSpec((None, D_MODEL, D_MODEL), lambda l, k: (l, 0, k)),
            pl.BlockSpec((None, 1, D_MODEL), lambda l, k: (l, 0, k)),
        ],
        out_specs=pl.BlockSpec((None, None, nb, D_MODEL), lambda l, k: (l, k, 0, 0)),
        out_shape=jax.ShapeDtypeStruct((DEPTH, 3, nb, D_MODEL), F32),
        name="adaln_mod",
    )(c_all, ada_w, ada_b)


TP = SUBLANES


def _sig(x):
    return 0.5 * jnp.tanh(0.5 * x) + 0.5


def _decode_kernel(x_ref, mod_ref, h0_ref, conv0_ref, s0_ref, prew_ref, postw_ref, win_ref, convw_ref,
                   convb_ref, wg_ref, ba_ref, bx_ref, lam_ref, lbl_ref, onw_ref, wout_ref,
                   y_ref, hn_ref, convn_ref, sn_ref,
                   xt_s, hin_s, xpad_s, ga_s, a_s, u_s, q_s, kk_s, b_s, gb_s, oi_s, qe8_s, kdx8_s, v8_s, oc8_s,
                   *, nb, bt, tl):
    m_s = hin_s
    i = pl.program_id(0)
    n_blk = nb // bt
    layer = i // n_blk
    blk = i % n_blk
    rows = tl * nb
    tail = (CONV_W - 1) * nb
    nh = N_HEADS_B
    heads = [slice(hd * DK_B, (hd + 1) * DK_B) for hd in range(nh)]

    @pl.when(i == 0)
    def _():
        xt_s[...] = x_ref[...]

    @pl.when(blk == 0)
    def _():
        shift, scale = mod_ref[0], mod_ref[1]
        pm = prew_ref[...] * (1.0 + scale)
        for t in range(tl):
            xt = xt_s[t]
            inv = lax.rsqrt(jnp.mean(xt * xt, axis=-1, keepdims=True) + EPS)
            hin_s[pl.ds(t * nb, nb), :] = (xt * inv * pm + shift).astype(BF16)

        def proj(c0, width):
            return _dot(hin_s[...], win_ref[:, c0:c0 + width])
        xpad_s[pl.ds(0, tail), :] = conv0_ref[...].reshape(tail, D_A)
        xpad_s[pl.ds(tail, rows), :] = proj(0, D_A)
        ga_s[...] = proj(D_A, D_A)
        q_s[...] = proj(2 * D_A, D_B)
        kk_s[...] = proj(2 * D_A + D_B, D_B)
        iv = proj(2 * D_A + 2 * D_B, D_B)
        gb_s[...] = proj(2 * D_A + 3 * D_B, D_B)

        xc = convb_ref[...] + sum(xpad_s[pl.ds(k * nb, rows), :] * convw_ref[k:k + 1, :] for k in range(CONV_W))
        convn_ref[...] = xpad_s[pl.ds(rows, tail), :].reshape(CONV_W - 1, nb, D_A)
        xcb = xc.astype(BF16)
        for c in range(D_A // LANES):
            g2 = _dot(xcb[:, c * LANES:(c + 1) * LANES], wg_ref[c])
            a_s[:, c * LANES:(c + 1) * LANES] = g2[:, :LANES]
            u_s[:, c * LANES:(c + 1) * LANES] = g2[:, LANES:]
        coef = -C_RGLRU * jax.nn.softplus(-lam_ref[...])
        h = h0_ref[...]
        for t in range(tl):
            sl = pl.ds(t * nb, nb)
            r = _sig(a_s[sl, :] + ba_ref[...])
            ig = _sig(u_s[sl, :] + bx_ref[...])
            log_a = coef * r
            a = jnp.exp(log_a)
            u = jnp.sqrt(jnp.maximum(jnp.tanh(-log_a) * (1.0 + a * a), 0.0)) * (ig * xc[t * nb:(t + 1) * nb])
            h = a * h + u
            ga = ga_s[sl, :]
            m_s[sl, :D_A] = (h * (ga * _sig(ga))).astype(BF16)
        hn_ref[...] = h

        logits = lbl_ref[...]
        pe = jnp.exp(logits - jnp.max(logits, axis=0, keepdims=True))
        probs = pe / jnp.sum(pe, axis=0, keepdims=True)
        lb = jnp.zeros((1, D_B), F32)
        for k in range(1, DEPTH):
            lb = lb + jnp.where(layer >= k, probs[k:k + 1], 0.0)
        c0 = 0.5 * (1.0 + lb)
        c1 = 0.5 * (1.0 - lb)
        bl = jnp.zeros((nb, D_B), F32)
        for t in range(tl):
            sl = pl.ds(t * nb, nb)
            p = c1 * jnp.tanh(0.5 * kk_s[sl, :])
            bl = bl + jnp.log(jnp.maximum(c0 + p, F_MIN))
            b_s[sl, :] = bl
            kk_s[sl, :] = c1 - p

        for t in range(tl):
            st_ = pl.ds(t * nb, nb)
            acc = [jnp.zeros((nb, DV_B), F32) for _ in range(nh)]
            for s in range(t + 1):
                ss_ = pl.ds(s * nb, nb)
                w = q_s[st_, :] * kk_s[ss_, :]
                if s < t:
                    w = w * jnp.exp(b_s[st_, :] - b_s[ss_, :])
                for hd in range(nh):
                    acc[hd] = acc[hd] + jnp.sum(w[:, heads[hd]], axis=-1, keepdims=True) * iv[s * nb:(s + 1) * nb, heads[hd]]
            for hd in range(nh):
                oi_s[st_, heads[hd]] = acc[hd]

        zrow = jnp.zeros((nb, LANES), F32)
        ebl = jnp.exp(bl)
        e_hi = ebl.astype(BF16).astype(F32)
        e_mid = (ebl - e_hi).astype(BF16).astype(F32)
        e_lo = ebl - e_hi - e_mid
        for hd in range(nh):
            for t in range(TP):
                dst = pl.ds(t, nb, stride=TP)
                if t < tl:
                    st_ = pl.ds(t * nb, nb)
                    bt_ = b_s[st_, heads[hd]]
                    qe8_s[hd, dst, :] = q_s[st_, heads[hd]] * jnp.exp(bt_)
                    kdx8_s[hd, dst, :] = kk_s[st_, heads[hd]] * jnp.exp(bl[:, heads[hd]] - bt_)
                    v8_s[hd, dst, :] = iv[t * nb:(t + 1) * nb, heads[hd]]
                else:
                    qe8_s[hd, dst, :] = zrow
                    v8_s[hd, dst, :] = zrow
                    dec = (e_hi, e_mid, e_lo)
                    kdx8_s[hd, dst, :] = dec[t - tl][:, heads[hd]] if t - tl < len(dec) else zrow

    rid = lax.broadcasted_iota(jnp.int32, (TP, LANES), 0)
    ones_rows = jnp.where((rid >= tl) & (rid < tl + 3), 1.0, 0.0)
    for bl_ in range(bt):
        r8 = pl.ds(pl.multiple_of((blk * bt + bl_) * TP, TP), TP)
        for hd in range(nh):
            s_old = s0_ref[bl_, hd]
            oc8_s[hd, r8, :] = _dot(qe8_s[hd, r8, :].astype(BF16), s_old.astype(BF16))
            rhs = jnp.concatenate([v8_s[hd, r8, :], ones_rows], axis=1).astype(BF16)
            g = _dot_tn(kdx8_s[hd, r8, :].astype(BF16), rhs)
            sn_ref[bl_, hd] = s_old * g[:, DV_B:] + g[:, :DV_B]

    @pl.when(blk == n_blk - 1)
    def _():
        for t in range(tl):
            st_ = pl.ds(t * nb, nb)
            for hd in range(nh):
                oh = oi_s[st_, heads[hd]] + oc8_s[hd, pl.ds(t, nb, stride=TP), :]
                gb = gb_s[st_, heads[hd]]
                yb = _rms(oh, onw_ref[:, heads[hd]]) * (gb * _sig(gb))
                m_s[st_, D_A + hd * DV_B:D_A + (hd + 1) * DV_B] = yb.astype(BF16)
        mo = _dot(m_s[...], wout_ref[...])
        gp = postw_ref[...] * mod_ref[2]
        for t in range(tl):
            mt = mo[t * nb:(t + 1) * nb]
            inv = lax.rsqrt(jnp.mean(mt * mt, axis=-1, keepdims=True) + EPS)
            xt_s[t] = xt_s[t] + mt * inv * gp

    @pl.when(i == pl.num_programs(0) - 1)
    def _():
        y_ref[...] = xt_s[...]


def _decode(x_tm, mod, h0, conv0_tm, s_all, w, *, bt):
    tl, nb, _ = x_tm.shape
    assert nb % bt == 0 and nb % SUBLANES == 0 and CONV_W - 1 <= tl and tl + 3 <= TP
    rows = tl * nb
    n_blk = nb // bt

    def whole(a):
        nd = a.ndim
        return pl.BlockSpec(a.shape, lambda i: (0,) * nd, pipeline_mode=pl.Buffered(1))

    def per_layer(a):
        nd = a.ndim - 1
        return pl.BlockSpec((None,) + a.shape[1:], lambda i: (i // n_blk,) + (0,) * nd, pipeline_mode=pl.Buffered(1))
    state_spec = pl.BlockSpec((None, bt, N_HEADS_B, DK_B, DV_B), lambda i: (i // n_blk, i % n_blk, 0, 0, 0))
    args = [x_tm, mod, h0, conv0_tm, s_all, *w]
    in_specs = [whole(x_tm), per_layer(mod), per_layer(h0), per_layer(conv0_tm), state_spec]
    in_specs += [whole(a) if a.shape[0] == DEPTH and a.ndim == 2 else per_layer(a) for a in w]
    out_specs = [
        pl.BlockSpec((tl, nb, D_MODEL), lambda i: (0, 0, 0)),
        pl.BlockSpec((None, nb, D_A), lambda i: (i // n_blk, 0, 0)),
        pl.BlockSpec((None, CONV_W - 1, nb, D_A), lambda i: (i // n_blk, 0, 0, 0)),
        state_spec,
    ]
    out_shape = [
        jax.ShapeDtypeStruct((tl, nb, D_MODEL), F32),
        jax.ShapeDtypeStruct((DEPTH, nb, D_A), F32),
        jax.ShapeDtypeStruct((DEPTH, CONV_W - 1, nb, D_A), F32),
        jax.ShapeDtypeStruct(s_all.shape, F32),
    ]
    wide = lambda: pltpu.VMEM((rows, D_A), F32)
    slab8 = lambda: pltpu.VMEM((N_HEADS_B, nb * TP, LANES), F32)
    scratch = [
        pltpu.VMEM((tl, nb, D_MODEL), F32),
        pltpu.VMEM((rows, D_MODEL), BF16),
        pltpu.VMEM((rows + (CONV_W - 1) * nb, D_A), F32),
        wide(), wide(), wide(),
        wide(), wide(), wide(), wide(),
        wide(),
        slab8(), slab8(), slab8(), slab8(),
    ]
    return pl.pallas_call(
        functools.partial(_decode_kernel, nb=nb, bt=bt, tl=tl),
        grid=(DEPTH * n_blk,),
        in_specs=in_specs,
        out_specs=out_specs,
        out_shape=out_shape,
        scratch_shapes=scratch,
        compiler_params=pltpu.CompilerParams(dimension_semantics=("arbitrary",), vmem_limit_bytes=VMEM_LIMIT),
        name="decode_layers",
    )(*args)


CT = SUBLANES


def _prompt_kernel(x_ref, mod_ref, prew_ref, postw_ref, win_ref, convw_ref, convb_ref, wg_ref, ba_ref, bx_ref,
                   lam_ref, lbl_ref, onw_ref, wout_ref,
                   y_ref, hn_ref, convn_ref, sn_ref,
                   xt_s, hin_s, xpad_s, ccar_s, ga_s, a_s, u_s, xc_s, q_s, kk_s, b_s, gb_s, bl_s, hcar_s, st_s,
                   xl_s, qe_s, kd_s, v_s, oi_s,
                   *, bt, tl, n_levels):
    o_s = b_s
    m_s = hin_s
    j = pl.program_id(0)
    nj = pl.num_programs(0)
    rows = tl * bt
    ch = CT * bt
    n_ch = tl // CT
    nh = N_HEADS_B
    nt = D_MODEL // LANES
    tail = (CONV_W - 1) * bt

    @pl.when(j == 0)
    def _():
        hcar_s[...] = jnp.zeros_like(hcar_s)
        ccar_s[...] = jnp.zeros_like(ccar_s)
        st_s[...] = jnp.zeros_like(st_s)

    for c in range(n_ch):
        for jl in range(nt):
            for b in range(bt):
                xt_s[jl, pl.ds(c * ch + b, CT, stride=bt), :] = x_ref[b, c * CT:(c + 1) * CT, jl * LANES:(jl + 1) * LANES]

    ti = lax.broadcasted_iota(jnp.int32, (tl, tl), 0)
    si = lax.broadcasted_iota(jnp.int32, (tl, tl), 1)
    masks = []
    for lv in range(n_levels):
        h = CT << lv
        masks.append(((ti // (2 * h)) == (si // (2 * h))) & ((ti // h) % 2 == 1) & ((si // h) % 2 == 0))

    def bcast(v):
        return jnp.concatenate([v] * CT, axis=0)

    for l in range(DEPTH):
        shift, scale, gate = mod_ref[l, 0], mod_ref[l, 1], mod_ref[l, 2]
        pm = prew_ref[l] * (1.0 + scale)
        gp = postw_ref[l] * gate

        for c in range(n_ch):
            rs = pl.ds(c * ch, ch)
            ss = jnp.sum(sum(jnp.square(xt_s[jl, rs, :]) for jl in range(nt)), axis=-1, keepdims=True)
            inv = lax.rsqrt(ss * (1.0 / D_MODEL) + EPS)
            for jl in range(nt):
                ls = slice(jl * LANES, (jl + 1) * LANES)
                hin_s[rs, ls] = (xt_s[jl, rs, :] * inv * bcast(pm[:, ls]) + bcast(shift[:, ls])).astype(BF16)

        def proj(c0, width):
            return _dot(hin_s[...], win_ref[l, :, c0:c0 + width])
        for dst, c0_ in ((q_s, 2 * D_A), (kk_s, 2 * D_A + D_B), (v_s, 2 * D_A + 2 * D_B), (gb_s, 2 * D_A + 3 * D_B)):
            z = proj(c0_, D_B)
            for hd in range(nh):
                dst[hd] = z[:, hd * DK_B:(hd + 1) * DK_B]

        logits = lbl_ref[...]
        pe = jnp.exp(logits - jnp.max(logits, axis=0, keepdims=True))
        probs = pe / jnp.sum(pe, axis=0, keepdims=True)
        lb = jnp.sum(probs[1:l + 1], axis=0, keepdims=True) if l > 0 else jnp.zeros((1, D_B), F32)
        c0 = 0.5 * (1.0 + lb)
        c1 = 0.5 * (1.0 - lb)
        bl = [jnp.zeros((bt, DK_B), F32) for _ in range(nh)]
        for t in range(tl):
            sl = pl.ds(t * bt, bt)
            for hd in range(nh):
                ls = slice(hd * DK_B, (hd + 1) * DK_B)
                p = c1[:, ls] * jnp.tanh(0.5 * kk_s[hd, sl, :])
                bl[hd] = bl[hd] + jnp.log(jnp.maximum(c0[:, ls] + p, F_MIN))
                b_s[hd, sl, :] = bl[hd]
                kk_s[hd, sl, :] = c1[:, ls] - p
        for hd in range(nh):
            bl_s[hd] = jnp.exp(bl[hd])

        for c in range(n_ch):
            rs = pl.ds(c * ch, ch)
            for hd in range(nh):
                bc, qc, kc, vc = b_s[hd, rs, :], q_s[hd, rs, :], kk_s[hd, rs, :], v_s[hd, rs, :]
                qe_s[hd, rs, :] = qc * jnp.exp(bc)
                kd_s[hd, rs, :] = kc * jnp.exp(bcast(bl[hd]) - bc)
                near = []
                for t in range(CT):
                    ts = slice(t * bt, (t + 1) * bt)
                    acc = jnp.sum(qc[ts] * kc[ts], axis=-1, keepdims=True) * vc[ts]
                    for u in range(t):
                        us = slice(u * bt, (u + 1) * bt)
                        w = qc[ts] * kc[us] * jnp.exp(bc[ts] - bc[us])
                        acc = acc + jnp.sum(w, axis=-1, keepdims=True) * vc[us]
                    near.append(acc)
                oi_s[hd, rs, :] = jnp.concatenate(near, axis=0)
                t0 = c * CT
                for lv in range(n_levels):
                    h = CT << lv
                    mid = (t0 // (2 * h)) * 2 * h + h - 1
                    bmid = bcast(b_s[hd, pl.ds(mid * bt, bt), :])
                    xl_s[lv, hd, rs, :] = qc * jnp.exp(bc - bmid) if t0 % (2 * h) >= h else kc * jnp.exp(bmid - bc)

        xpad_s[pl.ds(0, tail), :] = ccar_s[l]
        xpad_s[pl.ds(tail, rows), :] = proj(0, D_A)
        ga_s[...] = proj(D_A, D_A)
        cw = convw_ref[l]
        for c in range(n_ch):
            xc = convb_ref[l] + sum(xpad_s[pl.ds(c * ch + k * bt, ch), :] * cw[k:k + 1, :] for k in range(CONV_W))
            xc_s[pl.ds(c * ch, ch), :] = xc
        ctail = xpad_s[pl.ds(rows, tail), :]
        ccar_s[l] = ctail
        for c in range(D_A // LANES):
            g2 = _dot(xc_s[:, c * LANES:(c + 1) * LANES].astype(BF16), wg_ref[l, c])
            a_s[:, c * LANES:(c + 1) * LANES] = g2[:, :LANES]
            u_s[:, c * LANES:(c + 1) * LANES] = g2[:, LANES:]
        coef = -C_RGLRU * jax.nn.softplus(-lam_ref[l])
        ba, bx = ba_ref[l], bx_ref[l]

        def scan_chunk(c, h):
            for tp in range(CT // 2):
                yas = []
                for t2 in range(2):
                    sl = pl.ds(c * ch + (2 * tp + t2) * bt, bt)
                    r = _sig(a_s[sl, :] + ba)
                    i = _sig(u_s[sl, :] + bx)
                    log_a = coef * r
                    a = jnp.exp(log_a)
                    u = jnp.sqrt(jnp.maximum(jnp.tanh(-log_a) * (1.0 + a * a), 0.0)) * (i * xc_s[sl, :])
                    h = a * h + u
                    ga = ga_s[sl, :]
                    yas.append(h * (ga * _sig(ga)))
                m_s[pl.ds(c * ch + 2 * tp * bt, 2 * bt), :D_A] = jnp.concatenate(yas, axis=0).astype(BF16)
            return h

        def chains(b):
            sl = pl.ds(b, tl, stride=bt)
            hs = range(nh)
            xs = [[xl_s[lv, hd, sl, :].astype(BF16) for lv in range(n_levels)] for hd in hs]
            vs = [v_s[hd, sl, :].astype(BF16) for hd in hs]
            sts = [st_s[l, b, hd] for hd in hs]
            grams = [[_dot_nt(xs[hd][lv], xs[hd][lv]) for lv in range(n_levels)] for hd in hs]
            upd = [_dot_tn(vs[hd], kd_s[hd, sl, :].astype(BF16)) for hd in hs]
            amats = []
            for hd in hs:
                amat = jnp.zeros((tl, tl), F32)
                for lv in range(n_levels):
                    amat = amat + jnp.where(masks[lv], grams[hd][lv], 0.0)
                amats.append(amat.astype(BF16))
            os_ = [_dot(amats[hd], vs[hd]) + _dot_nt(qe_s[hd, sl, :].astype(BF16), sts[hd].astype(BF16)) for hd in hs]
            for hd in hs:
                gb = gb_s[hd, sl, :]
                o_s[hd, sl, :] = (_rms(os_[hd] + oi_s[hd, sl, :], onw_ref[l][:, hd * DV_B:(hd + 1) * DV_B])
                                  * (gb * _sig(gb)))
                st_s[l, b, hd] = sts[hd] * bl_s[hd, b:b + 1, :] + upd[hd]

        h_last = hcar_s[l]
        for idx in range(max(bt, n_ch)):
            if idx < bt:
                chains(idx)
            if idx < n_ch:
                h_last = scan_chunk(idx, h_last)
        hcar_s[l] = h_last

        for hd in range(nh):
            m_s[:, D_A + hd * DV_B:D_A + (hd + 1) * DV_B] = o_s[hd].astype(BF16)

        mo = _dot(m_s[...], wout_ref[l])
        for jl in range(nt):
            xl_s[jl // nh, jl % nh] = mo[:, jl * LANES:(jl + 1) * LANES]
        for c in range(n_ch):
            rs = pl.ds(c * ch, ch)
            ss = jnp.sum(sum(jnp.square(xl_s[jl // nh, jl % nh, rs, :]) for jl in range(nt)), axis=-1, keepdims=True)
            inv = lax.rsqrt(ss * (1.0 / D_MODEL) + EPS)
            for jl in range(nt):
                ls = slice(jl * LANES, (jl + 1) * LANES)
                xt_s[jl, rs, :] = xt_s[jl, rs, :] + xl_s[jl // nh, jl % nh, rs, :] * inv * bcast(gp[:, ls])

        @pl.when(j == nj - 1)
        def _():
            hn_ref[l] = h_last
            convn_ref[l] = ctail.reshape(CONV_W - 1, bt, D_A)
            for b in range(bt):
                for hd in range(nh):
                    sn_ref[l, b, hd] = st_s[l, b, hd].T

    for c in range(n_ch):
        for jl in range(nt):
            for b in range(bt):
                y_ref[b, c * CT:(c + 1) * CT, jl * LANES:(jl + 1) * LANES] = xt_s[jl, pl.ds(c * ch + b, CT, stride=bt), :]


def _prompt(x, mod, w, *, tl):
    B, L, _ = x.shape
    bt = B
    assert bt == SUBLANES and L % tl == 0 and tl % CT == 0
    n_levels = (tl // CT).bit_length() - 1
    assert (CT << n_levels) == tl and n_levels * N_HEADS_B >= D_MODEL // LANES
    rows = tl * bt
    tail = (CONV_W - 1) * bt

    def const(shape):
        nd = len(shape)
        return pl.BlockSpec(shape, lambda j: (0,) * nd, pipeline_mode=pl.Buffered(1))
    in_specs = [pl.BlockSpec((bt, tl, D_MODEL), lambda j: (0, j, 0)), const(mod.shape)] + [const(a.shape) for a in w]
    out_specs = [
        pl.BlockSpec((bt, tl, D_MODEL), lambda j: (0, j, 0)),
        pl.BlockSpec((DEPTH, bt, D_A), lambda j: (0, 0, 0)),
        pl.BlockSpec((DEPTH, CONV_W - 1, bt, D_A), lambda j: (0, 0, 0, 0)),
        pl.BlockSpec((DEPTH, bt, N_HEADS_B, DK_B, DV_B), lambda j: (0, 0, 0, 0, 0)),
    ]
    out_shape = [
        jax.ShapeDtypeStruct((B, L, D_MODEL), F32),
        jax.ShapeDtypeStruct((DEPTH, B, D_A), F32),
        jax.ShapeDtypeStruct((DEPTH, CONV_W - 1, B, D_A), F32),
        jax.ShapeDtypeStruct((DEPTH, B, N_HEADS_B, DK_B, DV_B), F32),
    ]
    slab = lambda n: pltpu.VMEM((n, rows, LANES), F32)
    wide = lambda: pltpu.VMEM((rows, D_A), F32)
    scratch = [
        slab(D_MODEL // LANES),
        pltpu.VMEM((rows, D_MODEL), BF16),
        pltpu.VMEM((rows + tail, D_A), F32),
        pltpu.VMEM((DEPTH, tail, D_A), F32),
        wide(), wide(), wide(), wide(),
        slab(N_HEADS_B), slab(N_HEADS_B), slab(N_HEADS_B), slab(N_HEADS_B),
        pltpu.VMEM((N_HEADS_B, bt, DK_B), F32),
        pltpu.VMEM((DEPTH, bt, D_A), F32),
        pltpu.VMEM((DEPTH, bt, N_HEADS_B, DV_B, DK_B), F32),
        pltpu.VMEM((n_levels, N_HEADS_B, rows, LANES), F32),
        slab(N_HEADS_B), slab(N_HEADS_B), slab(N_HEADS_B), slab(N_HEADS_B),
    ]
    body = functools.partial(_prompt_kernel, bt=bt, tl=tl, n_levels=n_levels)
    return pl.pallas_call(
        body,
        grid=(L // tl,),
        in_specs=in_specs,
        out_specs=out_specs,
        out_shape=out_shape,
        scratch_shapes=scratch,
        compiler_params=pltpu.CompilerParams(dimension_semantics=("arbitrary",), vmem_limit_bytes=VMEM_LIMIT),
        name="prompt_layers",
    )(x, mod, *w)


def _gate_weights(wa, wx):
    per = LANES // BLK_A

    def bd(w):
        w = w.reshape(D_A // LANES, per, BLK_A, BLK_A)
        eye = jnp.eye(per, dtype=w.dtype)
        return jnp.einsum('jpcd,pq->jpcqd', w, eye).reshape(D_A // LANES, LANES, LANES)
    return jnp.concatenate([bd(wa), bd(wx)], axis=-1).astype(BF16)


def kernel(x_prompt, x_sample, state_rglru_h, state_rglru_conv, state_hgrn_S, c_prompt, c_sample, ada_w, ada_b,
           pre_norm_w, post_norm_w, w_in, conv_w, conv_b, rg_wa, rg_ba, rg_wx, rg_bx, rg_lambda, hg_lb_logits,
           hg_onorm_w, w_out):
    bp, bs = x_prompt.shape[0], x_sample.shape[0]
    mod = _modulation(jnp.concatenate([c_prompt, c_sample], axis=0), ada_w,
                      ada_b.reshape(DEPTH, 1, 3 * D_MODEL))
    w_in_bf, w_out_bf = w_in.astype(BF16), w_out.astype(BF16)
    wg = jnp.stack([_gate_weights(rg_wa[l], rg_wx[l]) for l in range(DEPTH)])
    row = lambda a: a[:, None, :]
    wp = (row(pre_norm_w), row(post_norm_w), w_in_bf, conv_w, row(conv_b), wg, row(rg_ba), row(rg_bx),
          row(rg_lambda), hg_lb_logits, row(hg_onorm_w), w_out_bf)
    yp, hp, cvp, sp = _prompt(x_prompt, mod[:, :, :bp], wp, tl=64)

    ys, hs, cvs, ss = _decode(jnp.transpose(x_sample, (1, 0, 2)), mod[:, :, bp:], state_rglru_h,
                              jnp.transpose(state_rglru_conv, (0, 2, 1, 3)), state_hgrn_S, wp, bt=16)
    return (yp, jnp.transpose(ys, (1, 0, 2)), hp, jnp.transpose(cvp, (0, 2, 1, 3)), sp,
            hs, jnp.transpose(cvs, (0, 2, 1, 3)), ss)
```

```python
import functools

import jax
import jax.numpy as jnp
from jax import lax
from jax.experimental import pallas as pl
from jax.experimental.pallas import tpu as pltpu

D_MODEL = 1024
DEPTH = 2
D_A = 512
N_BLK_A = 8
BLK_A = 64
CONV_W = 4
C_RGLRU = 8.0
D_B = 512
N_HEADS_B = 4
DK_B = 128
DV_B = 128
P_IN = 2 * D_A + 4 * D_B
EPS = 1e-6
F_MIN = 1e-30

LANES = 128
SUBLANES = 8
VMEM_LIMIT = 60 * 1024 * 1024

BF16 = jnp.bfloat16
F32 = jnp.float32


def _dot(a, b):
    return jnp.dot(a, b, preferred_element_type=F32)


def _dot_nt(a, b):
    return lax.dot_general(a, b, (((1,), (1,)), ((), ())), preferred_element_type=F32)


def _dot_tn(a, b):
    return lax.dot_general(a, b, (((0,), (0,)), ((), ())), preferred_element_type=F32)


def _rms(x, w):
    return x * lax.rsqrt(jnp.mean(x * x, axis=-1, keepdims=True) + EPS) * w


def _silu(x):
    return x * jax.nn.sigmoid(x)


def _mod_kernel(c_ref, w_ref, b_ref, o_ref):
    o_ref[...] = _dot(_silu(c_ref[...]).astype(BF16), w_ref[...].astype(BF16)) + b_ref[...]


def _modulation(c_all, ada_w, ada_b):
    nb = c_all.shape[0]
    return pl.pallas_call(
        _mod_kernel,
        grid=(DEPTH, 3),
        in_specs=[
            pl.BlockSpec((nb, D_MODEL), lambda l, k: (0, 0)),
            pl.BlockSpec((None, D_MODEL, D_MODEL), lambda l, k: (l, 0, k)),
            pl.BlockSpec((None, 1, D_MODEL), lambda l, k: (l, 0, k)),
        ],
        out_specs=pl.BlockSpec((None, None, nb, D_MODEL), lambda l, k: (l, k, 0, 0)),
        out_shape=jax.ShapeDtypeStruct((DEPTH, 3, nb, D_MODEL), F32),
        name="adaln_mod",
    )(c_all, ada_w, ada_b)


TP = SUBLANES


def _sig_h(xh):
    return 0.5 * jnp.tanh(xh) + 0.5


def _silu_h(xh):
    return xh * jnp.tanh(xh) + xh


def _sqrt_pos(y):
    y = jnp.maximum(y, 0.0)
    return jnp.where(y > 0.0, y * lax.rsqrt(y), 0.0)


def _decode_kernel(x_ref, mod_ref, h0_ref, conv0_ref, s0_ref, prew_ref, postw_ref, win_ref, convw_ref,
                   convb_ref, wg_ref, ba_ref, bx_ref, lam_ref, lbl_ref, onw_ref, wout_ref,
                   y_ref, hn_ref, convn_ref, sn_ref,
                   xt_s, hin_s, xpad_s, ga_s, a_s, u_s, q_s, kk_s, b_s, gb_s, oi_s, qe8_s, kdx8_s, v8_s, oc8_s,
                   *, nb, bt, tl):
    m_s = hin_s
    i = pl.program_id(0)
    n_blk = nb // bt
    layer = i // n_blk
    blk = i % n_blk
    rows = tl * nb
    tail = (CONV_W - 1) * nb
    nh = N_HEADS_B
    heads = [slice(hd * DK_B, (hd + 1) * DK_B) for hd in range(nh)]

    @pl.when(i == 0)
    def _():
        xt_s[...] = x_ref[...]

    @pl.when(blk == 0)
    def _():
        shift, scale = mod_ref[0], mod_ref[1]
        pm = prew_ref[...] * (1.0 + scale)
        for t in range(tl):
            xt = xt_s[t]
            inv = lax.rsqrt(jnp.mean(xt * xt, axis=-1, keepdims=True) + EPS)
            hin_s[pl.ds(t * nb, nb), :] = (xt * inv * pm + shift).astype(BF16)

        def proj(c0, width):
            return _dot(hin_s[...], win_ref[:, c0:c0 + width])
        xpad_s[pl.ds(0, tail), :] = conv0_ref[...].reshape(tail, D_A)
        xpad_s[pl.ds(tail, rows), :] = proj(0, D_A)
        ga_s[...] = proj(D_A, D_A)
        q_s[...] = proj(2 * D_A, D_B)
        kk_s[...] = proj(2 * D_A + D_B, D_B)
        iv = proj(2 * D_A + 2 * D_B, D_B)
        gb_s[...] = proj(2 * D_A + 3 * D_B, D_B)

        xc = convb_ref[...] + sum(xpad_s[pl.ds(k * nb, rows), :] * convw_ref[k:k + 1, :] for k in range(CONV_W))
        convn_ref[...] = xpad_s[pl.ds(rows, tail), :].reshape(CONV_W - 1, nb, D_A)
        xcb = xc.astype(BF16)
        for c in range(D_A // LANES):
            g2 = _dot(xcb[:, c * LANES:(c + 1) * LANES], wg_ref[c])
            a_s[:, c * LANES:(c + 1) * LANES] = g2[:, :LANES]
            u_s[:, c * LANES:(c + 1) * LANES] = g2[:, LANES:]
        coef = -C_RGLRU * jax.nn.softplus(-lam_ref[...])
        h = h0_ref[...]
        for t in range(tl):
            sl = pl.ds(t * nb, nb)
            r = _sig_h(a_s[sl, :] + ba_ref[...])
            ig = _sig_h(u_s[sl, :] + bx_ref[...])
            log_a = coef * r
            a = jnp.exp(log_a)
            u = _sqrt_pos(jnp.tanh(-log_a) * (1.0 + a * a)) * (ig * xc[t * nb:(t + 1) * nb])
            h = a * h + u
            m_s[sl, :D_A] = (h * _silu_h(ga_s[sl, :])).astype(BF16)
        hn_ref[...] = h

        logits = lbl_ref[...]
        pe = jnp.exp(logits - jnp.max(logits, axis=0, keepdims=True))
        probs = pe / jnp.sum(pe, axis=0, keepdims=True)
        lb = jnp.zeros((1, D_B), F32)
        for k in range(1, DEPTH):
            lb = lb + jnp.where(layer >= k, probs[k:k + 1], 0.0)
        c0 = 0.5 * (1.0 + lb)
        c1 = 0.5 * (1.0 - lb)
        bl = jnp.zeros((nb, D_B), F32)
        for t in range(tl):
            sl = pl.ds(t * nb, nb)
            p = c1 * jnp.tanh(0.5 * kk_s[sl, :])
            bl = bl + jnp.log(jnp.maximum(c0 + p, F_MIN))
            b_s[sl, :] = bl
            kk_s[sl, :] = c1 - p

        for t in range(tl):
            st_ = pl.ds(t * nb, nb)
            acc = [jnp.zeros((nb, DV_B), F32) for _ in range(nh)]
            for s in range(t + 1):
                ss_ = pl.ds(s * nb, nb)
                w = q_s[st_, :] * kk_s[ss_, :]
                if s < t:
                    w = w * jnp.exp(b_s[st_, :] - b_s[ss_, :])
                for hd in range(nh):
                    acc[hd] = acc[hd] + jnp.sum(w[:, heads[hd]], axis=-1, keepdims=True) * iv[s * nb:(s + 1) * nb, heads[hd]]
            for hd in range(nh):
                oi_s[st_, heads[hd]] = acc[hd]

        zrow = jnp.zeros((nb, LANES), F32)
        ebl = jnp.exp(bl)
        e_hi = ebl.astype(BF16).astype(F32)
        e_mid = (ebl - e_hi).astype(BF16).astype(F32)
        e_lo = ebl - e_hi - e_mid
        for hd in range(nh):
            for t in range(TP):
                dst = pl.ds(t, nb, stride=TP)
                if t < tl:
                    st_ = pl.ds(t * nb, nb)
                    bt_ = b_s[st_, heads[hd]]
                    qe8_s[hd, dst, :] = q_s[st_, heads[hd]] * jnp.exp(bt_)
                    kdx8_s[hd, dst, :] = kk_s[st_, heads[hd]] * jnp.exp(bl[:, heads[hd]] - bt_)
                    v8_s[hd, dst, :] = iv[t * nb:(t + 1) * nb, heads[hd]]
                else:
                    qe8_s[hd, dst, :] = zrow
                    v8_s[hd, dst, :] = zrow
                    dec = (e_hi, e_mid, e_lo)
                    kdx8_s[hd, dst, :] = dec[t - tl][:, heads[hd]] if t - tl < len(dec) else zrow

    rid = lax.broadcasted_iota(jnp.int32, (TP, LANES), 0)
    ones_rows = jnp.where((rid >= tl) & (rid < tl + 3), 1.0, 0.0)
    for bl_ in range(bt):
        r8 = pl.ds(pl.multiple_of((blk * bt + bl_) * TP, TP), TP)
        for hd in range(nh):
            s_old = s0_ref[bl_, hd]
            oc8_s[hd, r8, :] = _dot(qe8_s[hd, r8, :].astype(BF16), s_old.astype(BF16))
            rhs = jnp.concatenate([v8_s[hd, r8, :], ones_rows], axis=1).astype(BF16)
            g = _dot_tn(kdx8_s[hd, r8, :].astype(BF16), rhs)
            sn_ref[bl_, hd] = s_old * g[:, DV_B:] + g[:, :DV_B]

    @pl.when(blk == n_blk - 1)
    def _():
        for t in range(tl):
            st_ = pl.ds(t * nb, nb)
            for hd in range(nh):
                oh = oi_s[st_, heads[hd]] + oc8_s[hd, pl.ds(t, nb, stride=TP), :]
                yb = _rms(oh, onw_ref[:, heads[hd]]) * _silu_h(gb_s[st_, heads[hd]])
                m_s[st_, D_A + hd * DV_B:D_A + (hd + 1) * DV_B] = yb.astype(BF16)
        mo = _dot(m_s[...], wout_ref[...])
        gp = postw_ref[...] * mod_ref[2]
        for t in range(tl):
            mt = mo[t * nb:(t + 1) * nb]
            inv = lax.rsqrt(jnp.mean(mt * mt, axis=-1, keepdims=True) + EPS)
            xt_s[t] = xt_s[t] + mt * inv * gp

    @pl.when(i == pl.num_programs(0) - 1)
    def _():
        y_ref[...] = xt_s[...]


def _decode(x_tm, mod, h0, conv0_tm, s_all, w, *, bt):
    tl, nb, _ = x_tm.shape
    assert nb % bt == 0 and nb % SUBLANES == 0 and CONV_W - 1 <= tl and tl + 3 <= TP
    rows = tl * nb
    n_blk = nb // bt

    def whole(a):
        nd = a.ndim
        return pl.BlockSpec(a.shape, lambda i: (0,) * nd, pipeline_mode=pl.Buffered(1))

    def per_layer(a):
        nd = a.ndim - 1
        return pl.BlockSpec((None,) + a.shape[1:], lambda i: (i // n_blk,) + (0,) * nd, pipeline_mode=pl.Buffered(1))
    state_spec = pl.BlockSpec((None, bt, N_HEADS_B, DK_B, DV_B), lambda i: (i // n_blk, i % n_blk, 0, 0, 0))
    args = [x_tm, mod, h0, conv0_tm, s_all, *w]
    in_specs = [whole(x_tm), per_layer(mod), per_layer(h0), per_layer(conv0_tm), state_spec]
    in_specs += [whole(a) if a.shape[0] == DEPTH and a.ndim == 2 else per_layer(a) for a in w]
    out_specs = [
        pl.BlockSpec((tl, nb, D_MODEL), lambda i: (0, 0, 0)),
        pl.BlockSpec((None, nb, D_A), lambda i: (i // n_blk, 0, 0)),
        pl.BlockSpec((None, CONV_W - 1, nb, D_A), lambda i: (i // n_blk, 0, 0, 0)),
        state_spec,
    ]
    out_shape = [
        jax.ShapeDtypeStruct((tl, nb, D_MODEL), F32),
        jax.ShapeDtypeStruct((DEPTH, nb, D_A), F32),
        jax.ShapeDtypeStruct((DEPTH, CONV_W - 1, nb, D_A), F32),
        jax.ShapeDtypeStruct(s_all.shape, F32),
    ]
    wide = lambda: pltpu.VMEM((rows, D_A), F32)
    slab8 = lambda: pltpu.VMEM((N_HEADS_B, nb * TP, LANES), F32)
    scratch = [
        pltpu.VMEM((tl, nb, D_MODEL), F32),
        pltpu.VMEM((rows, D_MODEL), BF16),
        pltpu.VMEM((rows + (CONV_W - 1) * nb, D_A), F32),
        wide(), wide(), wide(),
        wide(), wide(), wide(), wide(),
        wide(),
        slab8(), slab8(), slab8(), slab8(),
    ]
    return pl.pallas_call(
        functools.partial(_decode_kernel, nb=nb, bt=bt, tl=tl),
        grid=(DEPTH * n_blk,),
        in_specs=in_specs,
        out_specs=out_specs,
        out_shape=out_shape,
        scratch_shapes=scratch,
        compiler_params=pltpu.CompilerParams(dimension_semantics=("arbitrary",), vmem_limit_bytes=VMEM_LIMIT),
        name="decode_layers",
    )(*args)


CT = SUBLANES


def _prompt_kernel(x_ref, mod_ref, prew_ref, postw_ref, win_ref, convw_ref, convb_ref, wg_ref, ba_ref, bx_ref,
                   lam_ref, lbl_ref, onw_ref, wout_ref,
                   y_ref, hn_ref, convn_ref, sn_ref,
                   xt_s, hin_s, xpad_s, ccar_s, ga_s, a_s, u_s, xc_s, q_s, kk_s, b_s, gb_s, bl_s, hcar_s, st_s,
                   xl_s, qe_s, kd_s, v_s, oi_s,
                   *, bt, tl, n_levels):
    o_s = b_s
    m_s = hin_s
    j = pl.program_id(0)
    nj = pl.num_programs(0)
    rows = tl * bt
    ch = CT * bt
    n_ch = tl // CT
    nh = N_HEADS_B
    nt = D_MODEL // LANES
    tail = (CONV_W - 1) * bt

    @pl.when(j == 0)
    def _():
        hcar_s[...] = jnp.zeros_like(hcar_s)
        ccar_s[...] = jnp.zeros_like(ccar_s)
        st_s[...] = jnp.zeros_like(st_s)

    for c in range(n_ch):
        for jl in range(nt):
            for b in range(bt):
                xt_s[jl, pl.ds(c * ch + b, CT, stride=bt), :] = x_ref[b, c * CT:(c + 1) * CT, jl * LANES:(jl + 1) * LANES]

    ti = lax.broadcasted_iota(jnp.int32, (tl, tl), 0)
    si = lax.broadcasted_iota(jnp.int32, (tl, tl), 1)
    masks = []
    for lv in range(n_levels):
        h = CT << lv
        masks.append(((ti // (2 * h)) == (si // (2 * h))) & ((ti // h) % 2 == 1) & ((si // h) % 2 == 0))

    def bcast(v):
        return jnp.concatenate([v] * CT, axis=0)

    for l in range(DEPTH):
        shift, scale, gate = mod_ref[l, 0], mod_ref[l, 1], mod_ref[l, 2]
        pm = prew_ref[l] * (1.0 + scale)
        gp = postw_ref[l] * gate

        for c in range(n_ch):
            rs = pl.ds(c * ch, ch)
            ss = jnp.sum(sum(jnp.square(xt_s[jl, rs, :]) for jl in range(nt)), axis=-1, keepdims=True)
            inv = lax.rsqrt(ss * (1.0 / D_MODEL) + EPS)
            for jl in range(nt):
                ls = slice(jl * LANES, (jl + 1) * LANES)
                hin_s[rs, ls] = (xt_s[jl, rs, :] * inv * bcast(pm[:, ls]) + bcast(shift[:, ls])).astype(BF16)

        def proj(c0, width):
            return _dot(hin_s[...], win_ref[l, :, c0:c0 + width])
        for dst, c0_ in ((q_s, 2 * D_A), (kk_s, 2 * D_A + D_B), (v_s, 2 * D_A + 2 * D_B), (gb_s, 2 * D_A + 3 * D_B)):
            z = proj(c0_, D_B)
            for hd in range(nh):
                dst[hd] = z[:, hd * DK_B:(hd + 1) * DK_B]

        logits = lbl_ref[...]
        pe = jnp.exp(logits - jnp.max(logits, axis=0, keepdims=True))
        probs = pe / jnp.sum(pe, axis=0, keepdims=True)
        lb = jnp.sum(probs[1:l + 1], axis=0, keepdims=True) if l > 0 else jnp.zeros((1, D_B), F32)
        c0 = 0.5 * (1.0 + lb)
        c1 = 0.5 * (1.0 - lb)
        bl = [jnp.zeros((bt, DK_B), F32) for _ in range(nh)]
        for t in range(tl):
            sl = pl.ds(t * bt, bt)
            for hd in range(nh):
                ls = slice(hd * DK_B, (hd + 1) * DK_B)
                p = c1[:, ls] * jnp.tanh(0.5 * kk_s[hd, sl, :])
                bl[hd] = bl[hd] + jnp.log(jnp.maximum(c0[:, ls] + p, F_MIN))
                b_s[hd, sl, :] = bl[hd]
                kk_s[hd, sl, :] = c1[:, ls] - p
        for hd in range(nh):
            bl_s[hd] = jnp.exp(bl[hd])

        for c in range(n_ch):
            rs = pl.ds(c * ch, ch)
            for hd in range(nh):
                bc, qc, kc, vc = b_s[hd, rs, :], q_s[hd, rs, :], kk_s[hd, rs, :], v_s[hd, rs, :]
                qe_s[hd, rs, :] = qc * jnp.exp(bc)
                kd_s[hd, rs, :] = kc * jnp.exp(bcast(bl[hd]) - bc)
                near = []
                for t in range(CT):
                    ts = slice(t * bt, (t + 1) * bt)
                    acc = jnp.sum(qc[ts] * kc[ts], axis=-1, keepdims=True) * vc[ts]
                    for u in range(t):
                        us = slice(u * bt, (u + 1) * bt)
                        w = qc[ts] * kc[us] * jnp.exp(bc[ts] - bc[us])
                        acc = acc + jnp.sum(w, axis=-1, keepdims=True) * vc[us]
                    near.append(acc)
                oi_s[hd, rs, :] = jnp.concatenate(near, axis=0)
                t0 = c * CT
                for lv in range(n_levels):
                    h = CT << lv
                    mid = (t0 // (2 * h)) * 2 * h + h - 1
                    bmid = bcast(b_s[hd, pl.ds(mid * bt, bt), :])
                    xl_s[lv, hd, rs, :] = qc * jnp.exp(bc - bmid) if t0 % (2 * h) >= h else kc * jnp.exp(bmid - bc)

        xpad_s[pl.ds(0, tail), :] = ccar_s[l]
        xpad_s[pl.ds(tail, rows), :] = proj(0, D_A)
        ga_s[...] = proj(D_A, D_A)
        cw = convw_ref[l]
        for c in range(n_ch):
            xc = convb_ref[l] + sum(xpad_s[pl.ds(c * ch + k * bt, ch), :] * cw[k:k + 1, :] for k in range(CONV_W))
            xc_s[pl.ds(c * ch, ch), :] = xc
        ctail = xpad_s[pl.ds(rows, tail), :]
        ccar_s[l] = ctail
        for c in range(D_A // LANES):
            g2 = _dot(xc_s[:, c * LANES:(c + 1) * LANES].astype(BF16), wg_ref[l, c])
            a_s[:, c * LANES:(c + 1) * LANES] = g2[:, :LANES]
            u_s[:, c * LANES:(c + 1) * LANES] = g2[:, LANES:]
        coef = -C_RGLRU * jax.nn.softplus(-lam_ref[l])
        ba, bx = ba_ref[l], bx_ref[l]

        def scan_chunk(c, h):
            for tp in range(CT // 2):
                yas = []
                for t2 in range(2):
                    sl = pl.ds(c * ch + (2 * tp + t2) * bt, bt)
                    r = _sig_h(a_s[sl, :] + ba)
                    i = _sig_h(u_s[sl, :] + bx)
                    log_a = coef * r
                    a = jnp.exp(log_a)
                    u = _sqrt_pos(jnp.tanh(-log_a) * (1.0 + a * a)) * (i * xc_s[sl, :])
                    h = a * h + u
                    yas.append(h * _silu_h(ga_s[sl, :]))
                m_s[pl.ds(c * ch + 2 * tp * bt, 2 * bt), :D_A] = jnp.concatenate(yas, axis=0).astype(BF16)
            return h

        def chains(b):
            sl = pl.ds(b, tl, stride=bt)
            hs = range(nh)
            xs = [[xl_s[lv, hd, sl, :].astype(BF16) for lv in range(n_levels)] for hd in hs]
            vs = [v_s[hd, sl, :].astype(BF16) for hd in hs]
            sts = [st_s[l, b, hd] for hd in hs]
            grams = [[_dot_nt(xs[hd][lv], xs[hd][lv]) for lv in range(n_levels)] for hd in hs]
            upd = [_dot_tn(vs[hd], kd_s[hd, sl, :].astype(BF16)) for hd in hs]
            amats = []
            for hd in hs:
                amat = jnp.zeros((tl, tl), F32)
                for lv in range(n_levels):
                    amat = amat + jnp.where(masks[lv], grams[hd][lv], 0.0)
                amats.append(amat.astype(BF16))
            os_ = [_dot(amats[hd], vs[hd]) + _dot_nt(qe_s[hd, sl, :].astype(BF16), sts[hd].astype(BF16)) for hd in hs]
            for hd in hs:
                o_s[hd, sl, :] = (_rms(os_[hd] + oi_s[hd, sl, :], onw_ref[l][:, hd * DV_B:(hd + 1) * DV_B])
                                  * _silu_h(gb_s[hd, sl, :]))
                st_s[l, b, hd] = sts[hd] * bl_s[hd, b:b + 1, :] + upd[hd]

        h_last = hcar_s[l]
        for idx in range(max(bt, n_ch)):
            if idx < bt:
                chains(idx)
            if idx < n_ch:
                h_last = scan_chunk(idx, h_last)
        hcar_s[l] = h_last

        for hd in range(nh):
            m_s[:, D_A + hd * DV_B:D_A + (hd + 1) * DV_B] = o_s[hd].astype(BF16)

        mo = _dot(m_s[...], wout_ref[l])
        for jl in range(nt):
            xl_s[jl // nh, jl % nh] = mo[:, jl * LANES:(jl + 1) * LANES]
        for c in range(n_ch):
            rs = pl.ds(c * ch, ch)
            ss = jnp.sum(sum(jnp.square(xl_s[jl // nh, jl % nh, rs, :]) for jl in range(nt)), axis=-1, keepdims=True)
            inv = lax.rsqrt(ss * (1.0 / D_MODEL) + EPS)
            for jl in range(nt):
                ls = slice(jl * LANES, (jl + 1) * LANES)
                xt_s[jl, rs, :] = xt_s[jl, rs, :] + xl_s[jl // nh, jl % nh, rs, :] * inv * bcast(gp[:, ls])

        @pl.when(j == nj - 1)
        def _():
            hn_ref[l] = h_last
            convn_ref[l] = ctail.reshape(CONV_W - 1, bt, D_A)
            for b in range(bt):
                for hd in range(nh):
                    sn_ref[l, b, hd] = st_s[l, b, hd].T

    for c in range(n_ch):
        for jl in range(nt):
            for b in range(bt):
                y_ref[b, c * CT:(c + 1) * CT, jl * LANES:(jl + 1) * LANES] = xt_s[jl, pl.ds(c * ch + b, CT, stride=bt), :]


def _prompt(x, mod, w, *, tl):
    B, L, _ = x.shape
    bt = B
    assert bt == SUBLANES and L % tl == 0 and tl % CT == 0
    n_levels = (tl // CT).bit_length() - 1
    assert (CT << n_levels) == tl and n_levels * N_HEADS_B >= D_MODEL // LANES
    rows = tl * bt
    tail = (CONV_W - 1) * bt

    def const(shape):
        nd = len(shape)
        return pl.BlockSpec(shape, lambda j: (0,) * nd, pipeline_mode=pl.Buffered(1))
    in_specs = [pl.BlockSpec((bt, tl, D_MODEL), lambda j: (0, j, 0)), const(mod.shape)] + [const(a.shape) for a in w]
    out_specs = [
        pl.BlockSpec((bt, tl, D_MODEL), lambda j: (0, j, 0)),
        pl.BlockSpec((DEPTH, bt, D_A), lambda j: (0, 0, 0)),
        pl.BlockSpec((DEPTH, CONV_W - 1, bt, D_A), lambda j: (0, 0, 0, 0)),
        pl.BlockSpec((DEPTH, bt, N_HEADS_B, DK_B, DV_B), lambda j: (0, 0, 0, 0, 0)),
    ]
    out_shape = [
        jax.ShapeDtypeStruct((B, L, D_MODEL), F32),
        jax.ShapeDtypeStruct((DEPTH, B, D_A), F32),
        jax.ShapeDtypeStruct((DEPTH, CONV_W - 1, B, D_A), F32),
        jax.ShapeDtypeStruct((DEPTH, B, N_HEADS_B, DK_B, DV_B), F32),
    ]
    slab = lambda n: pltpu.VMEM((n, rows, LANES), F32)
    wide = lambda: pltpu.VMEM((rows, D_A), F32)
    scratch = [
        slab(D_MODEL // LANES),
        pltpu.VMEM((rows, D_MODEL), BF16),
        pltpu.VMEM((rows + tail, D_A), F32),
        pltpu.VMEM((DEPTH, tail, D_A), F32),
        wide(), wide(), wide(), wide(),
        slab(N_HEADS_B), slab(N_HEADS_B), slab(N_HEADS_B), slab(N_HEADS_B),
        pltpu.VMEM((N_HEADS_B, bt, DK_B), F32),
        pltpu.VMEM((DEPTH, bt, D_A), F32),
        pltpu.VMEM((DEPTH, bt, N_HEADS_B, DV_B, DK_B), F32),
        pltpu.VMEM((n_levels, N_HEADS_B, rows, LANES), F32),
        slab(N_HEADS_B), slab(N_HEADS_B), slab(N_HEADS_B), slab(N_HEADS_B),
    ]
    body = functools.partial(_prompt_kernel, bt=bt, tl=tl, n_levels=n_levels)
    return pl.pallas_call(
        body,
        grid=(L // tl,),
        in_specs=in_specs,
        out_specs=out_specs,
        out_shape=out_shape,
        scratch_shapes=scratch,
        compiler_params=pltpu.CompilerParams(dimension_semantics=("arbitrary",), vmem_limit_bytes=VMEM_LIMIT),
        name="prompt_layers",
    )(x, mod, *w)


def _gate_weights(wa, wx):
    per = LANES // BLK_A

    def bd(w):
        w = w.reshape(D_A // LANES, per, BLK_A, BLK_A)
        eye = jnp.eye(per, dtype=w.dtype)
        return jnp.einsum('jpcd,pq->jpcqd', w, eye).reshape(D_A // LANES, LANES, LANES)
    return (0.5 * jnp.concatenate([bd(wa), bd(wx)], axis=-1)).astype(BF16)


def kernel(x_prompt, x_sample, state_rglru_h, state_rglru_conv, state_hgrn_S, c_prompt, c_sample, ada_w, ada_b,
           pre_norm_w, post_norm_w, w_in, conv_w, conv_b, rg_wa, rg_ba, rg_wx, rg_bx, rg_lambda, hg_lb_logits,
           hg_onorm_w, w_out):
    bp, bs = x_prompt.shape[0], x_sample.shape[0]
    mod = _modulation(jnp.concatenate([c_prompt, c_sample], axis=0), ada_w,
                      ada_b.reshape(DEPTH, 1, 3 * D_MODEL))
    half = jnp.full((D_A,), 0.5, F32)
    one = jnp.ones((D_A,), F32)
    col_scale = jnp.concatenate([one, half, one, one, one, half])
    w_in_bf, w_out_bf = (w_in * col_scale).astype(BF16), w_out.astype(BF16)
    wg = jnp.stack([_gate_weights(rg_wa[l], rg_wx[l]) for l in range(DEPTH)])
    row = lambda a: a[:, None, :]
    wp = (row(pre_norm_w), row(post_norm_w), w_in_bf, conv_w, row(conv_b), wg, row(0.5 * rg_ba), row(0.5 * rg_bx),
          row(rg_lambda), hg_lb_logits, row(hg_onorm_w), w_out_bf)
    yp, hp, cvp, sp = _prompt(x_prompt, mod[:, :, :bp], wp, tl=64)

    ys, hs, cvs, ss = _decode(jnp.transpose(x_sample, (1, 0, 2)), mod[:, :, bp:], state_rglru_h,
                              jnp.transpose(state_rglru_conv, (0, 2, 1, 3)), state_hgrn_S, wp, bt=16)
    return (yp, jnp.transpose(ys, (1, 0, 2)), hp, jnp.transpose(cvp, (0, 2, 1, 3)), sp,
            hs, jnp.transpose(cvs, (0, 2, 1, 3)), ss)
```

```python
import functools

import jax
import jax.numpy as jnp
from jax import lax
from jax.experimental import pallas as pl
from jax.experimental.pallas import tpu as pltpu

D_MODEL = 1024
DEPTH = 2
D_A = 512
N_BLK_A = 8
BLK_A = 64
CONV_W = 4
C_RGLRU = 8.0
D_B = 512
N_HEADS_B = 4
DK_B = 128
DV_B = 128
P_IN = 2 * D_A + 4 * D_B
EPS = 1e-6
F_MIN = 1e-30

LANES = 128
SUBLANES = 8
VMEM_LIMIT = 60 * 1024 * 1024

BF16 = jnp.bfloat16
F32 = jnp.float32


def _dot(a, b):
    return jnp.dot(a, b, preferred_element_type=F32)


def _dot_nt(a, b):
    return lax.dot_general(a, b, (((1,), (1,)), ((), ())), preferred_element_type=F32)


def _dot_tn(a, b):
    return lax.dot_general(a, b, (((0,), (0,)), ((), ())), preferred_element_type=F32)


def _rms(x, w):
    return x * lax.rsqrt(jnp.mean(x * x, axis=-1, keepdims=True) + EPS) * w


def _silu(x):
    return x * jax.nn.sigmoid(x)


def _mod_kernel(c_ref, w_ref, b_ref, o_ref):
    o_ref[...] = _dot(_silu(c_ref[...]).astype(BF16), w_ref[...].astype(BF16)) + b_ref[...]


def _modulation(c_all, ada_w, ada_b):
    nb = c_all.shape[0]
    return pl.pallas_call(
        _mod_kernel,
        grid=(DEPTH, 3),
        in_specs=[
            pl.BlockSpec((nb, D_MODEL), lambda l, k: (0, 0)),
            pl.BlockSpec((None, D_MODEL, D_MODEL), lambda l, k: (l, 0, k)),
            pl.BlockSpec((None, 1, D_MODEL), lambda l, k: (l, 0, k)),
        ],
        out_specs=pl.BlockSpec((None, None, nb, D_MODEL), lambda l, k: (l, k, 0, 0)),
        out_shape=jax.ShapeDtypeStruct((DEPTH, 3, nb, D_MODEL), F32),
        name="adaln_mod",
    )(c_all, ada_w, ada_b)


TP = SUBLANES


def _sig_h(xh):
    return 0.5 * jnp.tanh(xh) + 0.5


def _silu_h(xh):
    return xh * jnp.tanh(xh) + xh


def _sqrt_pos(y):
    return jnp.where(y > 0.0, y * lax.rsqrt(y), 0.0)


def _decode_kernel(x_ref, mod_ref, h0_ref, conv0_ref, s0_ref, prew_ref, postw_ref, win_ref, convw_ref,
                   convb_ref, wg_ref, ba_ref, bx_ref, lam_ref, lbl_ref, onw_ref, wout_ref,
                   y_ref, hn_ref, convn_ref, sn_ref,
                   xt_s, hin_s, xpad_s, ga_s, a_s, u_s, q_s, kk_s, b_s, gb_s, oi_s, qe8_s, kdx8_s, v8_s, oc8_s,
                   *, nb, bt, tl):
    m_s = hin_s
    i = pl.program_id(0)
    n_blk = nb // bt
    layer = i // n_blk
    blk = i % n_blk
    rows = tl * nb
    tail = (CONV_W - 1) * nb
    nh = N_HEADS_B
    heads = [slice(hd * DK_B, (hd + 1) * DK_B) for hd in range(nh)]

    @pl.when(i == 0)
    def _():
        xt_s[...] = x_ref[...]

    @pl.when(blk == 0)
    def _():
        shift, scale = mod_ref[0], mod_ref[1]
        pm = prew_ref[...] * (1.0 + scale)
        for t in range(tl):
            xt = xt_s[t]
            inv = lax.rsqrt(jnp.mean(xt * xt, axis=-1, keepdims=True) + EPS)
            hin_s[pl.ds(t * nb, nb), :] = (xt * inv * pm + shift).astype(BF16)

        def proj(c0, width):
            return _dot(hin_s[...], win_ref[:, c0:c0 + width])
        xpad_s[pl.ds(0, tail), :] = conv0_ref[...].reshape(tail, D_A)
        xpad_s[pl.ds(tail, rows), :] = proj(0, D_A)
        ga_s[...] = proj(D_A, D_A)
        q_s[...] = proj(2 * D_A, D_B)
        kk_s[...] = proj(2 * D_A + D_B, D_B)
        iv = proj(2 * D_A + 2 * D_B, D_B)
        gb_s[...] = proj(2 * D_A + 3 * D_B, D_B)

        xc = convb_ref[...] + sum(xpad_s[pl.ds(k * nb, rows), :] * convw_ref[k:k + 1, :] for k in range(CONV_W))
        convn_ref[...] = xpad_s[pl.ds(rows, tail), :].reshape(CONV_W - 1, nb, D_A)
        xcb = xc.astype(BF16)
        for c in range(D_A // LANES):
            g2 = _dot(xcb[:, c * LANES:(c + 1) * LANES], wg_ref[c])
            a_s[:, c * LANES:(c + 1) * LANES] = g2[:, :LANES]
            u_s[:, c * LANES:(c + 1) * LANES] = g2[:, LANES:]
        hcoef = -0.5 * C_RGLRU * jax.nn.softplus(-lam_ref[...])
        h = h0_ref[...]
        for t in range(tl):
            sl = pl.ds(t * nb, nb)
            ig = _sig_h(u_s[sl, :] + bx_ref[...])
            log_a = hcoef * jnp.tanh(a_s[sl, :] + ba_ref[...]) + hcoef
            a = jnp.exp(log_a)
            u = _sqrt_pos(jnp.tanh(log_a) * (-1.0 - a * a)) * (ig * xc[t * nb:(t + 1) * nb])
            h = a * h + u
            m_s[sl, :D_A] = (h * _silu_h(ga_s[sl, :])).astype(BF16)
        hn_ref[...] = h

        logits = lbl_ref[...]
        pe = jnp.exp(logits - jnp.max(logits, axis=0, keepdims=True))
        probs = pe / jnp.sum(pe, axis=0, keepdims=True)
        lb = jnp.zeros((1, D_B), F32)
        for k in range(1, DEPTH):
            lb = lb + jnp.where(layer >= k, probs[k:k + 1], 0.0)
        c0 = 0.5 * (1.0 + lb)
        c1 = 0.5 * (1.0 - lb)
        bl = jnp.zeros((nb, D_B), F32)
        for t in range(tl):
            sl = pl.ds(t * nb, nb)
            p = c1 * jnp.tanh(kk_s[sl, :])
            bl = bl + jnp.log(jnp.maximum(c0 + p, F_MIN))
            b_s[sl, :] = bl
            kk_s[sl, :] = c1 - p

        for t in range(tl):
            st_ = pl.ds(t * nb, nb)
            acc = [jnp.zeros((nb, DV_B), F32) for _ in range(nh)]
            for s in range(t + 1):
                ss_ = pl.ds(s * nb, nb)
                w = q_s[st_, :] * kk_s[ss_, :]
                if s < t:
                    w = w * jnp.exp(b_s[st_, :] - b_s[ss_, :])
                for hd in range(nh):
                    acc[hd] = acc[hd] + jnp.sum(w[:, heads[hd]], axis=-1, keepdims=True) * iv[s * nb:(s + 1) * nb, heads[hd]]
            for hd in range(nh):
                oi_s[st_, heads[hd]] = acc[hd]

        zrow = jnp.zeros((nb, LANES), F32)
        ebl = jnp.exp(bl)
        e_hi = ebl.astype(BF16).astype(F32)
        e_mid = (ebl - e_hi).astype(BF16).astype(F32)
        e_lo = ebl - e_hi - e_mid
        for hd in range(nh):
            for t in range(TP):
                dst = pl.ds(t, nb, stride=TP)
                if t < tl:
                    st_ = pl.ds(t * nb, nb)
                    bt_ = b_s[st_, heads[hd]]
                    qe8_s[hd, dst, :] = q_s[st_, heads[hd]] * jnp.exp(bt_)
                    kdx8_s[hd, dst, :] = kk_s[st_, heads[hd]] * jnp.exp(bl[:, heads[hd]] - bt_)
                    v8_s[hd, dst, :] = iv[t * nb:(t + 1) * nb, heads[hd]]
                else:
                    qe8_s[hd, dst, :] = zrow
                    v8_s[hd, dst, :] = zrow
                    dec = (e_hi, e_mid, e_lo)
                    kdx8_s[hd, dst, :] = dec[t - tl][:, heads[hd]] if t - tl < len(dec) else zrow

    rid = lax.broadcasted_iota(jnp.int32, (TP, LANES), 0)
    ones_rows = jnp.where((rid >= tl) & (rid < tl + 3), 1.0, 0.0)
    for bl_ in range(bt):
        r8 = pl.ds(pl.multiple_of((blk * bt + bl_) * TP, TP), TP)
        for hd in range(nh):
            s_old = s0_ref[bl_, hd]
            oc8_s[hd, r8, :] = _dot(qe8_s[hd, r8, :].astype(BF16), s_old.astype(BF16))
            rhs = jnp.concatenate([v8_s[hd, r8, :], ones_rows], axis=1).astype(BF16)
            g = _dot_tn(kdx8_s[hd, r8, :].astype(BF16), rhs)
            sn_ref[bl_, hd] = s_old * g[:, DV_B:] + g[:, :DV_B]

    @pl.when(blk == n_blk - 1)
    def _():
        for t in range(tl):
            st_ = pl.ds(t * nb, nb)
            for hd in range(nh):
                oh = oi_s[st_, heads[hd]] + oc8_s[hd, pl.ds(t, nb, stride=TP), :]
                yb = _rms(oh, onw_ref[:, heads[hd]]) * _silu_h(gb_s[st_, heads[hd]])
                m_s[st_, D_A + hd * DV_B:D_A + (hd + 1) * DV_B] = yb.astype(BF16)
        mo = _dot(m_s[...], wout_ref[...])
        gp = postw_ref[...] * mod_ref[2]
        for t in range(tl):
            mt = mo[t * nb:(t + 1) * nb]
            inv = lax.rsqrt(jnp.mean(mt * mt, axis=-1, keepdims=True) + EPS)
            xt_s[t] = xt_s[t] + mt * inv * gp

    @pl.when(i == pl.num_programs(0) - 1)
    def _():
        y_ref[...] = xt_s[...]


def _decode(x_tm, mod, h0, conv0_tm, s_all, w, *, bt):
    tl, nb, _ = x_tm.shape
    assert nb % bt == 0 and nb % SUBLANES == 0 and CONV_W - 1 <= tl and tl + 3 <= TP
    rows = tl * nb
    n_blk = nb // bt

    def whole(a):
        nd = a.ndim
        return pl.BlockSpec(a.shape, lambda i: (0,) * nd, pipeline_mode=pl.Buffered(1))

    def per_layer(a):
        nd = a.ndim - 1
        return pl.BlockSpec((None,) + a.shape[1:], lambda i: (i // n_blk,) + (0,) * nd, pipeline_mode=pl.Buffered(1))
    state_spec = pl.BlockSpec((None, bt, N_HEADS_B, DK_B, DV_B), lambda i: (i // n_blk, i % n_blk, 0, 0, 0))
    args = [x_tm, mod, h0, conv0_tm, s_all, *w]
    in_specs = [whole(x_tm), per_layer(mod), per_layer(h0), per_layer(conv0_tm), state_spec]
    in_specs += [whole(a) if a.shape[0] == DEPTH and a.ndim == 2 else per_layer(a) for a in w]
    out_specs = [
        pl.BlockSpec((tl, nb, D_MODEL), lambda i: (0, 0, 0)),
        pl.BlockSpec((None, nb, D_A), lambda i: (i // n_blk, 0, 0)),
        pl.BlockSpec((None, CONV_W - 1, nb, D_A), lambda i: (i // n_blk, 0, 0, 0)),
        state_spec,
    ]
    out_shape = [
        jax.ShapeDtypeStruct((tl, nb, D_MODEL), F32),
        jax.ShapeDtypeStruct((DEPTH, nb, D_A), F32),
        jax.ShapeDtypeStruct((DEPTH, CONV_W - 1, nb, D_A), F32),
        jax.ShapeDtypeStruct(s_all.shape, F32),
    ]
    wide = lambda: pltpu.VMEM((rows, D_A), F32)
    slab8 = lambda: pltpu.VMEM((N_HEADS_B, nb * TP, LANES), F32)
    scratch = [
        pltpu.VMEM((tl, nb, D_MODEL), F32),
        pltpu.VMEM((rows, D_MODEL), BF16),
        pltpu.VMEM((rows + (CONV_W - 1) * nb, D_A), F32),
        wide(), wide(), wide(),
        wide(), wide(), wide(), wide(),
        wide(),
        slab8(), slab8(), slab8(), slab8(),
    ]
    return pl.pallas_call(
        functools.partial(_decode_kernel, nb=nb, bt=bt, tl=tl),
        grid=(DEPTH * n_blk,),
        in_specs=in_specs,
        out_specs=out_specs,
        out_shape=out_shape,
        scratch_shapes=scratch,
        compiler_params=pltpu.CompilerParams(dimension_semantics=("arbitrary",), vmem_limit_bytes=VMEM_LIMIT),
        name="decode_layers",
    )(*args)


CT = SUBLANES


def _prompt_kernel(x_ref, mod_ref, prew_ref, postw_ref, win_ref, convw_ref, convb_ref, wg_ref, ba_ref, bx_ref,
                   lam_ref, lbl_ref, onw_ref, wout_ref,
                   y_ref, hn_ref, convn_ref, sn_ref,
                   xt_s, hin_s, xpad_s, ccar_s, ga_s, a_s, u_s, xc_s, q_s, kk_s, b_s, gb_s, bl_s, hcar_s, st_s,
                   xl_s, qe_s, kd_s, v_s, oi_s,
                   *, bt, tl, n_levels):
    o_s = b_s
    m_s = hin_s
    j = pl.program_id(0)
    nj = pl.num_programs(0)
    rows = tl * bt
    ch = CT * bt
    n_ch = tl // CT
    nh = N_HEADS_B
    nt = D_MODEL // LANES
    tail = (CONV_W - 1) * bt

    @pl.when(j == 0)
    def _():
        hcar_s[...] = jnp.zeros_like(hcar_s)
        ccar_s[...] = jnp.zeros_like(ccar_s)
        st_s[...] = jnp.zeros_like(st_s)

    for c in range(n_ch):
        for jl in range(nt):
            for b in range(bt):
                xt_s[jl, pl.ds(c * ch + b, CT, stride=bt), :] = x_ref[b, c * CT:(c + 1) * CT, jl * LANES:(jl + 1) * LANES]

    ti = lax.broadcasted_iota(jnp.int32, (tl, tl), 0)
    si = lax.broadcasted_iota(jnp.int32, (tl, tl), 1)
    masks = []
    for lv in range(n_levels):
        h = CT << lv
        masks.append(((ti // (2 * h)) == (si // (2 * h))) & ((ti // h) % 2 == 1) & ((si // h) % 2 == 0))

    def bcast(v):
        return jnp.concatenate([v] * CT, axis=0)

    for l in range(DEPTH):
        shift, scale, gate = mod_ref[l, 0], mod_ref[l, 1], mod_ref[l, 2]
        pm = prew_ref[l] * (1.0 + scale)
        gp = postw_ref[l] * gate

        for c in range(n_ch):
            rs = pl.ds(c * ch, ch)
            ss = jnp.sum(sum(jnp.square(xt_s[jl, rs, :]) for jl in range(nt)), axis=-1, keepdims=True)
            inv = lax.rsqrt(ss * (1.0 / D_MODEL) + EPS)
            for jl in range(nt):
                ls = slice(jl * LANES, (jl + 1) * LANES)
                hin_s[rs, ls] = (xt_s[jl, rs, :] * inv * bcast(pm[:, ls]) + bcast(shift[:, ls])).astype(BF16)

        def proj(c0, width):
            return _dot(hin_s[...], win_ref[l, :, c0:c0 + width])
        for dst, c0_ in ((q_s, 2 * D_A), (kk_s, 2 * D_A + D_B), (v_s, 2 * D_A + 2 * D_B), (gb_s, 2 * D_A + 3 * D_B)):
            z = proj(c0_, D_B)
            for hd in range(nh):
                dst[hd] = z[:, hd * DK_B:(hd + 1) * DK_B]

        logits = lbl_ref[...]
        pe = jnp.exp(logits - jnp.max(logits, axis=0, keepdims=True))
        probs = pe / jnp.sum(pe, axis=0, keepdims=True)
        lb = jnp.sum(probs[1:l + 1], axis=0, keepdims=True) if l > 0 else jnp.zeros((1, D_B), F32)
        c0 = 0.5 * (1.0 + lb)
        c1 = 0.5 * (1.0 - lb)
        bl = [jnp.zeros((bt, DK_B), F32) for _ in range(nh)]
        for t in range(tl):
            sl = pl.ds(t * bt, bt)
            for hd in range(nh):
                ls = slice(hd * DK_B, (hd + 1) * DK_B)
                p = c1[:, ls] * jnp.tanh(kk_s[hd, sl, :])
                bl[hd] = bl[hd] + jnp.log(jnp.maximum(c0[:, ls] + p, F_MIN))
                b_s[hd, sl, :] = bl[hd]
                kk_s[hd, sl, :] = c1[:, ls] - p
        for hd in range(nh):
            bl_s[hd] = jnp.exp(bl[hd])

        for c in range(n_ch):
            rs = pl.ds(c * ch, ch)
            for hd in range(nh):
                bc, qc, kc, vc = b_s[hd, rs, :], q_s[hd, rs, :], kk_s[hd, rs, :], v_s[hd, rs, :]
                qe_s[hd, rs, :] = qc * jnp.exp(bc)
                kd_s[hd, rs, :] = kc * jnp.exp(bcast(bl[hd]) - bc)
                near = []
                for t in range(CT):
                    ts = slice(t * bt, (t + 1) * bt)
                    acc = jnp.sum(qc[ts] * kc[ts], axis=-1, keepdims=True) * vc[ts]
                    for u in range(t):
                        us = slice(u * bt, (u + 1) * bt)
                        w = qc[ts] * kc[us] * jnp.exp(bc[ts] - bc[us])
                        acc = acc + jnp.sum(w, axis=-1, keepdims=True) * vc[us]
                    near.append(acc)
                oi_s[hd, rs, :] = jnp.concatenate(near, axis=0)
                t0 = c * CT
                for lv in range(n_levels):
                    h = CT << lv
                    mid = (t0 // (2 * h)) * 2 * h + h - 1
                    bmid = bcast(b_s[hd, pl.ds(mid * bt, bt), :])
                    xl_s[lv, hd, rs, :] = qc * jnp.exp(bc - bmid) if t0 % (2 * h) >= h else kc * jnp.exp(bmid - bc)

        xpad_s[pl.ds(0, tail), :] = ccar_s[l]
        xpad_s[pl.ds(tail, rows), :] = proj(0, D_A)
        ga_s[...] = proj(D_A, D_A)
        cw = convw_ref[l]
        for c in range(n_ch):
            xc = convb_ref[l] + sum(xpad_s[pl.ds(c * ch + k * bt, ch), :] * cw[k:k + 1, :] for k in range(CONV_W))
            xc_s[pl.ds(c * ch, ch), :] = xc
        ctail = xpad_s[pl.ds(rows, tail), :]
        ccar_s[l] = ctail
        for c in range(D_A // LANES):
            g2 = _dot(xc_s[:, c * LANES:(c + 1) * LANES].astype(BF16), wg_ref[l, c])
            a_s[:, c * LANES:(c + 1) * LANES] = g2[:, :LANES]
            u_s[:, c * LANES:(c + 1) * LANES] = g2[:, LANES:]
        hcoef = -0.5 * C_RGLRU * jax.nn.softplus(-lam_ref[l])
        ba, bx = ba_ref[l], bx_ref[l]

        def scan_chunk(c, h):
            for tp in range(CT // 2):
                yas = []
                for t2 in range(2):
                    sl = pl.ds(c * ch + (2 * tp + t2) * bt, bt)
                    i = _sig_h(u_s[sl, :] + bx)
                    log_a = hcoef * jnp.tanh(a_s[sl, :] + ba) + hcoef
                    a = jnp.exp(log_a)
                    u = _sqrt_pos(jnp.tanh(log_a) * (-1.0 - a * a)) * (i * xc_s[sl, :])
                    h = a * h + u
                    yas.append(h * _silu_h(ga_s[sl, :]))
                m_s[pl.ds(c * ch + 2 * tp * bt, 2 * bt), :D_A] = jnp.concatenate(yas, axis=0).astype(BF16)
            return h

        def chains(b):
            sl = pl.ds(b, tl, stride=bt)
            hs = range(nh)
            xs = [[xl_s[lv, hd, sl, :].astype(BF16) for lv in range(n_levels)] for hd in hs]
            vs = [v_s[hd, sl, :].astype(BF16) for hd in hs]
            sts = [st_s[l, b, hd] for hd in hs]
            grams = [[_dot_nt(xs[hd][lv], xs[hd][lv]) for lv in range(n_levels)] for hd in hs]
            upd = [_dot_tn(vs[hd], kd_s[hd, sl, :].astype(BF16)) for hd in hs]
            amats = []
            for hd in hs:
                amat = jnp.zeros((tl, tl), F32)
                for lv in range(n_levels):
                    amat = amat + jnp.where(masks[lv], grams[hd][lv], 0.0)
                amats.append(amat.astype(BF16))
            os_ = [_dot(amats[hd], vs[hd]) + _dot_nt(qe_s[hd, sl, :].astype(BF16), sts[hd].astype(BF16)) for hd in hs]
            for hd in hs:
                o_s[hd, sl, :] = (_rms(os_[hd] + oi_s[hd, sl, :], onw_ref[l][:, hd * DV_B:(hd + 1) * DV_B])
                                  * _silu_h(gb_s[hd, sl, :]))
                st_s[l, b, hd] = sts[hd] * bl_s[hd, b:b + 1, :] + upd[hd]

        h_last = hcar_s[l]
        for idx in range(max(bt, n_ch)):
            if idx < bt:
                chains(idx)
            if idx < n_ch:
                h_last = scan_chunk(idx, h_last)
        hcar_s[l] = h_last

        for hd in range(nh):
            m_s[:, D_A + hd * DV_B:D_A + (hd + 1) * DV_B] = o_s[hd].astype(BF16)

        mo = _dot(m_s[...], wout_ref[l])
        for jl in range(nt):
            xl_s[jl // nh, jl % nh] = mo[:, jl * LANES:(jl + 1) * LANES]
        for c in range(n_ch):
            rs = pl.ds(c * ch, ch)
            ss = jnp.sum(sum(jnp.square(xl_s[jl // nh, jl % nh, rs, :]) for jl in range(nt)), axis=-1, keepdims=True)
            inv = lax.rsqrt(ss * (1.0 / D_MODEL) + EPS)
            for jl in range(nt):
                ls = slice(jl * LANES, (jl + 1) * LANES)
                xt_s[jl, rs, :] = xt_s[jl, rs, :] + xl_s[jl // nh, jl % nh, rs, :] * inv * bcast(gp[:, ls])

        @pl.when(j == nj - 1)
        def _():
            hn_ref[l] = h_last
            convn_ref[l] = ctail.reshape(CONV_W - 1, bt, D_A)
            for b in range(bt):
                for hd in range(nh):
                    sn_ref[l, b, hd] = st_s[l, b, hd].T

    for c in range(n_ch):
        for jl in range(nt):
            for b in range(bt):
                y_ref[b, c * CT:(c + 1) * CT, jl * LANES:(jl + 1) * LANES] = xt_s[jl, pl.ds(c * ch + b, CT, stride=bt), :]


def _prompt(x, mod, w, *, tl):
    B, L, _ = x.shape
    bt = B
    assert bt == SUBLANES and L % tl == 0 and tl % CT == 0
    n_levels = (tl // CT).bit_length() - 1
    assert (CT << n_levels) == tl and n_levels * N_HEADS_B >= D_MODEL // LANES
    rows = tl * bt
    tail = (CONV_W - 1) * bt

    def const(shape):
        nd = len(shape)
        return pl.BlockSpec(shape, lambda j: (0,) * nd, pipeline_mode=pl.Buffered(1))
    in_specs = [pl.BlockSpec((bt, tl, D_MODEL), lambda j: (0, j, 0)), const(mod.shape)] + [const(a.shape) for a in w]
    out_specs = [
        pl.BlockSpec((bt, tl, D_MODEL), lambda j: (0, j, 0)),
        pl.BlockSpec((DEPTH, bt, D_A), lambda j: (0, 0, 0)),
        pl.BlockSpec((DEPTH, CONV_W - 1, bt, D_A), lambda j: (0, 0, 0, 0)),
        pl.BlockSpec((DEPTH, bt, N_HEADS_B, DK_B, DV_B), lambda j: (0, 0, 0, 0, 0)),
    ]
    out_shape = [
        jax.ShapeDtypeStruct((B, L, D_MODEL), F32),
        jax.ShapeDtypeStruct((DEPTH, B, D_A), F32),
        jax.ShapeDtypeStruct((DEPTH, CONV_W - 1, B, D_A), F32),
        jax.ShapeDtypeStruct((DEPTH, B, N_HEADS_B, DK_B, DV_B), F32),
    ]
    slab = lambda n: pltpu.VMEM((n, rows, LANES), F32)
    wide = lambda: pltpu.VMEM((rows, D_A), F32)
    scratch = [
        slab(D_MODEL // LANES),
        pltpu.VMEM((rows, D_MODEL), BF16),
        pltpu.VMEM((rows + tail, D_A), F32),
        pltpu.VMEM((DEPTH, tail, D_A), F32),
        wide(), wide(), wide(), wide(),
        slab(N_HEADS_B), slab(N_HEADS_B), slab(N_HEADS_B), slab(N_HEADS_B),
        pltpu.VMEM((N_HEADS_B, bt, DK_B), F32),
        pltpu.VMEM((DEPTH, bt, D_A), F32),
        pltpu.VMEM((DEPTH, bt, N_HEADS_B, DV_B, DK_B), F32),
        pltpu.VMEM((n_levels, N_HEADS_B, rows, LANES), F32),
        slab(N_HEADS_B), slab(N_HEADS_B), slab(N_HEADS_B), slab(N_HEADS_B),
    ]
    body = functools.partial(_prompt_kernel, bt=bt, tl=tl, n_levels=n_levels)
    return pl.pallas_call(
        body,
        grid=(L // tl,),
        in_specs=in_specs,
        out_specs=out_specs,
        out_shape=out_shape,
        scratch_shapes=scratch,
        compiler_params=pltpu.CompilerParams(dimension_semantics=("arbitrary",), vmem_limit_bytes=VMEM_LIMIT),
        name="prompt_layers",
    )(x, mod, *w)


def _gate_weights(wa, wx):
    per = LANES // BLK_A

    def bd(w):
        w = w.reshape(D_A // LANES, per, BLK_A, BLK_A)
        eye = jnp.eye(per, dtype=w.dtype)
        return jnp.einsum('jpcd,pq->jpcqd', w, eye).reshape(D_A // LANES, LANES, LANES)
    return (0.5 * jnp.concatenate([bd(wa), bd(wx)], axis=-1)).astype(BF16)


def kernel(x_prompt, x_sample, state_rglru_h, state_rglru_conv, state_hgrn_S, c_prompt, c_sample, ada_w, ada_b,
           pre_norm_w, post_norm_w, w_in, conv_w, conv_b, rg_wa, rg_ba, rg_wx, rg_bx, rg_lambda, hg_lb_logits,
           hg_onorm_w, w_out):
    bp, bs = x_prompt.shape[0], x_sample.shape[0]
    mod = _modulation(jnp.concatenate([c_prompt, c_sample], axis=0), ada_w,
                      ada_b.reshape(DEPTH, 1, 3 * D_MODEL))
    half = jnp.full((D_A,), 0.5, F32)
    one = jnp.ones((D_A,), F32)
    col_scale = jnp.concatenate([one, half, one, half, one, half])
    w_in_bf, w_out_bf = (w_in * col_scale).astype(BF16), w_out.astype(BF16)
    wg = jnp.stack([_gate_weights(rg_wa[l], rg_wx[l]) for l in range(DEPTH)])
    row = lambda a: a[:, None, :]
    wp = (row(pre_norm_w), row(post_norm_w), w_in_bf, conv_w, row(conv_b), wg, row(0.5 * rg_ba), row(0.5 * rg_bx),
          row(rg_lambda), hg_lb_logits, row(hg_onorm_w), w_out_bf)
    yp, hp, cvp, sp = _prompt(x_prompt, mod[:, :, :bp], wp, tl=64)

    ys, hs, cvs, ss = _decode(jnp.transpose(x_sample, (1, 0, 2)), mod[:, :, bp:], state_rglru_h,
                              jnp.transpose(state_rglru_conv, (0, 2, 1, 3)), state_hgrn_S, wp, bt=16)
    return (yp, jnp.transpose(ys, (1, 0, 2)), hp, jnp.transpose(cvp, (0, 2, 1, 3)), sp,
            hs, jnp.transpose(cvs, (0, 2, 1, 3)), ss)
```

```python
import functools

import jax
import jax.numpy as jnp
from jax import lax
from jax.experimental import pallas as pl
from jax.experimental.pallas import tpu as pltpu

D_MODEL = 1024
DEPTH = 2
D_A = 512
N_BLK_A = 8
BLK_A = 64
CONV_W = 4
C_RGLRU = 8.0
D_B = 512
N_HEADS_B = 4
DK_B = 128
DV_B = 128
P_IN = 2 * D_A + 4 * D_B
EPS = 1e-6
F_MIN = 1e-30

LANES = 128
SUBLANES = 8
VMEM_LIMIT = 60 * 1024 * 1024

BF16 = jnp.bfloat16
F32 = jnp.float32


def _dot(a, b):
    return jnp.dot(a, b, preferred_element_type=F32)


def _dot_nt(a, b):
    return lax.dot_general(a, b, (((1,), (1,)), ((), ())), preferred_element_type=F32)


def _dot_tn(a, b):
    return lax.dot_general(a, b, (((0,), (0,)), ((), ())), preferred_element_type=F32)


def _rms(x, w):
    return x * lax.rsqrt(jnp.mean(x * x, axis=-1, keepdims=True) + EPS) * w


def _silu(x):
    return x * jax.nn.sigmoid(x)


def _mod_kernel(c_ref, w_ref, b_ref, o_ref):
    o_ref[...] = _dot(_silu(c_ref[...]).astype(BF16), w_ref[...].astype(BF16)) + b_ref[...]


def _modulation(c_all, ada_w, ada_b):
    nb = c_all.shape[0]
    return pl.pallas_call(
        _mod_kernel,
        grid=(DEPTH, 3),
        in_specs=[
            pl.BlockSpec((nb, D_MODEL), lambda l, k: (0, 0)),
            pl.BlockSpec((None, D_MODEL, D_MODEL), lambda l, k: (l, 0, k)),
            pl.BlockSpec((None, 1, D_MODEL), lambda l, k: (l, 0, k)),
        ],
        out_specs=pl.BlockSpec((None, None, nb, D_MODEL), lambda l, k: (l, k, 0, 0)),
        out_shape=jax.ShapeDtypeStruct((DEPTH, 3, nb, D_MODEL), F32),
        name="adaln_mod",
    )(c_all, ada_w, ada_b)


TP = SUBLANES


def _sig_h(xh):
    return 0.5 * jnp.tanh(xh) + 0.5


def _silu_h(xh):
    return xh * jnp.tanh(xh) + xh


def _sqrt_pos(y):
    return jnp.where(y > 0.0, y * lax.rsqrt(y), 0.0)


def _decode_kernel(x_ref, mod_ref, h0_ref, conv0_ref, s0_ref, prew_ref, postw_ref, win_ref, convw_ref,
                   convb_ref, wg_ref, ba_ref, bx_ref, lam_ref, lbl_ref, onw_ref, wout_ref,
                   y_ref, hn_ref, convn_ref, sn_ref,
                   xt_s, hin_s, xpad_s, ga_s, a_s, u_s, q_s, kk_s, b_s, gb_s, oi_s, qe8_s, kdx8_s, v8_s, oc8_s,
                   *, nb, bt, tl):
    m_s = hin_s
    i = pl.program_id(0)
    n_blk = nb // bt
    layer = i // n_blk
    blk = i % n_blk
    rows = tl * nb
    tail = (CONV_W - 1) * nb
    nh = N_HEADS_B
    heads = [slice(hd * DK_B, (hd + 1) * DK_B) for hd in range(nh)]

    @pl.when(i == 0)
    def _():
        xt_s[...] = x_ref[...]

    @pl.when(blk == 0)
    def _():
        shift, scale = mod_ref[0], mod_ref[1]
        pm = prew_ref[...] * (1.0 + scale)
        for t in range(tl):
            xt = xt_s[t]
            inv = lax.rsqrt(jnp.mean(xt * xt, axis=-1, keepdims=True) + EPS)
            hin_s[pl.ds(t * nb, nb), :] = (xt * inv * pm + shift).astype(BF16)

        def proj(c0, width):
            return _dot(hin_s[...], win_ref[:, c0:c0 + width])
        xpad_s[pl.ds(0, tail), :] = conv0_ref[...].reshape(tail, D_A)
        xpad_s[pl.ds(tail, rows), :] = proj(0, D_A)
        ga_s[...] = proj(D_A, D_A)
        q_s[...] = proj(2 * D_A, D_B)
        kk_s[...] = proj(2 * D_A + D_B, D_B)
        iv = proj(2 * D_A + 2 * D_B, D_B)
        gb_s[...] = proj(2 * D_A + 3 * D_B, D_B)

        xc = convb_ref[...] + sum(xpad_s[pl.ds(k * nb, rows), :] * convw_ref[k:k + 1, :] for k in range(CONV_W))
        convn_ref[...] = xpad_s[pl.ds(rows, tail), :].reshape(CONV_W - 1, nb, D_A)
        xcb = xc.astype(BF16)
        for c in range(D_A // LANES):
            g2 = _dot(xcb[:, c * LANES:(c + 1) * LANES], wg_ref[c])
            a_s[:, c * LANES:(c + 1) * LANES] = g2[:, :LANES]
            u_s[:, c * LANES:(c + 1) * LANES] = g2[:, LANES:]
        hcoef = -0.5 * C_RGLRU * jax.nn.softplus(-lam_ref[...])
        h = h0_ref[...]
        for t in range(tl):
            sl = pl.ds(t * nb, nb)
            ig = _sig_h(u_s[sl, :] + bx_ref[...])
            log_a = hcoef * jnp.tanh(a_s[sl, :] + ba_ref[...]) + hcoef
            a = jnp.exp(log_a)
            u = _sqrt_pos(jnp.tanh(log_a) * (-1.0 - a * a)) * (ig * xc[t * nb:(t + 1) * nb])
            h = a * h + u
            m_s[sl, :D_A] = (h * _silu_h(ga_s[sl, :])).astype(BF16)
        hn_ref[...] = h

        logits = lbl_ref[...]
        pe = jnp.exp(logits - jnp.max(logits, axis=0, keepdims=True))
        probs = pe / jnp.sum(pe, axis=0, keepdims=True)
        lb = jnp.zeros((1, D_B), F32)
        for k in range(1, DEPTH):
            lb = lb + jnp.where(layer >= k, probs[k:k + 1], 0.0)
        c0 = 0.5 * (1.0 + lb)
        c1 = 0.5 * (1.0 - lb)
        bl = jnp.zeros((nb, D_B), F32)
        for t in range(tl):
            sl = pl.ds(t * nb, nb)
            p = c1 * jnp.tanh(kk_s[sl, :])
            bl = bl + jnp.log(jnp.maximum(c0 + p, F_MIN))
            b_s[sl, :] = bl
            kk_s[sl, :] = c1 - p

        for t in range(tl):
            st_ = pl.ds(t * nb, nb)
            acc = [jnp.zeros((nb, DV_B), F32) for _ in range(nh)]
            for s in range(t + 1):
                ss_ = pl.ds(s * nb, nb)
                w = q_s[st_, :] * kk_s[ss_, :]
                if s < t:
                    w = w * jnp.exp(b_s[st_, :] - b_s[ss_, :])
                for hd in range(nh):
                    acc[hd] = acc[hd] + jnp.sum(w[:, heads[hd]], axis=-1, keepdims=True) * iv[s * nb:(s + 1) * nb, heads[hd]]
            for hd in range(nh):
                oi_s[st_, heads[hd]] = acc[hd]

        zrow = jnp.zeros((nb, LANES), F32)
        ebl = jnp.exp(bl)
        e_hi = ebl.astype(BF16).astype(F32)
        e_mid = (ebl - e_hi).astype(BF16).astype(F32)
        e_lo = ebl - e_hi - e_mid
        for hd in range(nh):
            for t in range(TP):
                dst = pl.ds(t, nb, stride=TP)
                if t < tl:
                    st_ = pl.ds(t * nb, nb)
                    bt_ = b_s[st_, heads[hd]]
                    qe8_s[hd, dst, :] = q_s[st_, heads[hd]] * jnp.exp(bt_)
                    kdx8_s[hd, dst, :] = kk_s[st_, heads[hd]] * jnp.exp(bl[:, heads[hd]] - bt_)
                    v8_s[hd, dst, :] = iv[t * nb:(t + 1) * nb, heads[hd]]
                else:
                    qe8_s[hd, dst, :] = zrow
                    v8_s[hd, dst, :] = zrow
                    dec = (e_hi, e_mid, e_lo)
                    kdx8_s[hd, dst, :] = dec[t - tl][:, heads[hd]] if t - tl < len(dec) else zrow

    rid = lax.broadcasted_iota(jnp.int32, (TP, LANES), 0)
    ones_rows = jnp.where((rid >= tl) & (rid < tl + 3), 1.0, 0.0)
    for bl_ in range(bt):
        r8 = pl.ds(pl.multiple_of((blk * bt + bl_) * TP, TP), TP)
        for hd in range(nh):
            s_old = s0_ref[bl_, hd]
            oc8_s[hd, r8, :] = _dot(qe8_s[hd, r8, :].astype(BF16), s_old.astype(BF16))
            rhs = jnp.concatenate([v8_s[hd, r8, :], ones_rows], axis=1).astype(BF16)
            g = _dot_tn(kdx8_s[hd, r8, :].astype(BF16), rhs)
            sn_ref[bl_, hd] = s_old * g[:, DV_B:] + g[:, :DV_B]

    @pl.when(blk == n_blk - 1)
    def _():
        for t in range(tl):
            st_ = pl.ds(t * nb, nb)
            for hd in range(nh):
                oh = oi_s[st_, heads[hd]] + oc8_s[hd, pl.ds(t, nb, stride=TP), :]
                yb = _rms(oh, onw_ref[:, heads[hd]]) * _silu_h(gb_s[st_, heads[hd]])
                m_s[st_, D_A + hd * DV_B:D_A + (hd + 1) * DV_B] = yb.astype(BF16)
        mo = _dot(m_s[...], wout_ref[...])
        gp = postw_ref[...] * mod_ref[2]
        for t in range(tl):
            mt = mo[t * nb:(t + 1) * nb]
            inv = lax.rsqrt(jnp.mean(mt * mt, axis=-1, keepdims=True) + EPS)
            xt_s[t] = xt_s[t] + mt * inv * gp

    @pl.when(i == pl.num_programs(0) - 1)
    def _():
        y_ref[...] = xt_s[...]


def _decode(x_tm, mod, h0, conv0_tm, s_all, w, *, bt):
    tl, nb, _ = x_tm.shape
    assert nb % bt == 0 and nb % SUBLANES == 0 and CONV_W - 1 <= tl and tl + 3 <= TP
    rows = tl * nb
    n_blk = nb // bt

    def whole(a):
        nd = a.ndim
        return pl.BlockSpec(a.shape, lambda i: (0,) * nd, pipeline_mode=pl.Buffered(1))

    def per_layer(a):
        nd = a.ndim - 1
        return pl.BlockSpec((None,) + a.shape[1:], lambda i: (i // n_blk,) + (0,) * nd, pipeline_mode=pl.Buffered(1))
    state_spec = pl.BlockSpec((None, bt, N_HEADS_B, DK_B, DV_B), lambda i: (i // n_blk, i % n_blk, 0, 0, 0))
    args = [x_tm, mod, h0, conv0_tm, s_all, *w]
    in_specs = [whole(x_tm), per_layer(mod), per_layer(h0), per_layer(conv0_tm), state_spec]
    in_specs += [whole(a) if a.shape[0] == DEPTH and a.ndim == 2 else per_layer(a) for a in w]
    out_specs = [
        pl.BlockSpec((tl, nb, D_MODEL), lambda i: (0, 0, 0)),
        pl.BlockSpec((None, nb, D_A), lambda i: (i // n_blk, 0, 0)),
        pl.BlockSpec((None, CONV_W - 1, nb, D_A), lambda i: (i // n_blk, 0, 0, 0)),
        state_spec,
    ]
    out_shape = [
        jax.ShapeDtypeStruct((tl, nb, D_MODEL), F32),
        jax.ShapeDtypeStruct((DEPTH, nb, D_A), F32),
        jax.ShapeDtypeStruct((DEPTH, CONV_W - 1, nb, D_A), F32),
        jax.ShapeDtypeStruct(s_all.shape, F32),
    ]
    wide = lambda: pltpu.VMEM((rows, D_A), F32)
    slab8 = lambda: pltpu.VMEM((N_HEADS_B, nb * TP, LANES), F32)
    scratch = [
        pltpu.VMEM((tl, nb, D_MODEL), F32),
        pltpu.VMEM((rows, D_MODEL), BF16),
        pltpu.VMEM((rows + (CONV_W - 1) * nb, D_A), F32),
        wide(), wide(), wide(),
        wide(), wide(), wide(), wide(),
        wide(),
        slab8(), slab8(), slab8(), slab8(),
    ]
    return pl.pallas_call(
        functools.partial(_decode_kernel, nb=nb, bt=bt, tl=tl),
        grid=(DEPTH * n_blk,),
        in_specs=in_specs,
        out_specs=out_specs,
        out_shape=out_shape,
        scratch_shapes=scratch,
        compiler_params=pltpu.CompilerParams(dimension_semantics=("arbitrary",), vmem_limit_bytes=VMEM_LIMIT),
        name="decode_layers",
    )(*args)


CT = SUBLANES


def _prompt_kernel(x_ref, mod_ref, prew_ref, postw_ref, win_ref, convw_ref, convb_ref, wg_ref, ba_ref, bx_ref,
                   lam_ref, lbl_ref, onw_ref, wout_ref,
                   y_ref, hn_ref, convn_ref, sn_ref,
                   xio_s, in_sem, out_sem, hin_s, xpad_s, ccar_s, ga_s, a_s, u_s, xc_s, q_s, kk_s, b_s, gb_s, bl_s, hcar_s, st_s,
                   xl_s, qe_s, kd_s, v_s, oi_s,
                   *, bt, tl, n_levels):
    o_s = b_s
    m_s = hin_s
    j = pl.program_id(0)
    nj = pl.num_programs(0)
    rows = tl * bt
    ch = CT * bt
    n_ch = tl // CT
    nh = N_HEADS_B
    nt = D_MODEL // LANES
    tail = (CONV_W - 1) * bt

    slot = j % 2
    nslot = 1 - slot

    def in_copy(tile, sl_, b):
        return pltpu.make_async_copy(x_ref.at[b, pl.ds(tile * tl, tl), :], xio_s.at[sl_, :, b, :], in_sem.at[sl_])

    def out_copy(tile, sl_, b):
        return pltpu.make_async_copy(xio_s.at[sl_, :, b, :], y_ref.at[b, pl.ds(tile * tl, tl), :], out_sem.at[sl_])

    @pl.when(j == 0)
    def _():
        hcar_s[...] = jnp.zeros_like(hcar_s)
        ccar_s[...] = jnp.zeros_like(ccar_s)
        st_s[...] = jnp.zeros_like(st_s)
        for b in range(bt):
            in_copy(0, 0, b).start()

    for b in range(bt):
        in_copy(j, slot, b).wait()

    @pl.when(j + 1 < nj)
    def _():
        @pl.when(j >= 1)
        def _():
            for b in range(bt):
                out_copy(j - 1, nslot, b).wait()
        for b in range(bt):
            in_copy(j + 1, nslot, b).start()

    xt = xio_s.at[slot]

    def xt_get(c, jl):
        return xt[c * CT:(c + 1) * CT, :, jl * LANES:(jl + 1) * LANES].reshape(ch, LANES)

    def xt_set(c, jl, v):
        xt[c * CT:(c + 1) * CT, :, jl * LANES:(jl + 1) * LANES] = v.reshape(CT, bt, LANES)

    ti = lax.broadcasted_iota(jnp.int32, (tl, tl), 0)
    si = lax.broadcasted_iota(jnp.int32, (tl, tl), 1)
    masks = []
    for lv in range(n_levels):
        h = CT << lv
        masks.append(((ti // (2 * h)) == (si // (2 * h))) & ((ti // h) % 2 == 1) & ((si // h) % 2 == 0))

    def bcast(v):
        return jnp.concatenate([v] * CT, axis=0)

    for l in range(DEPTH):
        shift, scale, gate = mod_ref[l, 0], mod_ref[l, 1], mod_ref[l, 2]
        pm = prew_ref[l] * (1.0 + scale)
        gp = postw_ref[l] * gate

        for c in range(n_ch):
            rs = pl.ds(c * ch, ch)
            ss = jnp.sum(sum(jnp.square(xt_get(c, jl)) for jl in range(nt)), axis=-1, keepdims=True)
            inv = lax.rsqrt(ss * (1.0 / D_MODEL) + EPS)
            for jl in range(nt):
                ls = slice(jl * LANES, (jl + 1) * LANES)
                hin_s[rs, ls] = (xt_get(c, jl) * inv * bcast(pm[:, ls]) + bcast(shift[:, ls])).astype(BF16)

        def proj(c0, width):
            return _dot(hin_s[...], win_ref[l, :, c0:c0 + width])
        for dst, c0_ in ((q_s, 2 * D_A), (kk_s, 2 * D_A + D_B), (v_s, 2 * D_A + 2 * D_B), (gb_s, 2 * D_A + 3 * D_B)):
            z = proj(c0_, D_B)
            for hd in range(nh):
                dst[hd] = z[:, hd * DK_B:(hd + 1) * DK_B]

        logits = lbl_ref[...]
        pe = jnp.exp(logits - jnp.max(logits, axis=0, keepdims=True))
        probs = pe / jnp.sum(pe, axis=0, keepdims=True)
        lb = jnp.sum(probs[1:l + 1], axis=0, keepdims=True) if l > 0 else jnp.zeros((1, D_B), F32)
        c0 = 0.5 * (1.0 + lb)
        c1 = 0.5 * (1.0 - lb)
        bl = [jnp.zeros((bt, DK_B), F32) for _ in range(nh)]
        for t in range(tl):
            sl = pl.ds(t * bt, bt)
            for hd in range(nh):
                ls = slice(hd * DK_B, (hd + 1) * DK_B)
                p = c1[:, ls] * jnp.tanh(kk_s[hd, sl, :])
                bl[hd] = bl[hd] + jnp.log(jnp.maximum(c0[:, ls] + p, F_MIN))
                b_s[hd, sl, :] = bl[hd]
                kk_s[hd, sl, :] = c1[:, ls] - p
        for hd in range(nh):
            bl_s[hd] = jnp.exp(bl[hd])

        for c in range(n_ch):
            rs = pl.ds(c * ch, ch)
            for hd in range(nh):
                bc, qc, kc, vc = b_s[hd, rs, :], q_s[hd, rs, :], kk_s[hd, rs, :], v_s[hd, rs, :]
                qe_s[hd, rs, :] = qc * jnp.exp(bc)
                kd_s[hd, rs, :] = kc * jnp.exp(bcast(bl[hd]) - bc)
                near = []
                for t in range(CT):
                    ts = slice(t * bt, (t + 1) * bt)
                    acc = jnp.sum(qc[ts] * kc[ts], axis=-1, keepdims=True) * vc[ts]
                    for u in range(t):
                        us = slice(u * bt, (u + 1) * bt)
                        w = qc[ts] * kc[us] * jnp.exp(bc[ts] - bc[us])
                        acc = acc + jnp.sum(w, axis=-1, keepdims=True) * vc[us]
                    near.append(acc)
                oi_s[hd, rs, :] = jnp.concatenate(near, axis=0)
                t0 = c * CT
                for lv in range(n_levels):
                    h = CT << lv
                    mid = (t0 // (2 * h)) * 2 * h + h - 1
                    bmid = bcast(b_s[hd, pl.ds(mid * bt, bt), :])
                    xl_s[lv, hd, rs, :] = qc * jnp.exp(bc - bmid) if t0 % (2 * h) >= h else kc * jnp.exp(bmid - bc)

        xpad_s[pl.ds(0, tail), :] = ccar_s[l]
        xpad_s[pl.ds(tail, rows), :] = proj(0, D_A)
        ga_s[...] = proj(D_A, D_A)
        cw = convw_ref[l]
        for c in range(n_ch):
            xc = convb_ref[l] + sum(xpad_s[pl.ds(c * ch + k * bt, ch), :] * cw[k:k + 1, :] for k in range(CONV_W))
            xc_s[pl.ds(c * ch, ch), :] = xc
        ctail = xpad_s[pl.ds(rows, tail), :]
        ccar_s[l] = ctail
        for c in range(D_A // LANES):
            g2 = _dot(xc_s[:, c * LANES:(c + 1) * LANES].astype(BF16), wg_ref[l, c])
            a_s[:, c * LANES:(c + 1) * LANES] = g2[:, :LANES]
            u_s[:, c * LANES:(c + 1) * LANES] = g2[:, LANES:]
        hcoef = -0.5 * C_RGLRU * jax.nn.softplus(-lam_ref[l])
        ba, bx = ba_ref[l], bx_ref[l]

        def scan_chunk(c, h):
            for tp in range(CT // 2):
                yas = []
                for t2 in range(2):
                    sl = pl.ds(c * ch + (2 * tp + t2) * bt, bt)
                    i = _sig_h(u_s[sl, :] + bx)
                    log_a = hcoef * jnp.tanh(a_s[sl, :] + ba) + hcoef
                    a = jnp.exp(log_a)
                    u = _sqrt_pos(jnp.tanh(log_a) * (-1.0 - a * a)) * (i * xc_s[sl, :])
                    h = a * h + u
                    yas.append(h * _silu_h(ga_s[sl, :]))
                m_s[pl.ds(c * ch + 2 * tp * bt, 2 * bt), :D_A] = jnp.concatenate(yas, axis=0).astype(BF16)
            return h

        def chains(b):
            sl = pl.ds(b, tl, stride=bt)
            hs = range(nh)
            xs = [[xl_s[lv, hd, sl, :].astype(BF16) for lv in range(n_levels)] for hd in hs]
            vs = [v_s[hd, sl, :].astype(BF16) for hd in hs]
            sts = [st_s[l, b, hd] for hd in hs]
            grams = [[_dot_nt(xs[hd][lv], xs[hd][lv]) for lv in range(n_levels)] for hd in hs]
            upd = [_dot_tn(vs[hd], kd_s[hd, sl, :].astype(BF16)) for hd in hs]
            amats = []
            for hd in hs:
                amat = jnp.zeros((tl, tl), F32)
                for lv in range(n_levels):
                    amat = amat + jnp.where(masks[lv], grams[hd][lv], 0.0)
                amats.append(amat.astype(BF16))
            os_ = [_dot(amats[hd], vs[hd]) + _dot_nt(qe_s[hd, sl, :].astype(BF16), sts[hd].astype(BF16)) for hd in hs]
            for hd in hs:
                o_s[hd, sl, :] = (_rms(os_[hd] + oi_s[hd, sl, :], onw_ref[l][:, hd * DV_B:(hd + 1) * DV_B])
                                  * _silu_h(gb_s[hd, sl, :]))
                st_s[l, b, hd] = sts[hd] * bl_s[hd, b:b + 1, :] + upd[hd]

        h_last = hcar_s[l]
        for idx in range(max(bt, n_ch)):
            if idx < bt:
                chains(idx)
            if idx < n_ch:
                h_last = scan_chunk(idx, h_last)
        hcar_s[l] = h_last

        for hd in range(nh):
            m_s[:, D_A + hd * DV_B:D_A + (hd + 1) * DV_B] = o_s[hd].astype(BF16)

        mo = _dot(m_s[...], wout_ref[l])
        for jl in range(nt):
            xl_s[jl // nh, jl % nh] = mo[:, jl * LANES:(jl + 1) * LANES]
        for c in range(n_ch):
            rs = pl.ds(c * ch, ch)
            ss = jnp.sum(sum(jnp.square(xl_s[jl // nh, jl % nh, rs, :]) for jl in range(nt)), axis=-1, keepdims=True)
            inv = lax.rsqrt(ss * (1.0 / D_MODEL) + EPS)
            for jl in range(nt):
                ls = slice(jl * LANES, (jl + 1) * LANES)
                xt_set(c, jl, xt_get(c, jl) + xl_s[jl // nh, jl % nh, rs, :] * inv * bcast(gp[:, ls]))

        @pl.when(j == nj - 1)
        def _():
            hn_ref[l] = h_last
            convn_ref[l] = ctail.reshape(CONV_W - 1, bt, D_A)
            for b in range(bt):
                for hd in range(nh):
                    sn_ref[l, b, hd] = st_s[l, b, hd].T

    for b in range(bt):
        out_copy(j, slot, b).start()

    @pl.when(j == nj - 1)
    def _():
        for b in range(bt):
            out_copy(j, slot, b).wait()

        @pl.when(j >= 1)
        def _():
            for b in range(bt):
                out_copy(j - 1, nslot, b).wait()


def _prompt(x, mod, w, *, tl):
    B, L, _ = x.shape
    bt = B
    assert bt == SUBLANES and L % tl == 0 and tl % CT == 0
    n_levels = (tl // CT).bit_length() - 1
    assert (CT << n_levels) == tl and n_levels * N_HEADS_B >= D_MODEL // LANES
    rows = tl * bt
    tail = (CONV_W - 1) * bt

    def const(shape):
        nd = len(shape)
        return pl.BlockSpec(shape, lambda j: (0,) * nd, pipeline_mode=pl.Buffered(1))
    in_specs = [pl.BlockSpec(memory_space=pl.ANY), const(mod.shape)] + [const(a.shape) for a in w]
    out_specs = [
        pl.BlockSpec(memory_space=pl.ANY),
        pl.BlockSpec((DEPTH, bt, D_A), lambda j: (0, 0, 0)),
        pl.BlockSpec((DEPTH, CONV_W - 1, bt, D_A), lambda j: (0, 0, 0, 0)),
        pl.BlockSpec((DEPTH, bt, N_HEADS_B, DK_B, DV_B), lambda j: (0, 0, 0, 0, 0)),
    ]
    out_shape = [
        jax.ShapeDtypeStruct((B, L, D_MODEL), F32),
        jax.ShapeDtypeStruct((DEPTH, B, D_A), F32),
        jax.ShapeDtypeStruct((DEPTH, CONV_W - 1, B, D_A), F32),
        jax.ShapeDtypeStruct((DEPTH, B, N_HEADS_B, DK_B, DV_B), F32),
    ]
    slab = lambda n: pltpu.VMEM((n, rows, LANES), F32)
    wide = lambda: pltpu.VMEM((rows, D_A), F32)
    scratch = [
        pltpu.VMEM((2, tl, bt, D_MODEL), F32),
        pltpu.SemaphoreType.DMA((2,)),
        pltpu.SemaphoreType.DMA((2,)),
        pltpu.VMEM((rows, D_MODEL), BF16),
        pltpu.VMEM((rows + tail, D_A), F32),
        pltpu.VMEM((DEPTH, tail, D_A), F32),
        wide(), wide(), wide(), wide(),
        slab(N_HEADS_B), slab(N_HEADS_B), slab(N_HEADS_B), slab(N_HEADS_B),
        pltpu.VMEM((N_HEADS_B, bt, DK_B), F32),
        pltpu.VMEM((DEPTH, bt, D_A), F32),
        pltpu.VMEM((DEPTH, bt, N_HEADS_B, DV_B, DK_B), F32),
        pltpu.VMEM((n_levels, N_HEADS_B, rows, LANES), F32),
        slab(N_HEADS_B), slab(N_HEADS_B), slab(N_HEADS_B), slab(N_HEADS_B),
    ]
    body = functools.partial(_prompt_kernel, bt=bt, tl=tl, n_levels=n_levels)
    return pl.pallas_call(
        body,
        grid=(L // tl,),
        in_specs=in_specs,
        out_specs=out_specs,
        out_shape=out_shape,
        scratch_shapes=scratch,
        compiler_params=pltpu.CompilerParams(dimension_semantics=("arbitrary",), vmem_limit_bytes=VMEM_LIMIT),
        name="prompt_layers",
    )(x, mod, *w)


def _gate_weights(wa, wx):
    per = LANES // BLK_A

    def bd(w):
        w = w.reshape(D_A // LANES, per, BLK_A, BLK_A)
        eye = jnp.eye(per, dtype=w.dtype)
        return jnp.einsum('jpcd,pq->jpcqd', w, eye).reshape(D_A // LANES, LANES, LANES)
    return (0.5 * jnp.concatenate([bd(wa), bd(wx)], axis=-1)).astype(BF16)


def kernel(x_prompt, x_sample, state_rglru_h, state_rglru_conv, state_hgrn_S, c_prompt, c_sample, ada_w, ada_b,
           pre_norm_w, post_norm_w, w_in, conv_w, conv_b, rg_wa, rg_ba, rg_wx, rg_bx, rg_lambda, hg_lb_logits,
           hg_onorm_w, w_out):
    bp, bs = x_prompt.shape[0], x_sample.shape[0]
    mod = _modulation(jnp.concatenate([c_prompt, c_sample], axis=0), ada_w,
                      ada_b.reshape(DEPTH, 1, 3 * D_MODEL))
    half = jnp.full((D_A,), 0.5, F32)
    one = jnp.ones((D_A,), F32)
    col_scale = jnp.concatenate([one, half, one, half, one, half])
    w_in_bf, w_out_bf = (w_in * col_scale).astype(BF16), w_out.astype(BF16)
    wg = jnp.stack([_gate_weights(rg_wa[l], rg_wx[l]) for l in range(DEPTH)])
    row = lambda a: a[:, None, :]
    wp = (row(pre_norm_w), row(post_norm_w), w_in_bf, conv_w, row(conv_b), wg, row(0.5 * rg_ba), row(0.5 * rg_bx),
          row(rg_lambda), hg_lb_logits, row(hg_onorm_w), w_out_bf)
    yp, hp, cvp, sp = _prompt(x_prompt, mod[:, :, :bp], wp, tl=64)

    ys, hs, cvs, ss = _decode(jnp.transpose(x_sample, (1, 0, 2)), mod[:, :, bp:], state_rglru_h,
                              jnp.transpose(state_rglru_conv, (0, 2, 1, 3)), state_hgrn_S, wp, bt=16)
    return (yp, jnp.transpose(ys, (1, 0, 2)), hp, jnp.transpose(cvp, (0, 2, 1, 3)), sp,
            hs, jnp.transpose(cvs, (0, 2, 1, 3)), ss)
```

```python
import functools

import jax
import jax.numpy as jnp
from jax import lax
from jax.experimental import pallas as pl
from jax.experimental.pallas import tpu as pltpu

D_MODEL = 1024
DEPTH = 2
D_A = 512
N_BLK_A = 8
BLK_A = 64
CONV_W = 4
C_RGLRU = 8.0
D_B = 512
N_HEADS_B = 4
DK_B = 128
DV_B = 128
P_IN = 2 * D_A + 4 * D_B
EPS = 1e-6
F_MIN = 1e-30

LANES = 128
SUBLANES = 8
VMEM_LIMIT = 60 * 1024 * 1024

BF16 = jnp.bfloat16
F32 = jnp.float32


def _dot(a, b):
    return jnp.dot(a, b, preferred_element_type=F32)


def _dot_nt(a, b):
    return lax.dot_general(a, b, (((1,), (1,)), ((), ())), preferred_element_type=F32)


def _dot_tn(a, b):
    return lax.dot_general(a, b, (((0,), (0,)), ((), ())), preferred_element_type=F32)


def _rms(x, w):
    return x * lax.rsqrt(jnp.mean(x * x, axis=-1, keepdims=True) + EPS) * w


def _silu(x):
    return x * jax.nn.sigmoid(x)


def _mod_kernel(c_ref, w_ref, b_ref, o_ref):
    s = _silu(c_ref[...]).astype(BF16)
    for k in range(3):
        cs = slice(k * D_MODEL, (k + 1) * D_MODEL)
        o_ref[k] = _dot(s, w_ref[:, cs].astype(BF16)) + b_ref[:, cs]


def _modulation(c_all, ada_w, ada_b):
    nb = c_all.shape[0]
    return pl.pallas_call(
        _mod_kernel,
        grid=(DEPTH,),
        in_specs=[
            pl.BlockSpec((nb, D_MODEL), lambda l: (0, 0)),
            pl.BlockSpec((None, D_MODEL, 3 * D_MODEL), lambda l: (l, 0, 0)),
            pl.BlockSpec((None, 1, 3 * D_MODEL), lambda l: (l, 0, 0)),
        ],
        out_specs=pl.BlockSpec((None, 3, nb, D_MODEL), lambda l: (l, 0, 0, 0)),
        out_shape=jax.ShapeDtypeStruct((DEPTH, 3, nb, D_MODEL), F32),
        compiler_params=pltpu.CompilerParams(vmem_limit_bytes=VMEM_LIMIT),
        name="adaln_mod",
    )(c_all, ada_w, ada_b)


TP = SUBLANES


def _sig_h(xh):
    return 0.5 * jnp.tanh(xh) + 0.5


def _silu_h(xh):
    return xh * jnp.tanh(xh) + xh


def _sqrt_pos(y):
    return jnp.where(y > 0.0, y * lax.rsqrt(y), 0.0)


def _decode_kernel(x_ref, mod_ref, h0_ref, conv0_ref, s0_ref, prew_ref, postw_ref, win_ref, convw_ref,
                   convb_ref, wg_ref, ba_ref, bx_ref, lam_ref, lbl_ref, onw_ref, wout_ref,
                   y_ref, hn_ref, convn_ref, sn_ref,
                   xt_s, hin_s, xpad_s, ga_s, a_s, u_s, q_s, kk_s, b_s, gb_s, oi_s, qe8_s, kdx8_s, v8_s, oc8_s,
                   *, nb, bt, tl):
    m_s = hin_s
    i = pl.program_id(0)
    n_blk = nb // bt
    layer = i // n_blk
    blk = i % n_blk
    rows = tl * nb
    tail = (CONV_W - 1) * nb
    nh = N_HEADS_B
    heads = [slice(hd * DK_B, (hd + 1) * DK_B) for hd in range(nh)]

    @pl.when(i == 0)
    def _():
        xt_s[...] = x_ref[...]

    @pl.when(blk == 0)
    def _():
        shift, scale = mod_ref[0], mod_ref[1]
        pm = prew_ref[...] * (1.0 + scale)
        for t in range(tl):
            xt = xt_s[t]
            inv = lax.rsqrt(jnp.mean(xt * xt, axis=-1, keepdims=True) + EPS)
            hin_s[pl.ds(t * nb, nb), :] = (xt * inv * pm + shift).astype(BF16)

        def proj(c0, width):
            return _dot(hin_s[...], win_ref[:, c0:c0 + width])
        xpad_s[pl.ds(0, tail), :] = conv0_ref[...].reshape(tail, D_A)
        xpad_s[pl.ds(tail, rows), :] = proj(0, D_A)
        ga_s[...] = proj(D_A, D_A)
        q_s[...] = proj(2 * D_A, D_B)
        kk_s[...] = proj(2 * D_A + D_B, D_B)
        iv = proj(2 * D_A + 2 * D_B, D_B)
        gb_s[...] = proj(2 * D_A + 3 * D_B, D_B)

        xc = convb_ref[...] + sum(xpad_s[pl.ds(k * nb, rows), :] * convw_ref[k:k + 1, :] for k in range(CONV_W))
        convn_ref[...] = xpad_s[pl.ds(rows, tail), :].reshape(CONV_W - 1, nb, D_A)
        xcb = xc.astype(BF16)
        for c in range(D_A // LANES):
            g2 = _dot(xcb[:, c * LANES:(c + 1) * LANES], wg_ref[c])
            a_s[:, c * LANES:(c + 1) * LANES] = g2[:, :LANES]
            u_s[:, c * LANES:(c + 1) * LANES] = g2[:, LANES:]
        hcoef = -0.5 * C_RGLRU * jax.nn.softplus(-lam_ref[...])
        h = h0_ref[...]
        for t in range(tl):
            sl = pl.ds(t * nb, nb)
            ig = _sig_h(u_s[sl, :] + bx_ref[...])
            log_a = hcoef * jnp.tanh(a_s[sl, :] + ba_ref[...]) + hcoef
            a = jnp.exp(log_a)
            u = _sqrt_pos(jnp.tanh(log_a) * (-1.0 - a * a)) * (ig * xc[t * nb:(t + 1) * nb])
            h = a * h + u
            m_s[sl, :D_A] = (h * _silu_h(ga_s[sl, :])).astype(BF16)
        hn_ref[...] = h

        logits = lbl_ref[...]
        pe = jnp.exp(logits - jnp.max(logits, axis=0, keepdims=True))
        probs = pe / jnp.sum(pe, axis=0, keepdims=True)
        lb = jnp.zeros((1, D_B), F32)
        for k in range(1, DEPTH):
            lb = lb + jnp.where(layer >= k, probs[k:k + 1], 0.0)
        c0 = 0.5 * (1.0 + lb)
        c1 = 0.5 * (1.0 - lb)
        bl = jnp.zeros((nb, D_B), F32)
        for t in range(tl):
            sl = pl.ds(t * nb, nb)
            p = c1 * jnp.tanh(kk_s[sl, :])
            bl = bl + jnp.log(jnp.maximum(c0 + p, F_MIN))
            b_s[sl, :] = bl
            kk_s[sl, :] = c1 - p

        for t in range(tl):
            st_ = pl.ds(t * nb, nb)
            acc = [jnp.zeros((nb, DV_B), F32) for _ in range(nh)]
            for s in range(t + 1):
                ss_ = pl.ds(s * nb, nb)
                w = q_s[st_, :] * kk_s[ss_, :]
                if s < t:
                    w = w * jnp.exp(b_s[st_, :] - b_s[ss_, :])
                for hd in range(nh):
                    acc[hd] = acc[hd] + jnp.sum(w[:, heads[hd]], axis=-1, keepdims=True) * iv[s * nb:(s + 1) * nb, heads[hd]]
            for hd in range(nh):
                oi_s[st_, heads[hd]] = acc[hd]

        zrow = jnp.zeros((nb, LANES), F32)
        ebl = jnp.exp(bl)
        e_hi = ebl.astype(BF16).astype(F32)
        e_mid = (ebl - e_hi).astype(BF16).astype(F32)
        e_lo = ebl - e_hi - e_mid
        for hd in range(nh):
            for t in range(TP):
                dst = pl.ds(t, nb, stride=TP)
                if t < tl:
                    st_ = pl.ds(t * nb, nb)
                    bt_ = b_s[st_, heads[hd]]
                    qe8_s[hd, dst, :] = q_s[st_, heads[hd]] * jnp.exp(bt_)
                    kdx8_s[hd, dst, :] = kk_s[st_, heads[hd]] * jnp.exp(bl[:, heads[hd]] - bt_)
                    v8_s[hd, dst, :] = iv[t * nb:(t + 1) * nb, heads[hd]]
                else:
                    qe8_s[hd, dst, :] = zrow
                    v8_s[hd, dst, :] = zrow
                    dec = (e_hi, e_mid, e_lo)
                    kdx8_s[hd, dst, :] = dec[t - tl][:, heads[hd]] if t - tl < len(dec) else zrow

    rid = lax.broadcasted_iota(jnp.int32, (TP, LANES), 0)
    ones_rows = jnp.where((rid >= tl) & (rid < tl + 3), 1.0, 0.0)
    for bl_ in range(bt):
        r8 = pl.ds(pl.multiple_of((blk * bt + bl_) * TP, TP), TP)
        for hd in range(nh):
            s_old = s0_ref[bl_, hd]
            oc8_s[hd, r8, :] = _dot(qe8_s[hd, r8, :].astype(BF16), s_old.astype(BF16))
            rhs = jnp.concatenate([v8_s[hd, r8, :], ones_rows], axis=1).astype(BF16)
            g = _dot_tn(kdx8_s[hd, r8, :].astype(BF16), rhs)
            sn_ref[bl_, hd] = s_old * g[:, DV_B:] + g[:, :DV_B]

    @pl.when(blk == n_blk - 1)
    def _():
        for t in range(tl):
            st_ = pl.ds(t * nb, nb)
            for hd in range(nh):
                oh = oi_s[st_, heads[hd]] + oc8_s[hd, pl.ds(t, nb, stride=TP), :]
                yb = _rms(oh, onw_ref[:, heads[hd]]) * _silu_h(gb_s[st_, heads[hd]])
                m_s[st_, D_A + hd * DV_B:D_A + (hd + 1) * DV_B] = yb.astype(BF16)
        mo = _dot(m_s[...], wout_ref[...])
        gp = postw_ref[...] * mod_ref[2]
        for t in range(tl):
            mt = mo[t * nb:(t + 1) * nb]
            inv = lax.rsqrt(jnp.mean(mt * mt, axis=-1, keepdims=True) + EPS)
            xt_s[t] = xt_s[t] + mt * inv * gp

    @pl.when(i == pl.num_programs(0) - 1)
    def _():
        y_ref[...] = xt_s[...]


def _decode(x_tm, mod, h0, conv0_tm, s_all, w, *, bt):
    tl, nb, _ = x_tm.shape
    assert nb % bt == 0 and nb % SUBLANES == 0 and CONV_W - 1 <= tl and tl + 3 <= TP
    rows = tl * nb
    n_blk = nb // bt

    def whole(a):
        nd = a.ndim
        return pl.BlockSpec(a.shape, lambda i: (0,) * nd, pipeline_mode=pl.Buffered(1))

    def per_layer(a):
        nd = a.ndim - 1
        return pl.BlockSpec((None,) + a.shape[1:], lambda i: (i // n_blk,) + (0,) * nd, pipeline_mode=pl.Buffered(1))
    state_spec = pl.BlockSpec((None, bt, N_HEADS_B, DK_B, DV_B), lambda i: (i // n_blk, i % n_blk, 0, 0, 0))
    args = [x_tm, mod, h0, conv0_tm, s_all, *w]
    in_specs = [whole(x_tm), per_layer(mod), per_layer(h0), per_layer(conv0_tm), state_spec]
    in_specs += [whole(a) if a.shape[0] == DEPTH and a.ndim == 2 else per_layer(a) for a in w]
    out_specs = [
        pl.BlockSpec((tl, nb, D_MODEL), lambda i: (0, 0, 0)),
        pl.BlockSpec((None, nb, D_A), lambda i: (i // n_blk, 0, 0)),
        pl.BlockSpec((None, CONV_W - 1, nb, D_A), lambda i: (i // n_blk, 0, 0, 0)),
        state_spec,
    ]
    out_shape = [
        jax.ShapeDtypeStruct((tl, nb, D_MODEL), F32),
        jax.ShapeDtypeStruct((DEPTH, nb, D_A), F32),
        jax.ShapeDtypeStruct((DEPTH, CONV_W - 1, nb, D_A), F32),
        jax.ShapeDtypeStruct(s_all.shape, F32),
    ]
    wide = lambda: pltpu.VMEM((rows, D_A), F32)
    slab8 = lambda: pltpu.VMEM((N_HEADS_B, nb * TP, LANES), F32)
    scratch = [
        pltpu.VMEM((tl, nb, D_MODEL), F32),
        pltpu.VMEM((rows, D_MODEL), BF16),
        pltpu.VMEM((rows + (CONV_W - 1) * nb, D_A), F32),
        wide(), wide(), wide(),
        wide(), wide(), wide(), wide(),
        wide(),
        slab8(), slab8(), slab8(), slab8(),
    ]
    return pl.pallas_call(
        functools.partial(_decode_kernel, nb=nb, bt=bt, tl=tl),
        grid=(DEPTH * n_blk,),
        in_specs=in_specs,
        out_specs=out_specs,
        out_shape=out_shape,
        scratch_shapes=scratch,
        compiler_params=pltpu.CompilerParams(dimension_semantics=("arbitrary",), vmem_limit_bytes=VMEM_LIMIT),
        name="decode_layers",
    )(*args)


CT = SUBLANES


def _prompt_kernel(x_ref, mod_ref, prew_ref, postw_ref, win_ref, convw_ref, convb_ref, wg_ref, ba_ref, bx_ref,
                   lam_ref, lbl_ref, onw_ref, wout_ref,
                   y_ref, hn_ref, convn_ref, sn_ref,
                   xt_s, hin_s, xpad_s, ccar_s, ga_s, a_s, u_s, xc_s, q_s, kk_s, b_s, gb_s, bl_s, hcar_s, st_s,
                   xl_s, qe_s, kd_s, v_s, oi_s,
                   *, bt, tl, n_levels):
    o_s = b_s
    m_s = hin_s
    j = pl.program_id(0)
    nj = pl.num_programs(0)
    rows = tl * bt
    ch = CT * bt
    n_ch = tl // CT
    nh = N_HEADS_B
    nt = D_MODEL // LANES
    tail = (CONV_W - 1) * bt

    @pl.when(j == 0)
    def _():
        hcar_s[...] = jnp.zeros_like(hcar_s)
        ccar_s[...] = jnp.zeros_like(ccar_s)
        st_s[...] = jnp.zeros_like(st_s)

    for c in range(n_ch):
        for jl in range(nt):
            for b in range(bt):
                xt_s[jl, pl.ds(c * ch + b, CT, stride=bt), :] = x_ref[b, c * CT:(c + 1) * CT, jl * LANES:(jl + 1) * LANES]

    ti = lax.broadcasted_iota(jnp.int32, (tl, tl), 0)
    si = lax.broadcasted_iota(jnp.int32, (tl, tl), 1)
    masks = []
    for lv in range(n_levels):
        h = CT << lv
        masks.append(((ti // (2 * h)) == (si // (2 * h))) & ((ti // h) % 2 == 1) & ((si // h) % 2 == 0))

    def bcast(v):
        return jnp.concatenate([v] * CT, axis=0)

    for l in range(DEPTH):
        shift, scale, gate = mod_ref[l, 0], mod_ref[l, 1], mod_ref[l, 2]
        pm = prew_ref[l] * (1.0 + scale)
        gp = postw_ref[l] * gate

        for c in range(n_ch):
            rs = pl.ds(c * ch, ch)
            ss = jnp.sum(sum(jnp.square(xt_s[jl, rs, :]) for jl in range(nt)), axis=-1, keepdims=True)
            inv = lax.rsqrt(ss * (1.0 / D_MODEL) + EPS)
            for jl in range(nt):
                ls = slice(jl * LANES, (jl + 1) * LANES)
                hin_s[rs, ls] = (xt_s[jl, rs, :] * inv * bcast(pm[:, ls]) + bcast(shift[:, ls])).astype(BF16)

        def proj(c0, width):
            return _dot(hin_s[...], win_ref[l, :, c0:c0 + width])
        for dst, c0_ in ((q_s, 2 * D_A), (kk_s, 2 * D_A + D_B), (v_s, 2 * D_A + 2 * D_B), (gb_s, 2 * D_A + 3 * D_B)):
            z = proj(c0_, D_B)
            for hd in range(nh):
                dst[hd] = z[:, hd * DK_B:(hd + 1) * DK_B]

        logits = lbl_ref[...]
        pe = jnp.exp(logits - jnp.max(logits, axis=0, keepdims=True))
        probs = pe / jnp.sum(pe, axis=0, keepdims=True)
        lb = jnp.sum(probs[1:l + 1], axis=0, keepdims=True) if l > 0 else jnp.zeros((1, D_B), F32)
        c0 = 0.5 * (1.0 + lb)
        c1 = 0.5 * (1.0 - lb)
        bl = [jnp.zeros((bt, DK_B), F32) for _ in range(nh)]
        for t in range(tl):
            sl = pl.ds(t * bt, bt)
            for hd in range(nh):
                ls = slice(hd * DK_B, (hd + 1) * DK_B)
                p = c1[:, ls] * jnp.tanh(kk_s[hd, sl, :])
                bl[hd] = bl[hd] + jnp.log(jnp.maximum(c0[:, ls] + p, F_MIN))
                b_s[hd, sl, :] = bl[hd]
                kk_s[hd, sl, :] = c1[:, ls] - p
        for hd in range(nh):
            bl_s[hd] = jnp.exp(bl[hd])

        for c in range(n_ch):
            rs = pl.ds(c * ch, ch)
            for hd in range(nh):
                bc, qc, kc, vc = b_s[hd, rs, :], q_s[hd, rs, :], kk_s[hd, rs, :], v_s[hd, rs, :]
                qe_s[hd, rs, :] = qc * jnp.exp(bc)
                kd_s[hd, rs, :] = kc * jnp.exp(bcast(bl[hd]) - bc)
                near = []
                for t in range(CT):
                    ts = slice(t * bt, (t + 1) * bt)
                    acc = jnp.sum(qc[ts] * kc[ts], axis=-1, keepdims=True) * vc[ts]
                    for u in range(t):
                        us = slice(u * bt, (u + 1) * bt)
                        w = qc[ts] * kc[us] * jnp.exp(bc[ts] - bc[us])
                        acc = acc + jnp.sum(w, axis=-1, keepdims=True) * vc[us]
                    near.append(acc)
                oi_s[hd, rs, :] = jnp.concatenate(near, axis=0)
                t0 = c * CT
                for lv in range(n_levels):
                    h = CT << lv
                    mid = (t0 // (2 * h)) * 2 * h + h - 1
                    bmid = bcast(b_s[hd, pl.ds(mid * bt, bt), :])
                    xl_s[lv, hd, rs, :] = qc * jnp.exp(bc - bmid) if t0 % (2 * h) >= h else kc * jnp.exp(bmid - bc)

        xpad_s[pl.ds(0, tail), :] = ccar_s[l]
        xpad_s[pl.ds(tail, rows), :] = proj(0, D_A)
        ga_s[...] = proj(D_A, D_A)
        cw = convw_ref[l]
        for c in range(n_ch):
            xc = convb_ref[l] + sum(xpad_s[pl.ds(c * ch + k * bt, ch), :] * cw[k:k + 1, :] for k in range(CONV_W))
            xc_s[pl.ds(c * ch, ch), :] = xc
        ctail = xpad_s[pl.ds(rows, tail), :]
        ccar_s[l] = ctail
        for c in range(D_A // LANES):
            g2 = _dot(xc_s[:, c * LANES:(c + 1) * LANES].astype(BF16), wg_ref[l, c])
            a_s[:, c * LANES:(c + 1) * LANES] = g2[:, :LANES]
            u_s[:, c * LANES:(c + 1) * LANES] = g2[:, LANES:]
        hcoef = -0.5 * C_RGLRU * jax.nn.softplus(-lam_ref[l])
        ba, bx = ba_ref[l], bx_ref[l]

        def scan_chunk(c, h):
            for tp in range(CT // 2):
                yas = []
                for t2 in range(2):
                    sl = pl.ds(c * ch + (2 * tp + t2) * bt, bt)
                    i = _sig_h(u_s[sl, :] + bx)
                    log_a = hcoef * jnp.tanh(a_s[sl, :] + ba) + hcoef
                    a = jnp.exp(log_a)
                    u = _sqrt_pos(jnp.tanh(log_a) * (-1.0 - a * a)) * (i * xc_s[sl, :])
                    h = a * h + u
                    yas.append(h * _silu_h(ga_s[sl, :]))
                m_s[pl.ds(c * ch + 2 * tp * bt, 2 * bt), :D_A] = jnp.concatenate(yas, axis=0).astype(BF16)
            return h

        def chains(b):
            sl = pl.ds(b, tl, stride=bt)
            hs = range(nh)
            xs = [[xl_s[lv, hd, sl, :].astype(BF16) for lv in range(n_levels)] for hd in hs]
            vs = [v_s[hd, sl, :].astype(BF16) for hd in hs]
            sts = [st_s[l, b, hd] for hd in hs]
            grams = [[_dot_nt(xs[hd][lv], xs[hd][lv]) for lv in range(n_levels)] for hd in hs]
            upd = [_dot_tn(vs[hd], kd_s[hd, sl, :].astype(BF16)) for hd in hs]
            amats = []
            for hd in hs:
                amat = jnp.zeros((tl, tl), F32)
                for lv in range(n_levels):
                    amat = amat + jnp.where(masks[lv], grams[hd][lv], 0.0)
                amats.append(amat.astype(BF16))
            os_ = [_dot(amats[hd], vs[hd]) + _dot_nt(qe_s[hd, sl, :].astype(BF16), sts[hd].astype(BF16)) for hd in hs]
            for hd in hs:
                o_s[hd, sl, :] = (_rms(os_[hd] + oi_s[hd, sl, :], onw_ref[l][:, hd * DV_B:(hd + 1) * DV_B])
                                  * _silu_h(gb_s[hd, sl, :]))
                st_s[l, b, hd] = sts[hd] * bl_s[hd, b:b + 1, :] + upd[hd]

        h_last = hcar_s[l]
        for idx in range(max(bt, n_ch)):
            if idx < bt:
                chains(idx)
            if idx < n_ch:
                h_last = scan_chunk(idx, h_last)
        hcar_s[l] = h_last

        for hd in range(nh):
            m_s[:, D_A + hd * DV_B:D_A + (hd + 1) * DV_B] = o_s[hd].astype(BF16)

        mo = _dot(m_s[...], wout_ref[l])
        for jl in range(nt):
            xl_s[jl // nh, jl % nh] = mo[:, jl * LANES:(jl + 1) * LANES]
        for c in range(n_ch):
            rs = pl.ds(c * ch, ch)
            ss = jnp.sum(sum(jnp.square(xl_s[jl // nh, jl % nh, rs, :]) for jl in range(nt)), axis=-1, keepdims=True)
            inv = lax.rsqrt(ss * (1.0 / D_MODEL) + EPS)
            for jl in range(nt):
                ls = slice(jl * LANES, (jl + 1) * LANES)
                xt_s[jl, rs, :] = xt_s[jl, rs, :] + xl_s[jl // nh, jl % nh, rs, :] * inv * bcast(gp[:, ls])

        @pl.when(j == nj - 1)
        def _():
            hn_ref[l] = h_last
            convn_ref[l] = ctail.reshape(CONV_W - 1, bt, D_A)
            for b in range(bt):
                for hd in range(nh):
                    sn_ref[l, b, hd] = st_s[l, b, hd].T

    for c in range(n_ch):
        for jl in range(nt):
            for b in range(bt):
                y_ref[b, c * CT:(c + 1) * CT, jl * LANES:(jl + 1) * LANES] = xt_s[jl, pl.ds(c * ch + b, CT, stride=bt), :]


def _prompt(x, mod, w, *, tl):
    B, L, _ = x.shape
    bt = B
    assert bt == SUBLANES and L % tl == 0 and tl % CT == 0
    n_levels = (tl // CT).bit_length() - 1
    assert (CT << n_levels) == tl and n_levels * N_HEADS_B >= D_MODEL // LANES
    rows = tl * bt
    tail = (CONV_W - 1) * bt

    def const(shape):
        nd = len(shape)
        return pl.BlockSpec(shape, lambda j: (0,) * nd, pipeline_mode=pl.Buffered(1))
    in_specs = [pl.BlockSpec((bt, tl, D_MODEL), lambda j: (0, j, 0)), const(mod.shape)] + [const(a.shape) for a in w]
    out_specs = [
        pl.BlockSpec((bt, tl, D_MODEL), lambda j: (0, j, 0)),
        pl.BlockSpec((DEPTH, bt, D_A), lambda j: (0, 0, 0)),
        pl.BlockSpec((DEPTH, CONV_W - 1, bt, D_A), lambda j: (0, 0, 0, 0)),
        pl.BlockSpec((DEPTH, bt, N_HEADS_B, DK_B, DV_B), lambda j: (0, 0, 0, 0, 0)),
    ]
    out_shape = [
        jax.ShapeDtypeStruct((B, L, D_MODEL), F32),
        jax.ShapeDtypeStruct((DEPTH, B, D_A), F32),
        jax.ShapeDtypeStruct((DEPTH, CONV_W - 1, B, D_A), F32),
        jax.ShapeDtypeStruct((DEPTH, B, N_HEADS_B, DK_B, DV_B), F32),
    ]
    slab = lambda n: pltpu.VMEM((n, rows, LANES), F32)
    wide = lambda: pltpu.VMEM((rows, D_A), F32)
    scratch = [
        slab(D_MODEL // LANES),
        pltpu.VMEM((rows, D_MODEL), BF16),
        pltpu.VMEM((rows + tail, D_A), F32),
        pltpu.VMEM((DEPTH, tail, D_A), F32),
        wide(), wide(), wide(), wide(),
        slab(N_HEADS_B), slab(N_HEADS_B), slab(N_HEADS_B), slab(N_HEADS_B),
        pltpu.VMEM((N_HEADS_B, bt, DK_B), F32),
        pltpu.VMEM((DEPTH, bt, D_A), F32),
        pltpu.VMEM((DEPTH, bt, N_HEADS_B, DV_B, DK_B), F32),
        pltpu.VMEM((n_levels, N_HEADS_B, rows, LANES), F32),
        slab(N_HEADS_B), slab(N_HEADS_B), slab(N_HEADS_B), slab(N_HEADS_B),
    ]
    body = functools.partial(_prompt_kernel, bt=bt, tl=tl, n_levels=n_levels)
    return pl.pallas_call(
        body,
        grid=(L // tl,),
        in_specs=in_specs,
        out_specs=out_specs,
        out_shape=out_shape,
        scratch_shapes=scratch,
        compiler_params=pltpu.CompilerParams(dimension_semantics=("arbitrary",), vmem_limit_bytes=VMEM_LIMIT),
        name="prompt_layers",
    )(x, mod, *w)


def _gate_weights(wa, wx):
    per = LANES // BLK_A

    def bd(w):
        w = w.reshape(D_A // LANES, per, BLK_A, BLK_A)
        eye = jnp.eye(per, dtype=w.dtype)
        return jnp.einsum('jpcd,pq->jpcqd', w, eye).reshape(D_A // LANES, LANES, LANES)
    return (0.5 * jnp.concatenate([bd(wa), bd(wx)], axis=-1)).astype(BF16)


def kernel(x_prompt, x_sample, state_rglru_h, state_rglru_conv, state_hgrn_S, c_prompt, c_sample, ada_w, ada_b,
           pre_norm_w, post_norm_w, w_in, conv_w, conv_b, rg_wa, rg_ba, rg_wx, rg_bx, rg_lambda, hg_lb_logits,
           hg_onorm_w, w_out):
    bp, bs = x_prompt.shape[0], x_sample.shape[0]
    mod = _modulation(jnp.concatenate([c_prompt, c_sample], axis=0), ada_w,
                      ada_b.reshape(DEPTH, 1, 3 * D_MODEL))
    half = jnp.full((D_A,), 0.5, F32)
    one = jnp.ones((D_A,), F32)
    col_scale = jnp.concatenate([one, half, one, half, one, half])
    w_in_bf, w_out_bf = (w_in * col_scale).astype(BF16), w_out.astype(BF16)
    wg = jnp.stack([_gate_weights(rg_wa[l], rg_wx[l]) for l in range(DEPTH)])
    row = lambda a: a[:, None, :]
    wp = (row(pre_norm_w), row(post_norm_w), w_in_bf, conv_w, row(conv_b), wg, row(0.5 * rg_ba), row(0.5 * rg_bx),
          row(rg_lambda), hg_lb_logits, row(hg_onorm_w), w_out_bf)
    yp, hp, cvp, sp = _prompt(x_prompt, mod[:, :, :bp], wp, tl=64)

    ys, hs, cvs, ss = _decode(jnp.transpose(x_sample, (1, 0, 2)), mod[:, :, bp:], state_rglru_h,
                              jnp.transpose(state_rglru_conv, (0, 2, 1, 3)), state_hgrn_S, wp, bt=16)
    return (yp, jnp.transpose(ys, (1, 0, 2)), hp, jnp.transpose(cvp, (0, 2, 1, 3)), sp,
            hs, jnp.transpose(cvs, (0, 2, 1, 3)), ss)
```

```python
import functools

import jax
import jax.numpy as jnp
from jax import lax
from jax.experimental import pallas as pl
from jax.experimental.pallas import tpu as pltpu

D_MODEL = 1024
DEPTH = 2
D_A = 512
N_BLK_A = 8
BLK_A = 64
CONV_W = 4
C_RGLRU = 8.0
D_B = 512
N_HEADS_B = 4
DK_B = 128
DV_B = 128
P_IN = 2 * D_A + 4 * D_B
EPS = 1e-6
F_MIN = 1e-30

LANES = 128
SUBLANES = 8
VMEM_LIMIT = 60 * 1024 * 1024

BF16 = jnp.bfloat16
F32 = jnp.float32


def _dot(a, b):
    return jnp.dot(a, b, preferred_element_type=F32)


def _dot_nt(a, b):
    return lax.dot_general(a, b, (((1,), (1,)), ((), ())), preferred_element_type=F32)


def _dot_tn(a, b):
    return lax.dot_general(a, b, (((0,), (0,)), ((), ())), preferred_element_type=F32)


def _rms(x, w):
    return x * lax.rsqrt(jnp.mean(x * x, axis=-1, keepdims=True) + EPS) * w


def _silu(x):
    return x * jax.nn.sigmoid(x)


def _mod_kernel(c_ref, w_ref, b_ref, o_ref):
    o_ref[...] = _dot(_silu(c_ref[...]).astype(BF16), w_ref[...].astype(BF16)) + b_ref[...]


def _modulation(c_all, ada_w, ada_b):
    nb = c_all.shape[0]
    return pl.pallas_call(
        _mod_kernel,
        grid=(DEPTH, 3),
        in_specs=[
            pl.BlockSpec((nb, D_MODEL), lambda l, k: (0, 0)),
            pl.BlockSpec((None, D_MODEL, D_MODEL), lambda l, k: (l, 0, k)),
            pl.BlockSpec((None, 1, D_MODEL), lambda l, k: (l, 0, k)),
        ],
        out_specs=pl.BlockSpec((None, None, nb, D_MODEL), lambda l, k: (l, k, 0, 0)),
        out_shape=jax.ShapeDtypeStruct((DEPTH, 3, nb, D_MODEL), F32),
        name="adaln_mod",
    )(c_all, ada_w, ada_b)


TP = SUBLANES


def _sig_h(xh):
    return 0.5 * jnp.tanh(xh) + 0.5


def _silu_h(xh):
    return xh * jnp.tanh(xh) + xh


def _sqrt_pos(y):
    return jnp.where(y > 0.0, y * lax.rsqrt(y), 0.0)


def _decode_kernel(x_ref, mod_ref, h0_ref, conv0_ref, s0_ref, prew_ref, postw_ref, win_ref, convw_ref,
                   convb_ref, wg_ref, ba_ref, bx_ref, lam_ref, lbl_ref, onw_ref, wout_ref,
                   y_ref, hn_ref, convn_ref, sn_ref,
                   xt_s, hin_s, xpad_s, ga_s, a_s, u_s, q_s, kk_s, b_s, gb_s, oi_s, qe8_s, kdx8_s, v8_s, oc8_s,
                   *, nb, bt, tl):
    m_s = hin_s
    i = pl.program_id(0)
    n_blk = nb // bt
    layer = i // n_blk
    blk = i % n_blk
    rows = tl * nb
    tail = (CONV_W - 1) * nb
    nh = N_HEADS_B
    heads = [slice(hd * DK_B, (hd + 1) * DK_B) for hd in range(nh)]

    @pl.when(i == 0)
    def _():
        xt_s[...] = x_ref[...]

    @pl.when(blk == 0)
    def _():
        shift, scale = mod_ref[0], mod_ref[1]
        pm = prew_ref[...] * (1.0 + scale)
        for t in range(tl):
            xt = xt_s[t]
            inv = lax.rsqrt(jnp.mean(xt * xt, axis=-1, keepdims=True) + EPS)
            hin_s[pl.ds(t * nb, nb), :] = (xt * inv * pm + shift).astype(BF16)

        def proj(c0, width):
            return _dot(hin_s[...], win_ref[:, c0:c0 + width])
        xpad_s[pl.ds(0, tail), :] = conv0_ref[...].reshape(tail, D_A)
        xpad_s[pl.ds(tail, rows), :] = proj(0, D_A)
        ga_s[...] = proj(D_A, D_A)
        q_s[...] = proj(2 * D_A, D_B)
        kk_s[...] = proj(2 * D_A + D_B, D_B)
        iv = proj(2 * D_A + 2 * D_B, D_B)
        gb_s[...] = proj(2 * D_A + 3 * D_B, D_B)

        xc = convb_ref[...] + sum(xpad_s[pl.ds(k * nb, rows), :] * convw_ref[k:k + 1, :] for k in range(CONV_W))
        convn_ref[...] = xpad_s[pl.ds(rows, tail), :].reshape(CONV_W - 1, nb, D_A)
        xcb = xc.astype(BF16)
        for c in range(D_A // LANES):
            g2 = _dot(xcb[:, c * LANES:(c + 1) * LANES], wg_ref[c])
            a_s[:, c * LANES:(c + 1) * LANES] = g2[:, :LANES]
            u_s[:, c * LANES:(c + 1) * LANES] = g2[:, LANES:]
        hcoef = -0.5 * C_RGLRU * jax.nn.softplus(-lam_ref[...])
        h = h0_ref[...]
        for t in range(tl):
            sl = pl.ds(t * nb, nb)
            ig = _sig_h(u_s[sl, :] + bx_ref[...])
            log_a = hcoef * jnp.tanh(a_s[sl, :] + ba_ref[...]) + hcoef
            a = jnp.exp(log_a)
            u = _sqrt_pos(jnp.tanh(log_a) * (-1.0 - a * a)) * (ig * xc[t * nb:(t + 1) * nb])
            h = a * h + u
            m_s[sl, :D_A] = (h * _silu_h(ga_s[sl, :])).astype(BF16)
        hn_ref[...] = h

        logits = lbl_ref[...]
        pe = jnp.exp(logits - jnp.max(logits, axis=0, keepdims=True))
        probs = pe / jnp.sum(pe, axis=0, keepdims=True)
        lb = jnp.zeros((1, D_B), F32)
        for k in range(1, DEPTH):
            lb = lb + jnp.where(layer >= k, probs[k:k + 1], 0.0)
        c0 = 0.5 * (1.0 + lb)
        c1 = 0.5 * (1.0 - lb)
        bl = jnp.zeros((nb, D_B), F32)
        for t in range(tl):
            sl = pl.ds(t * nb, nb)
            p = c1 * jnp.tanh(kk_s[sl, :])
            bl = bl + jnp.log(jnp.maximum(c0 + p, F_MIN))
            b_s[sl, :] = bl
            kk_s[sl, :] = c1 - p

        for t in range(tl):
            st_ = pl.ds(t * nb, nb)
            acc = [jnp.zeros((nb, DV_B), F32) for _ in range(nh)]
            for s in range(t + 1):
                ss_ = pl.ds(s * nb, nb)
                w = q_s[st_, :] * kk_s[ss_, :]
                if s < t:
                    w = w * jnp.exp(b_s[st_, :] - b_s[ss_, :])
                for hd in range(nh):
                    acc[hd] = acc[hd] + jnp.sum(w[:, heads[hd]], axis=-1, keepdims=True) * iv[s * nb:(s + 1) * nb, heads[hd]]
            for hd in range(nh):
                oi_s[st_, heads[hd]] = acc[hd]

        zrow = jnp.zeros((nb, LANES), F32)
        ebl = jnp.exp(bl)
        e_hi = ebl.astype(BF16).astype(F32)
        e_mid = (ebl - e_hi).astype(BF16).astype(F32)
        e_lo = ebl - e_hi - e_mid
        for hd in range(nh):
            for t in range(TP):
                dst = pl.ds(t, nb, stride=TP)
                if t < tl:
                    st_ = pl.ds(t * nb, nb)
                    bt_ = b_s[st_, heads[hd]]
                    qe8_s[hd, dst, :] = q_s[st_, heads[hd]] * jnp.exp(bt_)
                    kdx8_s[hd, dst, :] = kk_s[st_, heads[hd]] * jnp.exp(bl[:, heads[hd]] - bt_)
                    v8_s[hd, dst, :] = iv[t * nb:(t + 1) * nb, heads[hd]]
                else:
                    qe8_s[hd, dst, :] = zrow
                    v8_s[hd, dst, :] = zrow
                    dec = (e_hi, e_mid, e_lo)
                    kdx8_s[hd, dst, :] = dec[t - tl][:, heads[hd]] if t - tl < len(dec) else zrow

    rid = lax.broadcasted_iota(jnp.int32, (TP, LANES), 0)
    ones_rows = jnp.where((rid >= tl) & (rid < tl + 3), 1.0, 0.0)
    for bl_ in range(bt):
        r8 = pl.ds(pl.multiple_of((blk * bt + bl_) * TP, TP), TP)
        for hd in range(nh):
            s_old = s0_ref[bl_, hd]
            oc8_s[hd, r8, :] = _dot(qe8_s[hd, r8, :].astype(BF16), s_old.astype(BF16))
            rhs = jnp.concatenate([v8_s[hd, r8, :], ones_rows], axis=1).astype(BF16)
            g = _dot_tn(kdx8_s[hd, r8, :].astype(BF16), rhs)
            sn_ref[bl_, hd] = s_old * g[:, DV_B:] + g[:, :DV_B]

    @pl.when(blk == n_blk - 1)
    def _():
        for t in range(tl):
            st_ = pl.ds(t * nb, nb)
            for hd in range(nh):
                oh = oi_s[st_, heads[hd]] + oc8_s[hd, pl.ds(t, nb, stride=TP), :]
                yb = _rms(oh, onw_ref[:, heads[hd]]) * _silu_h(gb_s[st_, heads[hd]])
                m_s[st_, D_A + hd * DV_B:D_A + (hd + 1) * DV_B] = yb.astype(BF16)
        mo = _dot(m_s[...], wout_ref[...])
        gp = postw_ref[...] * mod_ref[2]
        for t in range(tl):
            mt = mo[t * nb:(t + 1) * nb]
            inv = lax.rsqrt(jnp.mean(mt * mt, axis=-1, keepdims=True) + EPS)
            xt_s[t] = xt_s[t] + mt * inv * gp

    @pl.when(i == pl.num_programs(0) - 1)
    def _():
        y_ref[...] = xt_s[...]


def _decode(x_tm, mod, h0, conv0_tm, s_all, w, *, bt):
    tl, nb, _ = x_tm.shape
    assert nb % bt == 0 and nb % SUBLANES == 0 and CONV_W - 1 <= tl and tl + 3 <= TP
    rows = tl * nb
    n_blk = nb // bt

    def whole(a):
        nd = a.ndim
        return pl.BlockSpec(a.shape, lambda i: (0,) * nd, pipeline_mode=pl.Buffered(1))

    def per_layer(a):
        nd = a.ndim - 1
        return pl.BlockSpec((None,) + a.shape[1:], lambda i: (i // n_blk,) + (0,) * nd, pipeline_mode=pl.Buffered(1))
    state_spec = pl.BlockSpec((None, bt, N_HEADS_B, DK_B, DV_B), lambda i: (i // n_blk, i % n_blk, 0, 0, 0))
    args = [x_tm, mod, h0, conv0_tm, s_all, *w]
    in_specs = [whole(x_tm), per_layer(mod), per_layer(h0), per_layer(conv0_tm), state_spec]
    in_specs += [whole(a) if a.shape[0] == DEPTH and a.ndim == 2 else per_layer(a) for a in w]
    out_specs = [
        pl.BlockSpec((tl, nb, D_MODEL), lambda i: (0, 0, 0)),
        pl.BlockSpec((None, nb, D_A), lambda i: (i // n_blk, 0, 0)),
        pl.BlockSpec((None, CONV_W - 1, nb, D_A), lambda i: (i // n_blk, 0, 0, 0)),
        state_spec,
    ]
    out_shape = [
        jax.ShapeDtypeStruct((tl, nb, D_MODEL), F32),
        jax.ShapeDtypeStruct((DEPTH, nb, D_A), F32),
        jax.ShapeDtypeStruct((DEPTH, CONV_W - 1, nb, D_A), F32),
        jax.ShapeDtypeStruct(s_all.shape, F32),
    ]
    wide = lambda: pltpu.VMEM((rows, D_A), F32)
    slab8 = lambda: pltpu.VMEM((N_HEADS_B, nb * TP, LANES), F32)
    scratch = [
        pltpu.VMEM((tl, nb, D_MODEL), F32),
        pltpu.VMEM((rows, D_MODEL), BF16),
        pltpu.VMEM((rows + (CONV_W - 1) * nb, D_A), F32),
        wide(), wide(), wide(),
        wide(), wide(), wide(), wide(),
        wide(),
        slab8(), slab8(), slab8(), slab8(),
    ]
    return pl.pallas_call(
        functools.partial(_decode_kernel, nb=nb, bt=bt, tl=tl),
        grid=(DEPTH * n_blk,),
        in_specs=in_specs,
        out_specs=out_specs,
        out_shape=out_shape,
        scratch_shapes=scratch,
        compiler_params=pltpu.CompilerParams(dimension_semantics=("arbitrary",), vmem_limit_bytes=VMEM_LIMIT),
        name="decode_layers",
    )(*args)


CT = SUBLANES


def _prompt_kernel(x_ref, mod_ref, prew_ref, postw_ref, win_ref, convw_ref, convb_ref, wg_ref, ba_ref, bx_ref,
                   lam_ref, lbl_ref, onw_ref, wout_ref,
                   y_ref, hn_ref, convn_ref, sn_ref,
                   xt_s, hin_s, xpad_s, ccar_s, ga_s, a_s, u_s, xc_s, q_s, kk_s, b_s, gb_s, bl_s, hcar_s, st_s,
                   xl_s, qe_s, kd_s, v_s, oi_s,
                   *, bt, tl, n_levels):
    o_s = b_s
    m_s = hin_s
    j = pl.program_id(0)
    nj = pl.num_programs(0)
    rows = tl * bt
    ch = CT * bt
    n_ch = tl // CT
    nh = N_HEADS_B
    nt = D_MODEL // LANES
    tail = (CONV_W - 1) * bt

    @pl.when(j == 0)
    def _():
        hcar_s[...] = jnp.zeros_like(hcar_s)
        ccar_s[...] = jnp.zeros_like(ccar_s)
        st_s[...] = jnp.zeros_like(st_s)

    for c in range(n_ch):
        for jl in range(nt):
            for b in range(bt):
                xt_s[jl, pl.ds(c * ch + b, CT, stride=bt), :] = x_ref[b, c * CT:(c + 1) * CT, jl * LANES:(jl + 1) * LANES]

    ti = lax.broadcasted_iota(jnp.int32, (tl, tl), 0)
    si = lax.broadcasted_iota(jnp.int32, (tl, tl), 1)
    masks = []
    for lv in range(n_levels):
        h = CT << lv
        masks.append(((ti // (2 * h)) == (si // (2 * h))) & ((ti // h) % 2 == 1) & ((si // h) % 2 == 0))

    def bcast(v):
        return jnp.concatenate([v] * CT, axis=0)

    for l in range(DEPTH):
        shift, scale, gate = mod_ref[l, 0], mod_ref[l, 1], mod_ref[l, 2]
        pm = prew_ref[l] * (1.0 + scale)
        gp = postw_ref[l] * gate

        for c in range(n_ch):
            rs = pl.ds(c * ch, ch)
            ss = jnp.sum(sum(jnp.square(xt_s[jl, rs, :]) for jl in range(nt)), axis=-1, keepdims=True)
            inv = lax.rsqrt(ss * (1.0 / D_MODEL) + EPS)
            for jl in range(nt):
                ls = slice(jl * LANES, (jl + 1) * LANES)
                hin_s[rs, ls] = (xt_s[jl, rs, :] * inv * bcast(pm[:, ls]) + bcast(shift[:, ls])).astype(BF16)

        def proj(c0, width):
            return _dot(hin_s[...], win_ref[l, :, c0:c0 + width])
        for dst, c0_ in ((kk_s, 2 * D_A + D_B), (q_s, 2 * D_A), (v_s, 2 * D_A + 2 * D_B), (gb_s, 2 * D_A + 3 * D_B)):
            z = proj(c0_, D_B)
            for hd in range(nh):
                dst[hd] = z[:, hd * DK_B:(hd + 1) * DK_B]

        logits = lbl_ref[...]
        pe = jnp.exp(logits - jnp.max(logits, axis=0, keepdims=True))
        probs = pe / jnp.sum(pe, axis=0, keepdims=True)
        lb = jnp.sum(probs[1:l + 1], axis=0, keepdims=True) if l > 0 else jnp.zeros((1, D_B), F32)
        c0 = 0.5 * (1.0 + lb)
        c1 = 0.5 * (1.0 - lb)
        bl = [jnp.zeros((bt, DK_B), F32) for _ in range(nh)]
        for t in range(tl):
            sl = pl.ds(t * bt, bt)
            for hd in range(nh):
                ls = slice(hd * DK_B, (hd + 1) * DK_B)
                p = c1[:, ls] * jnp.tanh(kk_s[hd, sl, :])
                bl[hd] = bl[hd] + jnp.log(jnp.maximum(c0[:, ls] + p, F_MIN))
                b_s[hd, sl, :] = bl[hd]
                kk_s[hd, sl, :] = c1[:, ls] - p
        for hd in range(nh):
            bl_s[hd] = jnp.exp(bl[hd])

        for c in range(n_ch):
            rs = pl.ds(c * ch, ch)
            for hd in range(nh):
                bc, qc, kc, vc = b_s[hd, rs, :], q_s[hd, rs, :], kk_s[hd, rs, :], v_s[hd, rs, :]
                qe_s[hd, rs, :] = qc * jnp.exp(bc)
                kd_s[hd, rs, :] = kc * jnp.exp(bcast(bl[hd]) - bc)
                near = []
                for t in range(CT):
                    ts = slice(t * bt, (t + 1) * bt)
                    acc = jnp.sum(qc[ts] * kc[ts], axis=-1, keepdims=True) * vc[ts]
                    for u in range(t):
                        us = slice(u * bt, (u + 1) * bt)
                        w = qc[ts] * kc[us] * jnp.exp(bc[ts] - bc[us])
                        acc = acc + jnp.sum(w, axis=-1, keepdims=True) * vc[us]
                    near.append(acc)
                oi_s[hd, rs, :] = jnp.concatenate(near, axis=0)
                t0 = c * CT
                for lv in range(n_levels):
                    h = CT << lv
                    mid = (t0 // (2 * h)) * 2 * h + h - 1
                    bmid = bcast(b_s[hd, pl.ds(mid * bt, bt), :])
                    xl_s[lv, hd, rs, :] = qc * jnp.exp(bc - bmid) if t0 % (2 * h) >= h else kc * jnp.exp(bmid - bc)

        xpad_s[pl.ds(0, tail), :] = ccar_s[l]
        xpad_s[pl.ds(tail, rows), :] = proj(0, D_A)
        ga_s[...] = proj(D_A, D_A)
        cw = convw_ref[l]
        for c in range(n_ch):
            xc = convb_ref[l] + sum(xpad_s[pl.ds(c * ch + k * bt, ch), :] * cw[k:k + 1, :] for k in range(CONV_W))
            xc_s[pl.ds(c * ch, ch), :] = xc
        ctail = xpad_s[pl.ds(rows, tail), :]
        ccar_s[l] = ctail
        for c in range(D_A // LANES):
            g2 = _dot(xc_s[:, c * LANES:(c + 1) * LANES].astype(BF16), wg_ref[l, c])
            a_s[:, c * LANES:(c + 1) * LANES] = g2[:, :LANES]
            u_s[:, c * LANES:(c + 1) * LANES] = g2[:, LANES:]
        hcoef = -0.5 * C_RGLRU * jax.nn.softplus(-lam_ref[l])
        ba, bx = ba_ref[l], bx_ref[l]

        def scan_chunk(c, h):
            for tp in range(CT // 2):
                yas = []
                for t2 in range(2):
                    sl = pl.ds(c * ch + (2 * tp + t2) * bt, bt)
                    i = _sig_h(u_s[sl, :] + bx)
                    log_a = hcoef * jnp.tanh(a_s[sl, :] + ba) + hcoef
                    a = jnp.exp(log_a)
                    u = _sqrt_pos(jnp.tanh(log_a) * (-1.0 - a * a)) * (i * xc_s[sl, :])
                    h = a * h + u
                    yas.append(h * _silu_h(ga_s[sl, :]))
                m_s[pl.ds(c * ch + 2 * tp * bt, 2 * bt), :D_A] = jnp.concatenate(yas, axis=0).astype(BF16)
            return h

        def chains(b):
            sl = pl.ds(b, tl, stride=bt)
            hs = range(nh)
            xs = [[xl_s[lv, hd, sl, :].astype(BF16) for lv in range(n_levels)] for hd in hs]
            vs = [v_s[hd, sl, :].astype(BF16) for hd in hs]
            sts = [st_s[l, b, hd] for hd in hs]
            grams = [[_dot_nt(xs[hd][lv], xs[hd][lv]) for lv in range(n_levels)] for hd in hs]
            upd = [_dot_tn(vs[hd], kd_s[hd, sl, :].astype(BF16)) for hd in hs]
            amats = []
            for hd in hs:
                amat = jnp.zeros((tl, tl), F32)
                for lv in range(n_levels):
                    amat = amat + jnp.where(masks[lv], grams[hd][lv], 0.0)
                amats.append(amat.astype(BF16))
            os_ = [_dot(amats[hd], vs[hd]) + _dot_nt(qe_s[hd, sl, :].astype(BF16), sts[hd].astype(BF16)) for hd in hs]
            for hd in hs:
                o_s[hd, sl, :] = (_rms(os_[hd] + oi_s[hd, sl, :], onw_ref[l][:, hd * DV_B:(hd + 1) * DV_B])
                                  * _silu_h(gb_s[hd, sl, :]))
                st_s[l, b, hd] = sts[hd] * bl_s[hd, b:b + 1, :] + upd[hd]

        h_last = hcar_s[l]
        for idx in range(max(bt, n_ch)):
            if idx < bt:
                chains(idx)
            if idx < n_ch:
                h_last = scan_chunk(idx, h_last)
        hcar_s[l] = h_last

        for hd in range(nh):
            m_s[:, D_A + hd * DV_B:D_A + (hd + 1) * DV_B] = o_s[hd].astype(BF16)

        mo = _dot(m_s[...], wout_ref[l])
        for jl in range(nt):
            xl_s[jl // nh, jl % nh] = mo[:, jl * LANES:(jl + 1) * LANES]
        for c in range(n_ch):
            rs = pl.ds(c * ch, ch)
            ss = jnp.sum(sum(jnp.square(xl_s[jl // nh, jl % nh, rs, :]) for jl in range(nt)), axis=-1, keepdims=True)
            inv = lax.rsqrt(ss * (1.0 / D_MODEL) + EPS)
            for jl in range(nt):
                ls = slice(jl * LANES, (jl + 1) * LANES)
                xt_s[jl, rs, :] = xt_s[jl, rs, :] + xl_s[jl // nh, jl % nh, rs, :] * inv * bcast(gp[:, ls])

        @pl.when(j == nj - 1)
        def _():
            hn_ref[l] = h_last
            convn_ref[l] = ctail.reshape(CONV_W - 1, bt, D_A)
            for b in range(bt):
                for hd in range(nh):
                    sn_ref[l, b, hd] = st_s[l, b, hd].T

    for c in range(n_ch):
        for jl in range(nt):
            for b in range(bt):
                y_ref[b, c * CT:(c + 1) * CT, jl * LANES:(jl + 1) * LANES] = xt_s[jl, pl.ds(c * ch + b, CT, stride=bt), :]


def _prompt(x, mod, w, *, tl):
    B, L, _ = x.shape
    bt = B
    assert bt == SUBLANES and L % tl == 0 and tl % CT == 0
    n_levels = (tl // CT).bit_length() - 1
    assert (CT << n_levels) == tl and n_levels * N_HEADS_B >= D_MODEL // LANES
    rows = tl * bt
    tail = (CONV_W - 1) * bt

    def const(shape):
        nd = len(shape)
        return pl.BlockSpec(shape, lambda j: (0,) * nd, pipeline_mode=pl.Buffered(1))
    in_specs = [pl.BlockSpec((bt, tl, D_MODEL), lambda j: (0, j, 0)), const(mod.shape)] + [const(a.shape) for a in w]
    out_specs = [
        pl.BlockSpec((bt, tl, D_MODEL), lambda j: (0, j, 0)),
        pl.BlockSpec((DEPTH, bt, D_A), lambda j: (0, 0, 0)),
        pl.BlockSpec((DEPTH, CONV_W - 1, bt, D_A), lambda j: (0, 0, 0, 0)),
        pl.BlockSpec((DEPTH, bt, N_HEADS_B, DK_B, DV_B), lambda j: (0, 0, 0, 0, 0)),
    ]
    out_shape = [
        jax.ShapeDtypeStruct((B, L, D_MODEL), F32),
        jax.ShapeDtypeStruct((DEPTH, B, D_A), F32),
        jax.ShapeDtypeStruct((DEPTH, CONV_W - 1, B, D_A), F32),
        jax.ShapeDtypeStruct((DEPTH, B, N_HEADS_B, DK_B, DV_B), F32),
    ]
    slab = lambda n: pltpu.VMEM((n, rows, LANES), F32)
    wide = lambda: pltpu.VMEM((rows, D_A), F32)
    scratch = [
        slab(D_MODEL // LANES),
        pltpu.VMEM((rows, D_MODEL), BF16),
        pltpu.VMEM((rows + tail, D_A), F32),
        pltpu.VMEM((DEPTH, tail, D_A), F32),
        wide(), wide(), wide(), wide(),
        slab(N_HEADS_B), slab(N_HEADS_B), slab(N_HEADS_B), slab(N_HEADS_B),
        pltpu.VMEM((N_HEADS_B, bt, DK_B), F32),
        pltpu.VMEM((DEPTH, bt, D_A), F32),
        pltpu.VMEM((DEPTH, bt, N_HEADS_B, DV_B, DK_B), F32),
        pltpu.VMEM((n_levels, N_HEADS_B, rows, LANES), F32),
        slab(N_HEADS_B), slab(N_HEADS_B), slab(N_HEADS_B), slab(N_HEADS_B),
    ]
    body = functools.partial(_prompt_kernel, bt=bt, tl=tl, n_levels=n_levels)
    return pl.pallas_call(
        body,
        grid=(L // tl,),
        in_specs=in_specs,
        out_specs=out_specs,
        out_shape=out_shape,
        scratch_shapes=scratch,
        compiler_params=pltpu.CompilerParams(dimension_semantics=("arbitrary",), vmem_limit_bytes=VMEM_LIMIT),
        name="prompt_layers",
    )(x, mod, *w)


def _gate_weights(wa, wx):
    per = LANES // BLK_A

    def bd(w):
        w = w.reshape(D_A // LANES, per, BLK_A, BLK_A)
        eye = jnp.eye(per, dtype=w.dtype)
        return jnp.einsum('jpcd,pq->jpcqd', w, eye).reshape(D_A // LANES, LANES, LANES)
    return (0.5 * jnp.concatenate([bd(wa), bd(wx)], axis=-1)).astype(BF16)


def kernel(x_prompt, x_sample, state_rglru_h, state_rglru_conv, state_hgrn_S, c_prompt, c_sample, ada_w, ada_b,
           pre_norm_w, post_norm_w, w_in, conv_w, conv_b, rg_wa, rg_ba, rg_wx, rg_bx, rg_lambda, hg_lb_logits,
           hg_onorm_w, w_out):
    bp, bs = x_prompt.shape[0], x_sample.shape[0]
    mod = _modulation(jnp.concatenate([c_prompt, c_sample], axis=0), ada_w,
                      ada_b.reshape(DEPTH, 1, 3 * D_MODEL))
    half = jnp.full((D_A,), 0.5, F32)
    one = jnp.ones((D_A,), F32)
    col_scale = jnp.concatenate([one, half, one, half, one, half])
    w_in_bf, w_out_bf = (w_in * col_scale).astype(BF16), w_out.astype(BF16)
    wg = jnp.stack([_gate_weights(rg_wa[l], rg_wx[l]) for l in range(DEPTH)])
    row = lambda a: a[:, None, :]
    wp = (row(pre_norm_w), row(post_norm_w), w_in_bf, conv_w, row(conv_b), wg, row(0.5 * rg_ba), row(0.5 * rg_bx),
          row(rg_lambda), hg_lb_logits, row(hg_onorm_w), w_out_bf)
    yp, hp, cvp, sp = _prompt(x_prompt, mod[:, :, :bp], wp, tl=64)

    ys, hs, cvs, ss = _decode(jnp.transpose(x_sample, (1, 0, 2)), mod[:, :, bp:], state_rglru_h,
                              jnp.transpose(state_rglru_conv, (0, 2, 1, 3)), state_hgrn_S, wp, bt=16)
    return (yp, jnp.transpose(ys, (1, 0, 2)), hp, jnp.transpose(cvp, (0, 2, 1, 3)), sp,
            hs, jnp.transpose(cvs, (0, 2, 1, 3)), ss)
```

```python
import functools

import jax
import jax.numpy as jnp
from jax import lax
from jax.experimental import pallas as pl
from jax.experimental.pallas import tpu as pltpu

D_MODEL = 1024
DEPTH = 2
D_A = 512
N_BLK_A = 8
BLK_A = 64
CONV_W = 4
C_RGLRU = 8.0
D_B = 512
N_HEADS_B = 4
DK_B = 128
DV_B = 128
P_IN = 2 * D_A + 4 * D_B
EPS = 1e-6
F_MIN = 1e-30

LANES = 128
SUBLANES = 8
VMEM_LIMIT = 60 * 1024 * 1024

BF16 = jnp.bfloat16
F32 = jnp.float32


def _dot(a, b):
    return jnp.dot(a, b, preferred_element_type=F32)


def _dot_nt(a, b):
    return lax.dot_general(a, b, (((1,), (1,)), ((), ())), preferred_element_type=F32)


def _dot_tn(a, b):
    return lax.dot_general(a, b, (((0,), (0,)), ((), ())), preferred_element_type=F32)


def _rms(x, w):
    return x * lax.rsqrt(jnp.mean(x * x, axis=-1, keepdims=True) + EPS) * w


def _silu(x):
    return x * jax.nn.sigmoid(x)


def _mod_kernel(c_ref, w_ref, b_ref, o_ref):
    o_ref[...] = _dot(_silu(c_ref[...]).astype(BF16), w_ref[...].astype(BF16)) + b_ref[...]


def _modulation(c_all, ada_w, ada_b):
    nb = c_all.shape[0]
    return pl.pallas_call(
        _mod_kernel,
        grid=(DEPTH, 3),
        in_specs=[
            pl.BlockSpec((nb, D_MODEL), lambda l, k: (0, 0)),
            pl.BlockSpec((None, D_MODEL, D_MODEL), lambda l, k: (l, 0, k)),
            pl.BlockSpec((None, 1, D_MODEL), lambda l, k: (l, 0, k)),
        ],
        out_specs=pl.BlockSpec((None, None, nb, D_MODEL), lambda l, k: (l, k, 0, 0)),
        out_shape=jax.ShapeDtypeStruct((DEPTH, 3, nb, D_MODEL), F32),
        name="adaln_mod",
    )(c_all, ada_w, ada_b)


TP = SUBLANES


def _sig_h(xh):
    return 0.5 * jnp.tanh(xh) + 0.5


def _silu_h(xh):
    return xh * jnp.tanh(xh) + xh


def _sqrt_pos(y):
    return jnp.where(y > 0.0, y * lax.rsqrt(y), 0.0)


def _decode_kernel(x_ref, mod_ref, h0_ref, conv0_ref, s0_ref, prew_ref, postw_ref, win_ref, convw_ref,
                   convb_ref, wg_ref, ba_ref, bx_ref, lam_ref, lbl_ref, onw_ref, wout_ref,
                   y_ref, hn_ref, convn_ref, sn_ref,
                   xt_s, hin_s, xpad_s, ga_s, a_s, u_s, q_s, kk_s, b_s, gb_s, oi_s, qe8_s, kdx8_s, v8_s, oc8_s,
                   *, nb, bt, tl):
    m_s = hin_s
    i = pl.program_id(0)
    n_blk = nb // bt
    layer = i // n_blk
    blk = i % n_blk
    rows = tl * nb
    tail = (CONV_W - 1) * nb
    nh = N_HEADS_B
    heads = [slice(hd * DK_B, (hd + 1) * DK_B) for hd in range(nh)]

    @pl.when(i == 0)
    def _():
        xt_s[...] = x_ref[...]

    @pl.when(blk == 0)
    def _():
        shift, scale = mod_ref[0], mod_ref[1]
        pm = prew_ref[...] * (1.0 + scale)
        for t in range(tl):
            xt = xt_s[t]
            inv = lax.rsqrt(jnp.mean(xt * xt, axis=-1, keepdims=True) + EPS)
            hin_s[pl.ds(t * nb, nb), :] = (xt * inv * pm + shift).astype(BF16)

        def proj(c0, width):
            return _dot(hin_s[...], win_ref[:, c0:c0 + width])
        xpad_s[pl.ds(0, tail), :] = conv0_ref[...].reshape(tail, D_A)
        xpad_s[pl.ds(tail, rows), :] = proj(0, D_A)
        ga_s[...] = proj(D_A, D_A)
        q_s[...] = proj(2 * D_A, D_B)
        kk_s[...] = proj(2 * D_A + D_B, D_B)
        iv = proj(2 * D_A + 2 * D_B, D_B)
        gb_s[...] = proj(2 * D_A + 3 * D_B, D_B)

        xc = convb_ref[...] + sum(xpad_s[pl.ds(k * nb, rows), :] * convw_ref[k:k + 1, :] for k in range(CONV_W))
        convn_ref[...] = xpad_s[pl.ds(rows, tail), :].reshape(CONV_W - 1, nb, D_A)
        xcb = xc.astype(BF16)
        for c in range(D_A // LANES):
            g2 = _dot(xcb[:, c * LANES:(c + 1) * LANES], wg_ref[c])
            a_s[:, c * LANES:(c + 1) * LANES] = g2[:, :LANES]
            u_s[:, c * LANES:(c + 1) * LANES] = g2[:, LANES:]
        hcoef = -0.5 * C_RGLRU * jax.nn.softplus(-lam_ref[...])
        h = h0_ref[...]
        for t in range(tl):
            sl = pl.ds(t * nb, nb)
            ig = _sig_h(u_s[sl, :] + bx_ref[...])
            log_a = hcoef * jnp.tanh(a_s[sl, :] + ba_ref[...]) + hcoef
            a = jnp.exp(log_a)
            u = _sqrt_pos(jnp.tanh(log_a) * (-1.0 - a * a)) * (ig * xc[t * nb:(t + 1) * nb])
            h = a * h + u
            m_s[sl, :D_A] = (h * _silu_h(ga_s[sl, :])).astype(BF16)
        hn_ref[...] = h

        logits = lbl_ref[...]
        pe = jnp.exp(logits - jnp.max(logits, axis=0, keepdims=True))
        probs = pe / jnp.sum(pe, axis=0, keepdims=True)
        lb = jnp.zeros((1, D_B), F32)
        for k in range(1, DEPTH):
            lb = lb + jnp.where(layer >= k, probs[k:k + 1], 0.0)
        c0 = 0.5 * (1.0 + lb)
        c1 = 0.5 * (1.0 - lb)
        bl = jnp.zeros((nb, D_B), F32)
        for t in range(tl):
            sl = pl.ds(t * nb, nb)
            p = c1 * jnp.tanh(kk_s[sl, :])
            bl = bl + jnp.log(jnp.maximum(c0 + p, F_MIN))
            b_s[sl, :] = bl
            kk_s[sl, :] = c1 - p

        for t in range(tl):
            st_ = pl.ds(t * nb, nb)
            acc = [jnp.zeros((nb, DV_B), F32) for _ in range(nh)]
            for s in range(t + 1):
                ss_ = pl.ds(s * nb, nb)
                w = q_s[st_, :] * kk_s[ss_, :]
                if s < t:
                    w = w * jnp.exp(b_s[st_, :] - b_s[ss_, :])
                for hd in range(nh):
                    acc[hd] = acc[hd] + jnp.sum(w[:, heads[hd]], axis=-1, keepdims=True) * iv[s * nb:(s + 1) * nb, heads[hd]]
            for hd in range(nh):
                oi_s[st_, heads[hd]] = acc[hd]

        zrow = jnp.zeros((nb, LANES), F32)
        ebl = jnp.exp(bl)
        e_hi = ebl.astype(BF16).astype(F32)
        e_mid = (ebl - e_hi).astype(BF16).astype(F32)
        e_lo = ebl - e_hi - e_mid
        for hd in range(nh):
            for t in range(TP):
                dst = pl.ds(t, nb, stride=TP)
                if t < tl:
                    st_ = pl.ds(t * nb, nb)
                    bt_ = b_s[st_, heads[hd]]
                    qe8_s[hd, dst, :] = q_s[st_, heads[hd]] * jnp.exp(bt_)
                    kdx8_s[hd, dst, :] = kk_s[st_, heads[hd]] * jnp.exp(bl[:, heads[hd]] - bt_)
                    v8_s[hd, dst, :] = iv[t * nb:(t + 1) * nb, heads[hd]]
                else:
                    qe8_s[hd, dst, :] = zrow
                    v8_s[hd, dst, :] = zrow
                    dec = (e_hi, e_mid, e_lo)
                    kdx8_s[hd, dst, :] = dec[t - tl][:, heads[hd]] if t - tl < len(dec) else zrow

    rid = lax.broadcasted_iota(jnp.int32, (TP, LANES), 0)
    ones_rows = jnp.where((rid >= tl) & (rid < tl + 3), 1.0, 0.0)
    for bl_ in range(bt):
        r8 = pl.ds(pl.multiple_of((blk * bt + bl_) * TP, TP), TP)
        for hd in range(nh):
            s_old = s0_ref[bl_, hd]
            oc8_s[hd, r8, :] = _dot(qe8_s[hd, r8, :].astype(BF16), s_old.astype(BF16))
            rhs = jnp.concatenate([v8_s[hd, r8, :], ones_rows], axis=1).astype(BF16)
            g = _dot_tn(kdx8_s[hd, r8, :].astype(BF16), rhs)
            sn_ref[bl_, hd] = s_old * g[:, DV_B:] + g[:, :DV_B]

    @pl.when(blk == n_blk - 1)
    def _():
        for t in range(tl):
            st_ = pl.ds(t * nb, nb)
            for hd in range(nh):
                oh = oi_s[st_, heads[hd]] + oc8_s[hd, pl.ds(t, nb, stride=TP), :]
                yb = _rms(oh, onw_ref[:, heads[hd]]) * _silu_h(gb_s[st_, heads[hd]])
                m_s[st_, D_A + hd * DV_B:D_A + (hd + 1) * DV_B] = yb.astype(BF16)
        mo = _dot(m_s[...], wout_ref[...])
        gp = postw_ref[...] * mod_ref[2]
        for t in range(tl):
            mt = mo[t * nb:(t + 1) * nb]
            inv = lax.rsqrt(jnp.mean(mt * mt, axis=-1, keepdims=True) + EPS)
            xt_s[t] = xt_s[t] + mt * inv * gp

    @pl.when(i == pl.num_programs(0) - 1)
    def _():
        y_ref[...] = xt_s[...]


def _decode(x_tm, mod, h0, conv0_tm, s_all, w, *, bt):
    tl, nb, _ = x_tm.shape
    assert nb % bt == 0 and nb % SUBLANES == 0 and CONV_W - 1 <= tl and tl + 3 <= TP
    rows = tl * nb
    n_blk = nb // bt

    def whole(a):
        nd = a.ndim
        return pl.BlockSpec(a.shape, lambda i: (0,) * nd, pipeline_mode=pl.Buffered(1))

    def per_layer(a):
        nd = a.ndim - 1
        return pl.BlockSpec((None,) + a.shape[1:], lambda i: (i // n_blk,) + (0,) * nd, pipeline_mode=pl.Buffered(1))
    state_spec = pl.BlockSpec((None, bt, N_HEADS_B, DK_B, DV_B), lambda i: (i // n_blk, i % n_blk, 0, 0, 0))
    args = [x_tm, mod, h0, conv0_tm, s_all, *w]
    in_specs = [whole(x_tm), per_layer(mod), per_layer(h0), per_layer(conv0_tm), state_spec]
    in_specs += [whole(a) if a.shape[0] == DEPTH and a.ndim == 2 else per_layer(a) for a in w]
    out_specs = [
        pl.BlockSpec((tl, nb, D_MODEL), lambda i: (0, 0, 0)),
        pl.BlockSpec((None, nb, D_A), lambda i: (i // n_blk, 0, 0)),
        pl.BlockSpec((None, CONV_W - 1, nb, D_A), lambda i: (i // n_blk, 0, 0, 0)),
        state_spec,
    ]
    out_shape = [
        jax.ShapeDtypeStruct((tl, nb, D_MODEL), F32),
        jax.ShapeDtypeStruct((DEPTH, nb, D_A), F32),
        jax.ShapeDtypeStruct((DEPTH, CONV_W - 1, nb, D_A), F32),
        jax.ShapeDtypeStruct(s_all.shape, F32),
    ]
    wide = lambda: pltpu.VMEM((rows, D_A), F32)
    slab8 = lambda: pltpu.VMEM((N_HEADS_B, nb * TP, LANES), F32)
    scratch = [
        pltpu.VMEM((tl, nb, D_MODEL), F32),
        pltpu.VMEM((rows, D_MODEL), BF16),
        pltpu.VMEM((rows + (CONV_W - 1) * nb, D_A), F32),
        wide(), wide(), wide(),
        wide(), wide(), wide(), wide(),
        wide(),
        slab8(), slab8(), slab8(), slab8(),
    ]
    return pl.pallas_call(
        functools.partial(_decode_kernel, nb=nb, bt=bt, tl=tl),
        grid=(DEPTH * n_blk,),
        in_specs=in_specs,
        out_specs=out_specs,
        out_shape=out_shape,
        scratch_shapes=scratch,
        compiler_params=pltpu.CompilerParams(dimension_semantics=("arbitrary",), vmem_limit_bytes=VMEM_LIMIT),
        name="decode_layers",
    )(*args)


CT = SUBLANES


def _prompt_kernel(x_ref, mod_ref, prew_ref, postw_ref, win_ref, convw_ref, convb_ref, wg_ref, ba_ref, bx_ref,
                   lam_ref, lbl_ref, onw_ref, wout_ref,
                   y_ref, hn_ref, convn_ref, sn_ref,
                   xt_s, hin_s, xpad_s, ccar_s, ga_s, a_s, u_s, xc_s, q_s, kk_s, b_s, gb_s, bl_s, hcar_s, st_s,
                   xl_s, qe_s, kd_s, v_s, oi_s,
                   *, bt, tl, n_levels):
    o_s = b_s
    m_s = hin_s
    j = pl.program_id(0)
    nj = pl.num_programs(0)
    rows = tl * bt
    ch = CT * bt
    n_ch = tl // CT
    nh = N_HEADS_B
    nt = D_MODEL // LANES
    tail = (CONV_W - 1) * bt

    @pl.when(j == 0)
    def _():
        hcar_s[...] = jnp.zeros_like(hcar_s)
        ccar_s[...] = jnp.zeros_like(ccar_s)
        st_s[...] = jnp.zeros_like(st_s)

    for c in range(n_ch):
        for jl in range(nt):
            for b in range(bt):
                xt_s[jl, pl.ds(c * ch + b, CT, stride=bt), :] = x_ref[b, c * CT:(c + 1) * CT, jl * LANES:(jl + 1) * LANES]

    ti = lax.broadcasted_iota(jnp.int32, (tl, tl), 0)
    si = lax.broadcasted_iota(jnp.int32, (tl, tl), 1)
    masks = []
    for lv in range(n_levels):
        h = CT << lv
        masks.append(((ti // (2 * h)) == (si // (2 * h))) & ((ti // h) % 2 == 1) & ((si // h) % 2 == 0))

    def bcast(v):
        return jnp.concatenate([v] * CT, axis=0)

    for l in range(DEPTH):
        shift, scale, gate = mod_ref[l, 0], mod_ref[l, 1], mod_ref[l, 2]
        pm = prew_ref[l] * (1.0 + scale)
        gp = postw_ref[l] * gate

        for c in range(n_ch):
            rs = pl.ds(c * ch, ch)
            ss = jnp.sum(sum(jnp.square(xt_s[jl, rs, :]) for jl in range(nt)), axis=-1, keepdims=True)
            inv = lax.rsqrt(ss * (1.0 / D_MODEL) + EPS)
            for jl in range(nt):
                ls = slice(jl * LANES, (jl + 1) * LANES)
                hin_s[rs, ls] = (xt_s[jl, rs, :] * inv * bcast(pm[:, ls]) + bcast(shift[:, ls])).astype(BF16)

        def proj(c0, width):
            return _dot(hin_s[...], win_ref[l, :, c0:c0 + width])
        for dst, c0_ in ((q_s, 2 * D_A), (kk_s, 2 * D_A + D_B), (v_s, 2 * D_A + 2 * D_B), (gb_s, 2 * D_A + 3 * D_B)):
            for h2 in range(0, nh, 2):
                z = proj(c0_ + h2 * DK_B, 2 * DK_B)
                dst[h2] = z[:, :DK_B]
                dst[h2 + 1] = z[:, DK_B:]

        logits = lbl_ref[...]
        pe = jnp.exp(logits - jnp.max(logits, axis=0, keepdims=True))
        probs = pe / jnp.sum(pe, axis=0, keepdims=True)
        lb = jnp.sum(probs[1:l + 1], axis=0, keepdims=True) if l > 0 else jnp.zeros((1, D_B), F32)
        c0 = 0.5 * (1.0 + lb)
        c1 = 0.5 * (1.0 - lb)
        bl = [jnp.zeros((bt, DK_B), F32) for _ in range(nh)]
        for t in range(tl):
            sl = pl.ds(t * bt, bt)
            for hd in range(nh):
                ls = slice(hd * DK_B, (hd + 1) * DK_B)
                p = c1[:, ls] * jnp.tanh(kk_s[hd, sl, :])
                bl[hd] = bl[hd] + jnp.log(jnp.maximum(c0[:, ls] + p, F_MIN))
                b_s[hd, sl, :] = bl[hd]
                kk_s[hd, sl, :] = c1[:, ls] - p
        for hd in range(nh):
            bl_s[hd] = jnp.exp(bl[hd])

        for c in range(n_ch):
            rs = pl.ds(c * ch, ch)
            for hd in range(nh):
                bc, qc, kc, vc = b_s[hd, rs, :], q_s[hd, rs, :], kk_s[hd, rs, :], v_s[hd, rs, :]
                qe_s[hd, rs, :] = qc * jnp.exp(bc)
                kd_s[hd, rs, :] = kc * jnp.exp(bcast(bl[hd]) - bc)
                near = []
                for t in range(CT):
                    ts = slice(t * bt, (t + 1) * bt)
                    acc = jnp.sum(qc[ts] * kc[ts], axis=-1, keepdims=True) * vc[ts]
                    for u in range(t):
                        us = slice(u * bt, (u + 1) * bt)
                        w = qc[ts] * kc[us] * jnp.exp(bc[ts] - bc[us])
                        acc = acc + jnp.sum(w, axis=-1, keepdims=True) * vc[us]
                    near.append(acc)
                oi_s[hd, rs, :] = jnp.concatenate(near, axis=0)
                t0 = c * CT
                for lv in range(n_levels):
                    h = CT << lv
                    mid = (t0 // (2 * h)) * 2 * h + h - 1
                    bmid = bcast(b_s[hd, pl.ds(mid * bt, bt), :])
                    xl_s[lv, hd, rs, :] = qc * jnp.exp(bc - bmid) if t0 % (2 * h) >= h else kc * jnp.exp(bmid - bc)

        xpad_s[pl.ds(0, tail), :] = ccar_s[l]
        xpad_s[pl.ds(tail, rows), :] = proj(0, D_A)
        ga_s[...] = proj(D_A, D_A)
        cw = convw_ref[l]
        for c in range(n_ch):
            xc = convb_ref[l] + sum(xpad_s[pl.ds(c * ch + k * bt, ch), :] * cw[k:k + 1, :] for k in range(CONV_W))
            xc_s[pl.ds(c * ch, ch), :] = xc
        ctail = xpad_s[pl.ds(rows, tail), :]
        ccar_s[l] = ctail
        for c in range(D_A // LANES):
            g2 = _dot(xc_s[:, c * LANES:(c + 1) * LANES].astype(BF16), wg_ref[l, c])
            a_s[:, c * LANES:(c + 1) * LANES] = g2[:, :LANES]
            u_s[:, c * LANES:(c + 1) * LANES] = g2[:, LANES:]
        hcoef = -0.5 * C_RGLRU * jax.nn.softplus(-lam_ref[l])
        ba, bx = ba_ref[l], bx_ref[l]

        def scan_chunk(c, h):
            for tp in range(CT // 2):
                yas = []
                for t2 in range(2):
                    sl = pl.ds(c * ch + (2 * tp + t2) * bt, bt)
                    i = _sig_h(u_s[sl, :] + bx)
                    log_a = hcoef * jnp.tanh(a_s[sl, :] + ba) + hcoef
                    a = jnp.exp(log_a)
                    u = _sqrt_pos(jnp.tanh(log_a) * (-1.0 - a * a)) * (i * xc_s[sl, :])
                    h = a * h + u
                    yas.append(h * _silu_h(ga_s[sl, :]))
                m_s[pl.ds(c * ch + 2 * tp * bt, 2 * bt), :D_A] = jnp.concatenate(yas, axis=0).astype(BF16)
            return h

        def chains(b):
            sl = pl.ds(b, tl, stride=bt)
            hs = range(nh)
            xs = [[xl_s[lv, hd, sl, :].astype(BF16) for lv in range(n_levels)] for hd in hs]
            vs = [v_s[hd, sl, :].astype(BF16) for hd in hs]
            sts = [st_s[l, b, hd] for hd in hs]
            grams = [[_dot_nt(xs[hd][lv], xs[hd][lv]) for lv in range(n_levels)] for hd in hs]
            upd = [_dot_tn(vs[hd], kd_s[hd, sl, :].astype(BF16)) for hd in hs]
            amats = []
            for hd in hs:
                amat = jnp.zeros((tl, tl), F32)
                for lv in range(n_levels):
                    amat = amat + jnp.where(masks[lv], grams[hd][lv], 0.0)
                amats.append(amat.astype(BF16))
            os_ = [_dot(amats[hd], vs[hd]) + _dot_nt(qe_s[hd, sl, :].astype(BF16), sts[hd].astype(BF16)) for hd in hs]
            for hd in hs:
                o_s[hd, sl, :] = (_rms(os_[hd] + oi_s[hd, sl, :], onw_ref[l][:, hd * DV_B:(hd + 1) * DV_B])
                                  * _silu_h(gb_s[hd, sl, :]))
                st_s[l, b, hd] = sts[hd] * bl_s[hd, b:b + 1, :] + upd[hd]

        h_last = hcar_s[l]
        for idx in range(max(bt, n_ch)):
            if idx < bt:
                chains(idx)
            if idx < n_ch:
                h_last = scan_chunk(idx, h_last)
        hcar_s[l] = h_last

        for hd in range(nh):
            m_s[:, D_A + hd * DV_B:D_A + (hd + 1) * DV_B] = o_s[hd].astype(BF16)

        for j2 in range(0, nt, 2):
            mo = _dot(m_s[...], wout_ref[l, :, j2 * LANES:(j2 + 2) * LANES])
            xl_s[j2 // nh, j2 % nh] = mo[:, :LANES]
            xl_s[(j2 + 1) // nh, (j2 + 1) % nh] = mo[:, LANES:]
        for c in range(n_ch):
            rs = pl.ds(c * ch, ch)
            ss = jnp.sum(sum(jnp.square(xl_s[jl // nh, jl % nh, rs, :]) for jl in range(nt)), axis=-1, keepdims=True)
            inv = lax.rsqrt(ss * (1.0 / D_MODEL) + EPS)
            for jl in range(nt):
                ls = slice(jl * LANES, (jl + 1) * LANES)
                xt_s[jl, rs, :] = xt_s[jl, rs, :] + xl_s[jl // nh, jl % nh, rs, :] * inv * bcast(gp[:, ls])

        @pl.when(j == nj - 1)
        def _():
            hn_ref[l] = h_last
            convn_ref[l] = ctail.reshape(CONV_W - 1, bt, D_A)
            for b in range(bt):
                for hd in range(nh):
                    sn_ref[l, b, hd] = st_s[l, b, hd].T

    for c in range(n_ch):
        for jl in range(nt):
            for b in range(bt):
                y_ref[b, c * CT:(c + 1) * CT, jl * LANES:(jl + 1) * LANES] = xt_s[jl, pl.ds(c * ch + b, CT, stride=bt), :]


def _prompt(x, mod, w, *, tl):
    B, L, _ = x.shape
    bt = B
    assert bt == SUBLANES and L % tl == 0 and tl % CT == 0
    n_levels = (tl // CT).bit_length() - 1
    assert (CT << n_levels) == tl and n_levels * N_HEADS_B >= D_MODEL // LANES
    rows = tl * bt
    tail = (CONV_W - 1) * bt

    def const(shape):
        nd = len(shape)
        return pl.BlockSpec(shape, lambda j: (0,) * nd, pipeline_mode=pl.Buffered(1))
    in_specs = [pl.BlockSpec((bt, tl, D_MODEL), lambda j: (0, j, 0)), const(mod.shape)] + [const(a.shape) for a in w]
    out_specs = [
        pl.BlockSpec((bt, tl, D_MODEL), lambda j: (0, j, 0)),
        pl.BlockSpec((DEPTH, bt, D_A), lambda j: (0, 0, 0)),
        pl.BlockSpec((DEPTH, CONV_W - 1, bt, D_A), lambda j: (0, 0, 0, 0)),
        pl.BlockSpec((DEPTH, bt, N_HEADS_B, DK_B, DV_B), lambda j: (0, 0, 0, 0, 0)),
    ]
    out_shape = [
        jax.ShapeDtypeStruct((B, L, D_MODEL), F32),
        jax.ShapeDtypeStruct((DEPTH, B, D_A), F32),
        jax.ShapeDtypeStruct((DEPTH, CONV_W - 1, B, D_A), F32),
        jax.ShapeDtypeStruct((DEPTH, B, N_HEADS_B, DK_B, DV_B), F32),
    ]
    slab = lambda n: pltpu.VMEM((n, rows, LANES), F32)
    wide = lambda: pltpu.VMEM((rows, D_A), F32)
    scratch = [
        slab(D_MODEL // LANES),
        pltpu.VMEM((rows, D_MODEL), BF16),
        pltpu.VMEM((rows + tail, D_A), F32),
        pltpu.VMEM((DEPTH, tail, D_A), F32),
        wide(), wide(), wide(), wide(),
        slab(N_HEADS_B), slab(N_HEADS_B), slab(N_HEADS_B), slab(N_HEADS_B),
        pltpu.VMEM((N_HEADS_B, bt, DK_B), F32),
        pltpu.VMEM((DEPTH, bt, D_A), F32),
        pltpu.VMEM((DEPTH, bt, N_HEADS_B, DV_B, DK_B), F32),
        pltpu.VMEM((n_levels, N_HEADS_B, rows, LANES), F32),
        slab(N_HEADS_B), slab(N_HEADS_B), slab(N_HEADS_B), slab(N_HEADS_B),
    ]
    body = functools.partial(_prompt_kernel, bt=bt, tl=tl, n_levels=n_levels)
    return pl.pallas_call(
        body,
        grid=(L // tl,),
        in_specs=in_specs,
        out_specs=out_specs,
        out_shape=out_shape,
        scratch_shapes=scratch,
        compiler_params=pltpu.CompilerParams(dimension_semantics=("arbitrary",), vmem_limit_bytes=VMEM_LIMIT),
        name="prompt_layers",
    )(x, mod, *w)


def _gate_weights(wa, wx):
    per = LANES // BLK_A

    def bd(w):
        w = w.reshape(D_A // LANES, per, BLK_A, BLK_A)
        eye = jnp.eye(per, dtype=w.dtype)
        return jnp.einsum('jpcd,pq->jpcqd', w, eye).reshape(D_A // LANES, LANES, LANES)
    return (0.5 * jnp.concatenate([bd(wa), bd(wx)], axis=-1)).astype(BF16)


def kernel(x_prompt, x_sample, state_rglru_h, state_rglru_conv, state_hgrn_S, c_prompt, c_sample, ada_w, ada_b,
           pre_norm_w, post_norm_w, w_in, conv_w, conv_b, rg_wa, rg_ba, rg_wx, rg_bx, rg_lambda, hg_lb_logits,
           hg_onorm_w, w_out):
    bp, bs = x_prompt.shape[0], x_sample.shape[0]
    mod = _modulation(jnp.concatenate([c_prompt, c_sample], axis=0), ada_w,
                      ada_b.reshape(DEPTH, 1, 3 * D_MODEL))
    half = jnp.full((D_A,), 0.5, F32)
    one = jnp.ones((D_A,), F32)
    col_scale = jnp.concatenate([one, half, one, half, one, half])
    w_in_bf, w_out_bf = (w_in * col_scale).astype(BF16), w_out.astype(BF16)
    wg = jnp.stack([_gate_weights(rg_wa[l], rg_wx[l]) for l in range(DEPTH)])
    row = lambda a: a[:, None, :]
    wp = (row(pre_norm_w), row(post_norm_w), w_in_bf, conv_w, row(conv_b), wg, row(0.5 * rg_ba), row(0.5 * rg_bx),
          row(rg_lambda), hg_lb_logits, row(hg_onorm_w), w_out_bf)
    yp, hp, cvp, sp = _prompt(x_prompt, mod[:, :, :bp], wp, tl=64)

    ys, hs, cvs, ss = _decode(jnp.transpose(x_sample, (1, 0, 2)), mod[:, :, bp:], state_rglru_h,
                              jnp.transpose(state_rglru_conv, (0, 2, 1, 3)), state_hgrn_S, wp, bt=16)
    return (yp, jnp.transpose(ys, (1, 0, 2)), hp, jnp.transpose(cvp, (0, 2, 1, 3)), sp,
            hs, jnp.transpose(cvs, (0, 2, 1, 3)), ss)
```
